```python
import jax, jax.numpy as jnp
from jax import lax
import numpy as np

D_MODEL = 1024
BATCH = 8
SEQ = 2048
DEPTH = 4

HEAD_DIM = 64
GRID_W = 64
MEM_LEN = 256
Q_BLOCK = 128
ROPE_THETA = 10000.0
EPS = 1e-6
NEG = -1e30

A_HEADS = 8
A_KV_HEADS = 2
B_HEADS = 8
B_KV_HEADS = 2
B_WINDOW = 128
C_HEADS = 8
C_Q_RANK = 256
C_KV_RANK = 128
C_NOPE = 64
C_ROPE = 32
C_V = 64
D_HEADS = 8
D_WIN_R = 8
D_WIN_C = 16
X_HEADS = 4
X_HEAD_DIM = 128
D_FF = -(-8 * D_MODEL // (3 * 256)) * 256

AB_SIZES = (A_HEADS * HEAD_DIM, A_KV_HEADS * HEAD_DIM, A_KV_HEADS * HEAD_DIM,
            B_HEADS * HEAD_DIM, B_KV_HEADS * HEAD_DIM, B_KV_HEADS * HEAD_DIM)
CD_SIZES = (C_Q_RANK, C_KV_RANK, C_ROPE,
            D_HEADS * HEAD_DIM, D_HEADS * HEAD_DIM, D_HEADS * HEAD_DIM)
IN_AB = sum(AB_SIZES)
IN_CD = sum(CD_SIZES)
MIX_AB = (A_HEADS + B_HEADS) * HEAD_DIM
MIX_CD = C_HEADS * C_V + D_HEADS * HEAD_DIM

kernel_name = "hybrid_gqa_swa_mla_natten_encoder"


def _split(z, sizes):
    idx = [int(v) for v in np.cumsum(sizes)[:-1]]
    return jnp.split(z, idx, axis=-1)


def rms_norm(x, g):
    xf = x.astype(jnp.float32)
    y = xf * lax.rsqrt(jnp.mean(xf * xf, axis=-1, keepdims=True) + EPS)
    return (y * g.astype(jnp.float32)).astype(x.dtype)


def rope_angles(pos, dim):
    inv = ROPE_THETA ** (-jnp.arange(0, dim, 2, dtype=jnp.float32) / dim)
    return pos.astype(jnp.float32)[:, None] * inv[None, :]


def apply_rope(x, ang):
    cos = jnp.cos(ang)[None, :, None, :]
    sin = jnp.sin(ang)[None, :, None, :]
    x1, x2 = jnp.split(x.astype(jnp.float32), 2, axis=-1)
    out = jnp.concatenate([x1 * cos - x2 * sin, x1 * sin + x2 * cos], axis=-1)
    return out.astype(x.dtype)


def blocked_dense_attention(q, k, v, scale):
    B, S = q.shape[0], q.shape[1]
    nb = S // Q_BLOCK
    qb = q.reshape(B, nb, Q_BLOCK, *q.shape[2:]).swapaxes(0, 1)

    def one_block(q_blk):
        s = jnp.einsum('bqhgd,bkhd->bhgqk', q_blk, k,
                       preferred_element_type=jnp.float32) * scale
        p = jax.nn.softmax(s, axis=-1).astype(v.dtype)
        return jnp.einsum('bhgqk,bkhd->bqhgd', p, v)

    out = lax.map(one_block, qb)
    return out.swapaxes(0, 1).reshape(B, S, -1)


def window_attention_with_sink(q, k, v, sink, scale):
    B, S, Hkv, G, d = q.shape
    nb = S // Q_BLOCK
    pad = ((0, 0), (Q_BLOCK, Q_BLOCK), (0, 0), (0, 0))
    kp = jnp.pad(k, pad).reshape(B, nb + 2, Q_BLOCK, Hkv, d)
    vp = jnp.pad(v, pad).reshape(B, nb + 2, Q_BLOCK, Hkv, d)
    kb = jnp.concatenate([kp[:, :-2], kp[:, 1:-1], kp[:, 2:]], axis=2)
    vb = jnp.concatenate([vp[:, :-2], vp[:, 1:-1], vp[:, 2:]], axis=2)
    qb = q.reshape(B, nb, Q_BLOCK, Hkv, G, d)
    s = jnp.einsum('bnqhgd,bnkhd->bnhgqk', qb, kb,
                   preferred_element_type=jnp.float32) * scale
    blk = jnp.arange(nb)[:, None, None] * Q_BLOCK
    q_abs = blk + jnp.arange(Q_BLOCK)[None, :, None]
    k_abs = blk - Q_BLOCK + jnp.arange(3 * Q_BLOCK)[None, None, :]
    valid = (jnp.abs(k_abs - q_abs) <= B_WINDOW) & (k_abs >= 0) & (k_abs < S)
    s = jnp.where(valid[None, :, None, None], s, NEG)
    sink_col = jnp.broadcast_to(sink.astype(jnp.float32).reshape(1, 1, Hkv, G, 1, 1),
                                s.shape[:-1] + (1,))
    p = jax.nn.softmax(jnp.concatenate([s, sink_col], axis=-1), axis=-1)[..., :-1]
    out = jnp.einsum('bnhgqk,bnkhd->bnqhgd', p.astype(v.dtype), vb)
    return out.reshape(B, S, Hkv * G * d)


def neighbourhood_attention(q, k, v, rpb, scale):
    B, S, H, d = q.shape
    rows = S // GRID_W
    wr = min(D_WIN_R, rows)
    c = jnp.arange(GRID_W)
    c0 = jnp.clip(c - D_WIN_C // 2, 0, GRID_W - D_WIN_C)
    col_valid = (c[None, :] >= c0[:, None]) & (c[None, :] < c0[:, None] + D_WIN_C)
    valid = jnp.tile(col_valid, (1, wr))
    col_idx = jnp.clip(c[None, :] - c[:, None] + (D_WIN_C - 1), 0, 2 * D_WIN_C - 2)
    qg = q.reshape(B, rows, GRID_W, H, d)
    kg = k.reshape(B, rows, GRID_W, H, d)
    vg = v.reshape(B, rows, GRID_W, H, d)
    rpb_f = rpb.astype(jnp.float32)

    def one_row(args):
        q_row, r = args
        r0 = jnp.clip(r - wr // 2, 0, rows - wr)
        k_win = lax.dynamic_slice_in_dim(kg, r0, wr, axis=1).reshape(B, wr * GRID_W, H, d)
        v_win = lax.dynamic_slice_in_dim(vg, r0, wr, axis=1).reshape(B, wr * GRID_W, H, d)
        s = jnp.einsum('bqhd,bkhd->bhqk', q_row, k_win,
                       preferred_element_type=jnp.float32) * scale
        row_idx = r0 + jnp.arange(wr) - r + (D_WIN_R - 1)
        bias = rpb_f[:, row_idx[None, :, None], col_idx[:, None, :]]
        s = jnp.where(valid, s + bias.reshape(H, GRID_W, wr * GRID_W)[None], NEG)
        p = jax.nn.softmax(s, axis=-1).astype(v.dtype)
        return jnp.einsum('bhqk,bkhd->bqhd', p, v_win)

    out = lax.map(one_row, (qg.swapaxes(0, 1), jnp.arange(rows)))
    return out.swapaxes(0, 1).reshape(B, S, H * d)


def mixer_ab(h, w_in, g_qa, g_ka, sink, w_out, ang_1d, ang_2d):
    B, S, _ = h.shape
    qa, ka, va, qb, kb, vb = _split(h @ w_in, AB_SIZES)
    qa = apply_rope(rms_norm(qa.reshape(B, S, A_HEADS, HEAD_DIM), g_qa), ang_2d)
    ka = apply_rope(rms_norm(ka.reshape(B, S, A_KV_HEADS, HEAD_DIM), g_ka), ang_2d)
    qa = qa.reshape(B, S, A_KV_HEADS, A_HEADS // A_KV_HEADS, HEAD_DIM)
    va = va.reshape(B, S, A_KV_HEADS, HEAD_DIM)
    oa = blocked_dense_attention(qa, ka, va, HEAD_DIM ** -0.5)
    qb = apply_rope(qb.reshape(B, S, B_HEADS, HEAD_DIM), ang_1d)
    kb = apply_rope(kb.reshape(B, S, B_KV_HEADS, HEAD_DIM), ang_1d)
    qb = qb.reshape(B, S, B_KV_HEADS, B_HEADS // B_KV_HEADS, HEAD_DIM)
    vb = vb.reshape(B, S, B_KV_HEADS, HEAD_DIM)
    ob = window_attention_with_sink(qb, kb, vb, sink, HEAD_DIM ** -0.5)
    return jnp.concatenate([oa, ob], axis=-1) @ w_out


def mixer_cd(h, w_in, g_cq, g_ckv, w_uq, w_ukv, rpb, w_out, ang_c):
    B, S, _ = h.shape
    cq, ckv, kr, qd, kd, vd = _split(h @ w_in, CD_SIZES)
    q = (rms_norm(cq, g_cq) @ w_uq).reshape(B, S, C_HEADS, C_NOPE + C_ROPE)
    q_nope, q_rope = jnp.split(q, [C_NOPE], axis=-1)
    q_rope = apply_rope(q_rope, ang_c)
    kv = (rms_norm(ckv, g_ckv) @ w_ukv).reshape(B, S, C_HEADS, C_NOPE + C_V)
    k_nope, v_c = jnp.split(kv, [C_NOPE], axis=-1)
    k_rope = apply_rope(kr.reshape(B, S, 1, C_ROPE), ang_c)
    qc = jnp.concatenate([q_nope, q_rope], axis=-1)[:, :, :, None, :]
    kc = jnp.concatenate([k_nope, jnp.broadcast_to(k_rope, (B, S, C_HEADS, C_ROPE))], axis=-1)
    oc = blocked_dense_attention(qc, kc, v_c, (C_NOPE + C_ROPE) ** -0.5)
    od = neighbourhood_attention(qd.reshape(B, S, D_HEADS, HEAD_DIM),
                                 kd.reshape(B, S, D_HEADS, HEAD_DIM),
                                 vd.reshape(B, S, D_HEADS, HEAD_DIM), rpb, HEAD_DIM ** -0.5)
    return jnp.concatenate([oc, od], axis=-1) @ w_out


def memory_cross_attention(h, m, w_q, w_kv, w_o):
    B, S, _ = h.shape
    q = (h @ w_q).reshape(B, S, X_HEADS, X_HEAD_DIM)
    k, v = jnp.split(m @ w_kv, 2, axis=-1)
    k = k.reshape(B, m.shape[1], X_HEADS, X_HEAD_DIM)
    v = v.reshape(B, m.shape[1], X_HEADS, X_HEAD_DIM)
    s = jnp.einsum('bqhd,bkhd->bhqk', q, k, preferred_element_type=jnp.float32) * X_HEAD_DIM ** -0.5
    p = jax.nn.softmax(s, axis=-1).astype(v.dtype)
    o = jnp.einsum('bhqk,bkhd->bqhd', p, v).reshape(B, S, X_HEADS * X_HEAD_DIM)
    return o @ w_o


def swiglu(h, w_gate_up, w_down):
    g, u = jnp.split(h @ w_gate_up, 2, axis=-1)
    return (jax.nn.silu(g) * u) @ w_down


def setup_inputs(seed: int = 0) -> dict:
    key = jax.random.key(seed)
    ks = jax.random.split(key, 24)
    n_even = (DEPTH + 1) // 2
    n_odd = DEPTH // 2

    def w(k, shape, fan_in):
        return jax.random.normal(k, shape, jnp.float32) * fan_in ** -0.5

    def gain(k, shape):
        return 1.0 + 0.05 * jax.random.normal(k, shape, jnp.float32)

    return {
        'x': jax.random.normal(ks[0], (BATCH, SEQ, D_MODEL), jnp.float32),
        'mem': jax.random.normal(ks[1], (BATCH, MEM_LEN, D_MODEL), jnp.float32),
        'g_mix': gain(ks[2], (DEPTH, D_MODEL)),
        'w_in_ab': w(ks[3], (n_even, D_MODEL, IN_AB), D_MODEL),
        'g_qa': gain(ks[4], (n_even, HEAD_DIM)),
        'g_ka': gain(ks[5], (n_even, HEAD_DIM)),
        'sink_b': jax.random.normal(ks[6], (n_even, B_HEADS), jnp.float32),
        'w_out_ab': w(ks[7], (n_even, MIX_AB, D_MODEL), MIX_AB),
        'w_in_cd': w(ks[8], (n_odd, D_MODEL, IN_CD), D_MODEL),
        'g_cq': gain(ks[9], (n_odd, C_Q_RANK)),
        'g_ckv': gain(ks[10], (n_odd, C_KV_RANK)),
        'w_uq': w(ks[11], (n_odd, C_Q_RANK, C_HEADS * (C_NOPE + C_ROPE)), C_Q_RANK),
        'w_ukv': w(ks[12], (n_odd, C_KV_RANK, C_HEADS * (C_NOPE + C_V)), C_KV_RANK),
        'rpb_d': 0.1 * jax.random.normal(ks[13], (n_odd, D_HEADS, 2 * D_WIN_R - 1, 2 * D_WIN_C - 1), jnp.float32),
        'w_out_cd': w(ks[14], (n_odd, MIX_CD, D_MODEL), MIX_CD),
        'g_xq': gain(ks[15], (DEPTH, D_MODEL)),
        'g_mem': gain(ks[16], (DEPTH, D_MODEL)),
        'w_xq': w(ks[17], (DEPTH, D_MODEL, X_HEADS * X_HEAD_DIM), D_MODEL),
        'w_xkv': w(ks[18], (DEPTH, D_MODEL, 2 * X_HEADS * X_HEAD_DIM), D_MODEL),
        'w_xo': w(ks[19], (DEPTH, X_HEADS * X_HEAD_DIM, D_MODEL), X_HEADS * X_HEAD_DIM),
        'g_ffn': gain(ks[20], (DEPTH, D_MODEL)),
        'w_gate_up': w(ks[21], (DEPTH, D_MODEL, 2 * D_FF), D_MODEL),
        'w_down': w(ks[22], (DEPTH, D_FF, D_MODEL), D_FF),
        'g_final': gain(ks[23], (D_MODEL,)),
    }


def reference(x, mem, g_mix, w_in_ab, g_qa, g_ka, sink_b, w_out_ab, w_in_cd, g_cq, g_ckv,
              w_uq, w_ukv, rpb_d, w_out_cd, g_xq, g_mem, w_xq, w_xkv, w_xo, g_ffn,
              w_gate_up, w_down, g_final):
    S = x.shape[1]
    pos = jnp.arange(S)
    row = pos // GRID_W
    col = pos % GRID_W
    ang_1d = rope_angles(pos, HEAD_DIM)
    ang_2d = jnp.concatenate([rope_angles(row, HEAD_DIM // 2),
                              rope_angles(col, HEAD_DIM // 2)], axis=-1)
    ang_c = rope_angles(pos, C_ROPE)
    for i in range(DEPTH):
        j = i // 2
        h = rms_norm(x, g_mix[i])
        if i % 2 == 0:
            x = x + mixer_ab(h, w_in_ab[j], g_qa[j], g_ka[j], sink_b[j], w_out_ab[j],
                             ang_1d, ang_2d)
        else:
            x = x + mixer_cd(h, w_in_cd[j], g_cq[j], g_ckv[j], w_uq[j], w_ukv[j],
                             rpb_d[j], w_out_cd[j], ang_c)
        x = x + memory_cross_attention(rms_norm(x, g_xq[i]), rms_norm(mem, g_mem[i]),
                                       w_xq[i], w_xkv[i], w_xo[i])
        x = x + swiglu(rms_norm(x, g_ffn[i]), w_gate_up[i], w_down[i])
    return rms_norm(x, g_final)
```

```python
import functools

import numpy as np
import jax
import jax.numpy as jnp
from jax import lax
from jax.experimental import pallas as pl
from jax.experimental.pallas import tpu as pltpu

D_MODEL = 1024
SEQ = 2048
HEAD_DIM = 64
GRID_W = 64
MEM_LEN = 256
ROPE_THETA = 10000.0
EPS = 1e-6
NEG = -1e30

A_HEADS = 8
B_HEADS = 8
B_WINDOW = 128
C_HEADS = 8
C_Q_RANK = 256
C_KV_RANK = 128
C_NOPE = 64
C_ROPE = 32
C_V = 64
D_HEADS = 8
D_WIN_R = 8
D_WIN_C = 16
X_HEADS = 4
X_HEAD_DIM = 128
D_FF = 2816

LANES = 128
TM = 256
TQ_DENSE = 256
FF_CHUNK = 256
VMEM_LIMIT = 56 * 1024 * 1024

F32 = jnp.float32
BF16 = jnp.bfloat16

_NT = (((1,), (1,)), ((), ()))


def _dot(a, b):
    return jnp.dot(a, b, preferred_element_type=F32)


def _dot_nt(a, b):
    return lax.dot_general(a, b, _NT, preferred_element_type=F32)


def _rms_rows(xf, g):
    return xf * lax.rsqrt(jnp.mean(xf * xf, axis=-1, keepdims=True) + EPS) * g


def _lane_iota(shape):
    return lax.broadcasted_iota(jnp.int32, shape, len(shape) - 1)


def _rope_chunk(x, cos, sin_signed, first_mask, half):
    rot = jnp.where(first_mask, pltpu.roll(x, LANES - half, 1), pltpu.roll(x, half, 1))
    return x * cos + rot * sin_signed


def _half_masks(dtype):
    lane = _lane_iota((1, LANES))
    return (lane < 64).astype(dtype), (lane >= 64).astype(dtype)


def _softmax_parts(s):
    m = jnp.max(s, axis=-1, keepdims=True)
    p = jnp.exp(s - m)
    return p, jnp.sum(p, axis=-1, keepdims=True), m


def _pair_out(p_a, l_a, p_b, l_b, v_lo, v_hi):
    o = _dot(p_a.astype(BF16), v_lo) + _dot(p_b.astype(BF16), v_hi)
    lane = _lane_iota(o.shape)
    return o * jnp.where(lane < 64, 1.0 / l_a, 1.0 / l_b)


def _inproj_ab_kernel(x_ref, g_ref, w_ref, gq_ref, gk_ref, bd_ref, tab_ref,
                      qa_ref, ka_ref, va_ref, qb_ref, kb_ref, vb_ref):
    h = _rms_rows(x_ref[...], g_ref[...]).astype(BF16)
    z = _dot(h, w_ref[...])
    lane = _lane_iota((TM, LANES))
    first = (lane & 63) < 32
    lo = lane < 64
    bd = bd_ref[...]

    def head_norm(zc, gain):
        zz = zc * zc
        hi = zz.astype(BF16)
        rest = (zz - hi.astype(F32)).astype(BF16)
        ss = _dot(hi, bd) + _dot(rest, bd)
        return zc * lax.rsqrt(ss * (1.0 / HEAD_DIM) + EPS) * gain

    def dup(c):
        r = pltpu.roll(c, 64, 1)
        return jnp.where(lo, c, r), jnp.where(lo, r, c)

    def chunk(i):
        return z[:, i * LANES:(i + 1) * LANES]

    for c in range(4):
        n = head_norm(chunk(c), gq_ref[...])
        qa_ref[:, c * LANES:(c + 1) * LANES] = _rope_chunk(
            n, tab_ref[0], tab_ref[1], first, 32).astype(BF16)
    k = _rope_chunk(head_norm(chunk(4), gk_ref[...]), tab_ref[2], tab_ref[3], first, 32)
    k0, k1 = dup(k)
    ka_ref[:, 0:LANES] = k0.astype(BF16)
    ka_ref[:, LANES:2 * LANES] = k1.astype(BF16)
    v0, v1 = dup(chunk(5))
    va_ref[:, 0:LANES] = v0.astype(BF16)
    va_ref[:, LANES:2 * LANES] = v1.astype(BF16)
    for c in range(4):
        qb_ref[:, c * LANES:(c + 1) * LANES] = _rope_chunk(
            chunk(6 + c), tab_ref[4], tab_ref[5], first, 32).astype(BF16)
    k = _rope_chunk(chunk(10), tab_ref[6], tab_ref[7], first, 32)
    k0, k1 = dup(k)
    kb_ref[:, 0:LANES] = k0.astype(BF16)
    kb_ref[:, LANES:2 * LANES] = k1.astype(BF16)
    v0, v1 = dup(chunk(11))
    vb_ref[:, 0:LANES] = v0.astype(BF16)
    vb_ref[:, LANES:2 * LANES] = v1.astype(BF16)


def _inproj_ab(x, g, w, gq, gk, bd, tabs):
    t = x.shape[0]
    n_pos = SEQ // TM
    row = lambda i: (i, 0)
    const = lambda i: (0, 0)
    outs = [(t, 512), (t, 256), (t, 256), (t, 512), (t, 256), (t, 256)]
    return pl.pallas_call(
        _inproj_ab_kernel,
        grid=(t // TM,),
        in_specs=[
            pl.BlockSpec((TM, D_MODEL), row),
            pl.BlockSpec((1, D_MODEL), const),
            pl.BlockSpec(w.shape, const),
            pl.BlockSpec((1, LANES), const),
            pl.BlockSpec((1, LANES), const),
            pl.BlockSpec((LANES, LANES), const),
            pl.BlockSpec((8, TM, LANES), lambda i: (0, i % n_pos, 0)),
        ],
        out_specs=[pl.BlockSpec((TM, n), row) for _, n in outs],
        out_shape=[jax.ShapeDtypeStruct(s, BF16) for s in outs],
        compiler_params=pltpu.CompilerParams(
            dimension_semantics=("arbitrary",), vmem_limit_bytes=VMEM_LIMIT),
        name="inproj_ab",
    )(x, g, w, gq, gk, bd, tabs)


def _attn_a_kernel(q_ref, k_ref, v_ref, o_ref):
    m_lo, m_hi = _half_masks(BF16)
    for kvh in range(2):
        k2 = k_ref[0, :, kvh * LANES:(kvh + 1) * LANES]
        v2 = v_ref[0, :, kvh * LANES:(kvh + 1) * LANES]
        v_lo, v_hi = v2 * m_lo, v2 * m_hi
        for c in (2 * kvh, 2 * kvh + 1):
            qc = q_ref[0, :, c * LANES:(c + 1) * LANES]
            p_a, l_a, _ = _softmax_parts(_dot_nt(qc * m_lo, k2))
            p_b, l_b, _ = _softmax_parts(_dot_nt(qc * m_hi, k2))
            o_ref[0, :, c * LANES:(c + 1) * LANES] = _pair_out(
                p_a, l_a, p_b, l_b, v_lo, v_hi).astype(BF16)


def _attn_a(q, k, v):
    b = q.shape[0]
    return pl.pallas_call(
        _attn_a_kernel,
        grid=(b, SEQ // TQ_DENSE),
        in_specs=[
            pl.BlockSpec((1, TQ_DENSE, 512), lambda i, j: (i, j, 0)),
            pl.BlockSpec((1, SEQ, 256), lambda i, j: (i, 0, 0)),
            pl.BlockSpec((1, SEQ, 256), lambda i, j: (i, 0, 0)),
        ],
        out_specs=pl.BlockSpec((1, TQ_DENSE, 512), lambda i, j: (i, j, 0)),
        out_shape=jax.ShapeDtypeStruct((b, SEQ, 512), BF16),
        compiler_params=pltpu.CompilerParams(
            dimension_semantics=("arbitrary", "arbitrary"), vmem_limit_bytes=VMEM_LIMIT),
        name="attn_a",
    )(q, k, v)


B_BLOCK = 128
B_KEYS = 3 * B_BLOCK


def _attn_b_kernel(sink_ref, q_ref, k_ref, v_ref, o_ref):
    n = pl.program_id(1)
    start = pl.multiple_of(jnp.clip((n - 1) * B_BLOCK, 0, SEQ - B_KEYS), B_BLOCK)
    q_pos = n * B_BLOCK + lax.broadcasted_iota(jnp.int32, (B_BLOCK, B_KEYS), 0)
    k_pos = start + lax.broadcasted_iota(jnp.int32, (B_BLOCK, B_KEYS), 1)
    delta = k_pos - q_pos
    valid = (delta <= B_WINDOW) & (delta >= -B_WINDOW)
    m_lo, m_hi = _half_masks(BF16)

    def head(qh, k2, sink):
        s = jnp.where(valid, _dot_nt(qh, k2), NEG)
        m = jnp.maximum(jnp.max(s, axis=-1, keepdims=True), sink)
        p = jnp.exp(s - m)
        return p, jnp.sum(p, axis=-1, keepdims=True) + jnp.exp(sink - m)

    for kvh in range(2):
        k2 = k_ref[0, pl.ds(start, B_KEYS), kvh * LANES:(kvh + 1) * LANES]
        v2 = v_ref[0, pl.ds(start, B_KEYS), kvh * LANES:(kvh + 1) * LANES]
        v_lo, v_hi = v2 * m_lo, v2 * m_hi
        for c in (2 * kvh, 2 * kvh + 1):
            qc = q_ref[0, :, c * LANES:(c + 1) * LANES]
            p_a, l_a = head(qc * m_lo, k2, sink_ref[2 * c])
            p_b, l_b = head(qc * m_hi, k2, sink_ref[2 * c + 1])
            o_ref[0, :, c * LANES:(c + 1) * LANES] = _pair_out(
                p_a, l_a, p_b, l_b, v_lo, v_hi).astype(BF16)


def _attn_b(sink, q, k, v):
    b = q.shape[0]
    return pl.pallas_call(
        _attn_b_kernel,
        grid=(b, SEQ // B_BLOCK),
        in_specs=[
            pl.BlockSpec(memory_space=pltpu.SMEM),
            pl.BlockSpec((1, B_BLOCK, 512), lambda i, j: (i, j, 0)),
            pl.BlockSpec((1, SEQ, 256), lambda i, j: (i, 0, 0)),
            pl.BlockSpec((1, SEQ, 256), lambda i, j: (i, 0, 0)),
        ],
        out_specs=pl.BlockSpec((1, B_BLOCK, 512), lambda i, j: (i, j, 0)),
        out_shape=jax.ShapeDtypeStruct((b, SEQ, 512), BF16),
        compiler_params=pltpu.CompilerParams(
            dimension_semantics=("arbitrary", "arbitrary"), vmem_limit_bytes=VMEM_LIMIT),
        name="attn_b",
    )(sink, q, k, v)


def _inproj_cd_kernel(x_ref, g_ref, w_ref, gcq_ref, gckv_ref, wuq_ref, wuk_ref, wuv_ref,
                      tab_ref, qc_ref, kc_ref, vc_ref, qd_ref, kd_ref, vd_ref):
    h = _rms_rows(x_ref[...], g_ref[...]).astype(BF16)
    z = _dot(h, w_ref[...])
    lane = _lane_iota((TM, LANES))
    first = lane < 80
    cq = _rms_rows(z[:, 0:256], gcq_ref[...]).astype(BF16)
    ckv = _rms_rows(z[:, 256:384], gckv_ref[...]).astype(BF16)
    q = _dot(cq, wuq_ref[...])
    kn = _dot(ckv, wuk_ref[...])
    kr = _rope_chunk(z[:, 384:512], tab_ref[2], tab_ref[3], first, 16)
    for hd in range(C_HEADS):
        sl = slice(hd * LANES, (hd + 1) * LANES)
        qc_ref[:, sl] = _rope_chunk(q[:, sl], tab_ref[0], tab_ref[1], first, 16).astype(BF16)
        kc_ref[:, sl] = (kn[:, sl] + kr).astype(BF16)
    vc_ref[...] = _dot(ckv, wuv_ref[...]).astype(BF16)
    qd_ref[...] = (z[:, 512:1024] * (HEAD_DIM ** -0.5)).astype(BF16)
    kd_ref[...] = z[:, 1024:1536].astype(BF16)
    vd_ref[...] = z[:, 1536:2048].astype(BF16)


def _inproj_cd(x, g, w, gcq, gckv, wuq, wuk, wuv, tabs):
    t = x.shape[0]
    n_pos = SEQ // TM
    row = lambda i: (i, 0)
    const = lambda i: (0, 0)
    outs = [(t, 1024), (t, 1024), (t, 512), (t, 512), (t, 512), (t, 512)]
    return pl.pallas_call(
        _inproj_cd_kernel,
        grid=(t // TM,),
        in_specs=[
            pl.BlockSpec((TM, D_MODEL), row),
            pl.BlockSpec((1, D_MODEL), const),
            pl.BlockSpec(w.shape, const),
            pl.BlockSpec((1, C_Q_RANK), const),
            pl.BlockSpec((1, C_KV_RANK), const),
            pl.BlockSpec(wuq.shape, const),
            pl.BlockSpec(wuk.shape, const),
            pl.BlockSpec(wuv.shape, const),
            pl.BlockSpec((4, TM, LANES), lambda i: (0, i % n_pos, 0)),
        ],
        out_specs=[pl.BlockSpec((TM, n), row) for _, n in outs],
        out_shape=[jax.ShapeDtypeStruct(s, BF16) for s in outs],
        compiler_params=pltpu.CompilerParams(
            dimension_semantics=("arbitrary",), vmem_limit_bytes=VMEM_LIMIT),
        name="inproj_cd",
    )(x, g, w, gcq, gckv, wuq, wuk, wuv, tabs)


def _attn_c_kernel(q_ref, k_ref, v_ref, o_ref):
    m_lo, m_hi = _half_masks(BF16)
    for c in range(4):
        v2 = v_ref[0, :, c * LANES:(c + 1) * LANES]
        v_lo, v_hi = v2 * m_lo, v2 * m_hi
        ha, hb = 2 * c, 2 * c + 1
        p_a, l_a, _ = _softmax_parts(_dot_nt(q_ref[0, :, ha * LANES:(ha + 1) * LANES],
                                             k_ref[0, :, ha * LANES:(ha + 1) * LANES]))
        p_b, l_b, _ = _softmax_parts(_dot_nt(q_ref[0, :, hb * LANES:(hb + 1) * LANES],
                                             k_ref[0, :, hb * LANES:(hb + 1) * LANES]))
        o_ref[0, :, c * LANES:(c + 1) * LANES] = _pair_out(
            p_a, l_a, p_b, l_b, v_lo, v_hi).astype(BF16)


def _attn_c(q, k, v):
    b = q.shape[0]
    return pl.pallas_call(
        _attn_c_kernel,
        grid=(b, SEQ // TQ_DENSE),
        in_specs=[
            pl.BlockSpec((1, TQ_DENSE, 1024), lambda i, j: (i, j, 0)),
            pl.BlockSpec((1, SEQ, 1024), lambda i, j: (i, 0, 0)),
            pl.BlockSpec((1, SEQ, 512), lambda i, j: (i, 0, 0)),
        ],
        out_specs=pl.BlockSpec((1, TQ_DENSE, 512), lambda i, j: (i, j, 0)),
        out_shape=jax.ShapeDtypeStruct((b, SEQ, 512), BF16),
        compiler_params=pltpu.CompilerParams(
            dimension_semantics=("arbitrary", "arbitrary"), vmem_limit_bytes=VMEM_LIMIT),
        name="attn_c",
    )(q, k, v)


N_ROWS = SEQ // GRID_W
D_KEYS = D_WIN_R * GRID_W
N_DR = 2 * D_WIN_R - 1
N_DC = 2 * D_WIN_C - 1


def _bias_table_kernel(rpb_ref, tab_ref):
    h = pl.program_id(0)
    shape = (GRID_W, LANES)
    qc = lax.broadcasted_iota(jnp.int32, shape, 0)
    lane = lax.broadcasted_iota(jnp.int32, shape, 1)
    kc = lane & (GRID_W - 1)
    upper = lane >= GRID_W
    idx = jnp.clip(kc - qc + (D_WIN_C - 1), 0, N_DC - 1)
    c0 = jnp.clip(qc - D_WIN_C // 2, 0, GRID_W - D_WIN_C)
    valid = (kc >= c0) & (kc < c0 + D_WIN_C)
    base = h * (N_DR * N_DC)
    for d in range(N_DR - 1):
        acc = jnp.zeros(shape, F32)
        for j in range(N_DC):
            lo_v = rpb_ref[base + d * N_DC + j]
            hi_v = rpb_ref[base + (d + 1) * N_DC + j]
            acc = jnp.where(idx == j, jnp.where(upper, hi_v, lo_v), acc)
        tab_ref[0, d] = jnp.where(valid, acc, NEG)


def _bias_table(rpb_flat):
    return pl.pallas_call(
        _bias_table_kernel,
        grid=(D_HEADS,),
        in_specs=[pl.BlockSpec(memory_space=pltpu.SMEM)],
        out_specs=pl.BlockSpec((1, N_DR - 1, GRID_W, LANES), lambda i: (i, 0, 0, 0)),
        out_shape=jax.ShapeDtypeStruct((D_HEADS, N_DR - 1, GRID_W, LANES), F32),
        compiler_params=pltpu.CompilerParams(dimension_semantics=("arbitrary",)),
        name="bias_table",
    )(rpb_flat)


def _attn_d_kernel(q_ref, k_ref, v_ref, tab_ref, o_ref):
    r = pl.program_id(1)
    r0 = jnp.clip(r - D_WIN_R // 2, 0, N_ROWS - D_WIN_R)
    start = pl.multiple_of(r0 * GRID_W, GRID_W)
    d0 = r0 - r + (D_WIN_R - 1)
    m_lo, m_hi = _half_masks(BF16)

    def head(qh, kwin, hd):
        bias = jnp.concatenate([tab_ref[hd, d0 + 2 * i] for i in range(D_WIN_R // 2)], axis=1)
        p, l, _ = _softmax_parts(_dot_nt(qh, kwin) + bias)
        return p, l

    for c in range(4):
        kwin = k_ref[0, pl.ds(start, D_KEYS), c * LANES:(c + 1) * LANES]
        v2 = v_ref[0, pl.ds(start, D_KEYS), c * LANES:(c + 1) * LANES]
        v_lo, v_hi = v2 * m_lo, v2 * m_hi
        qc = q_ref[0, :, c * LANES:(c + 1) * LANES]
        p_a, l_a = head(qc * m_lo, kwin, 2 * c)
        p_b, l_b = head(qc * m_hi, kwin, 2 * c + 1)
        o_ref[0, :, c * LANES:(c + 1) * LANES] = _pair_out(
            p_a, l_a, p_b, l_b, v_lo, v_hi).astype(BF16)


def _attn_d(q, k, v, tab):
    b = q.shape[0]
    return pl.pallas_call(
        _attn_d_kernel,
        grid=(b, N_ROWS),
        in_specs=[
            pl.BlockSpec((1, GRID_W, 512), lambda i, j: (i, j, 0)),
            pl.BlockSpec((1, SEQ, 512), lambda i, j: (i, 0, 0)),
            pl.BlockSpec((1, SEQ, 512), lambda i, j: (i, 0, 0)),
            pl.BlockSpec(tab.shape, lambda i, j: (0, 0, 0, 0)),
        ],
        out_specs=pl.BlockSpec((1, GRID_W, 512), lambda i, j: (i, j, 0)),
        out_shape=jax.ShapeDtypeStruct((b, SEQ, 512), BF16),
        compiler_params=pltpu.CompilerParams(
            dimension_semantics=("arbitrary", "arbitrary"), vmem_limit_bytes=VMEM_LIMIT),
        name="attn_d",
    )(q, k, v, tab)


def _norm_proj_kernel(x_ref, g_ref, w_ref, o_ref):
    h = _rms_rows(x_ref[...], g_ref[...]).astype(BF16)
    o_ref[...] = _dot(h, w_ref[...]).astype(BF16)


def _norm_proj(x, g, w):
    t = x.shape[0]
    n = w.shape[1]
    return pl.pallas_call(
        _norm_proj_kernel,
        grid=(t // TM,),
        in_specs=[
            pl.BlockSpec((TM, D_MODEL), lambda i: (i, 0)),
            pl.BlockSpec((1, D_MODEL), lambda i: (0, 0)),
            pl.BlockSpec(w.shape, lambda i: (0, 0)),
        ],
        out_specs=pl.BlockSpec((TM, n), lambda i: (i, 0)),
        out_shape=jax.ShapeDtypeStruct((t, n), BF16),
        compiler_params=pltpu.CompilerParams(
            dimension_semantics=("arbitrary",), vmem_limit_bytes=VMEM_LIMIT),
        name="mem_kv_proj",
    )(x, g, w)


def _cross_kernel(x_ref, a1_ref, a2_ref, wo1_ref, wo2_ref, g_ref, wq_ref, kv_ref, wxo_ref, o_ref):
    x1 = x_ref[...] + _dot(a1_ref[...], wo1_ref[...]) + _dot(a2_ref[...], wo2_ref[...])
    h = _rms_rows(x1, g_ref[...]).astype(BF16)
    q = (_dot(h, wq_ref[...]) * (X_HEAD_DIM ** -0.5)).astype(BF16)
    heads = []
    for hd in range(X_HEADS):
        sl = slice(hd * LANES, (hd + 1) * LANES)
        k = kv_ref[0, :, sl]
        v = kv_ref[0, :, X_HEADS * LANES + hd * LANES:X_HEADS * LANES + (hd + 1) * LANES]
        p, l, _ = _softmax_parts(_dot_nt(q[:, sl], k))
        heads.append((_dot(p.astype(BF16), v) * (1.0 / l)).astype(BF16))
    o = jnp.concatenate(heads, axis=1)
    o_ref[...] = x1 + _dot(o, wxo_ref[...])


def _cross(x, a1, a2, wo1, wo2, g, wq, kv, wxo):
    t = x.shape[0]
    per_b = SEQ // TM
    row = lambda i: (i, 0)
    const = lambda i: (0, 0)
    return pl.pallas_call(
        _cross_kernel,
        grid=(t // TM,),
        in_specs=[
            pl.BlockSpec((TM, D_MODEL), row),
            pl.BlockSpec((TM, 512), row),
            pl.BlockSpec((TM, 512), row),
            pl.BlockSpec(wo1.shape, const),
            pl.BlockSpec(wo2.shape, const),
            pl.BlockSpec((1, D_MODEL), const),
            pl.BlockSpec(wq.shape, const),
            pl.BlockSpec((1, MEM_LEN, 2 * X_HEADS * X_HEAD_DIM), lambda i: (i // per_b, 0, 0)),
            pl.BlockSpec(wxo.shape, const),
        ],
        out_specs=pl.BlockSpec((TM, D_MODEL), row),
        out_shape=jax.ShapeDtypeStruct((t, D_MODEL), F32),
        compiler_params=pltpu.CompilerParams(
            dimension_semantics=("arbitrary",), vmem_limit_bytes=VMEM_LIMIT),
        name="outproj_cross",
    )(x, a1, a2, wo1, wo2, g, wq, kv, wxo)


def _ffn_kernel(x_ref, g_ref, wgu_ref, wd_ref, gf_ref, o_ref, *, final_norm):
    x = x_ref[...]
    h = _rms_rows(x, g_ref[...]).astype(BF16)
    acc = x
    for c in range(D_FF // FF_CHUNK):
        lo = c * FF_CHUNK
        gate = _dot(h, wgu_ref[:, lo:lo + FF_CHUNK])
        up = _dot(h, wgu_ref[:, D_FF + lo:D_FF + lo + FF_CHUNK])
        act = (gate * jax.nn.sigmoid(gate) * up).astype(BF16)
        acc = acc + _dot(act, wd_ref[lo:lo + FF_CHUNK, :])
    if final_norm:
        acc = _rms_rows(acc, gf_ref[...])
    o_ref[...] = acc


def _ffn(x, g, wgu, wd, gf, final_norm):
    t = x.shape[0]
    row = lambda i: (i, 0)
    const = lambda i: (0, 0)
    return pl.pallas_call(
        functools.partial(_ffn_kernel, final_norm=final_norm),
        grid=(t // TM,),
        in_specs=[
            pl.BlockSpec((TM, D_MODEL), row),
            pl.BlockSpec((1, D_MODEL), const),
            pl.BlockSpec(wgu.shape, const),
            pl.BlockSpec(wd.shape, const),
            pl.BlockSpec((1, D_MODEL), const),
        ],
        out_specs=pl.BlockSpec((TM, D_MODEL), row),
        out_shape=jax.ShapeDtypeStruct((t, D_MODEL), F32),
        compiler_params=pltpu.CompilerParams(
            dimension_semantics=("arbitrary",), vmem_limit_bytes=VMEM_LIMIT),
        name="ffn_final" if final_norm else "ffn",
    )(x, g, wgu, wd, gf)


def _rope_angles(pos, dim):
    inv = ROPE_THETA ** (-jnp.arange(0, dim, 2, dtype=F32) / dim)
    return pos.astype(F32)[:, None] * inv[None, :]


def _head64_tables(ang, scale):
    d = np.arange(LANES) % HEAD_DIM
    sign = np.where(d < HEAD_DIM // 2, -1.0, 1.0).astype(np.float32)
    a = ang[:, d % (HEAD_DIM // 2)]
    return jnp.cos(a) * scale, jnp.sin(a) * sign[None, :] * scale


def _latent_tables(ang, scale):
    lane = np.arange(LANES)
    in_rope = (lane >= C_NOPE) & (lane < C_NOPE + C_ROPE)
    a = ang[:, (lane - C_NOPE) % (C_ROPE // 2)]
    sign = np.where(lane < C_NOPE + C_ROPE // 2, -1.0, 1.0).astype(np.float32)
    cos = jnp.where(in_rope[None, :], jnp.cos(a), 1.0) * scale
    sin = jnp.where(in_rope[None, :], jnp.sin(a) * sign[None, :], 0.0) * scale
    return cos, sin


def _pad_cols(w, width):
    return jnp.pad(w, ((0, 0), (0, width - w.shape[1])))


def kernel(x, mem, g_mix, w_in_ab, g_qa, g_ka, sink_b, w_out_ab, w_in_cd, g_cq, g_ckv,
           w_uq, w_ukv, rpb_d, w_out_cd, g_xq, g_mem, w_xq, w_xkv, w_xo, g_ffn,
           w_gate_up, w_down, g_final):
    b, s, d = x.shape
    assert (s, d) == (SEQ, D_MODEL) and mem.shape == (b, MEM_LEN, D_MODEL)
    depth = g_mix.shape[0]
    t = b * s

    pos = jnp.arange(s)
    ang_1d = _rope_angles(pos, HEAD_DIM)
    ang_2d = jnp.concatenate([_rope_angles(pos // GRID_W, HEAD_DIM // 2),
                              _rope_angles(pos % GRID_W, HEAD_DIM // 2)], axis=-1)
    ang_c = _rope_angles(pos, C_ROPE)
    q_scale = HEAD_DIM ** -0.5
    tabs_ab = jnp.stack([*_head64_tables(ang_2d, q_scale), *_head64_tables(ang_2d, 1.0),
                         *_head64_tables(ang_1d, q_scale), *_head64_tables(ang_1d, 1.0)])
    tabs_cd = jnp.stack([*_latent_tables(ang_c, (C_NOPE + C_ROPE) ** -0.5),
                         *_latent_tables(ang_c, 1.0)])
    lane = np.arange(LANES)
    block_ones = jnp.asarray(lane[:, None] // HEAD_DIM == lane[None, :] // HEAD_DIM, BF16)

    row2 = lambda v: v.reshape(1, -1).astype(F32)
    xf = x.reshape(t, d)
    mem_f = mem.reshape(b * MEM_LEN, d)

    for i in range(depth):
        j = i // 2
        if i % 2 == 0:
            gq = row2(jnp.tile(g_qa[j], 2))
            gk = row2(jnp.tile(g_ka[j], 2))
            qa, ka, va, qb, kb, vb = _inproj_ab(
                xf, row2(g_mix[i]), w_in_ab[j].astype(BF16), gq, gk, block_ones, tabs_ab)
            sh = lambda a: a.reshape(b, s, a.shape[-1])
            o1 = _attn_a(sh(qa), sh(ka), sh(va)).reshape(t, 512)
            o2 = _attn_b(sink_b[j].astype(F32), sh(qb), sh(kb), sh(vb)).reshape(t, 512)
            w_out = w_out_ab[j].astype(BF16)
        else:
            w_in = w_in_cd[j]
            w_in = jnp.concatenate(
                [w_in[:, :C_Q_RANK + C_KV_RANK],
                 _pad_cols(jnp.pad(w_in[:, 384:416], ((0, 0), (C_NOPE, 0))), LANES),
                 w_in[:, 416:]], axis=1).astype(BF16)
            wuq = w_uq[j].reshape(C_Q_RANK, C_HEADS, C_NOPE + C_ROPE)
            wuq = jnp.pad(wuq, ((0, 0), (0, 0), (0, LANES - C_NOPE - C_ROPE)))
            wuq = wuq.reshape(C_Q_RANK, C_HEADS * LANES).astype(BF16)
            wukv = w_ukv[j].reshape(C_KV_RANK, C_HEADS, C_NOPE + C_V)
            wuk = jnp.pad(wukv[:, :, :C_NOPE], ((0, 0), (0, 0), (0, LANES - C_NOPE)))
            wuk = wuk.reshape(C_KV_RANK, C_HEADS * LANES).astype(BF16)
            wuv = wukv[:, :, C_NOPE:].reshape(C_KV_RANK, C_HEADS * C_V).astype(BF16)
            qc, kc, vc, qd, kd, vd = _inproj_cd(
                xf, row2(g_mix[i]), w_in, row2(g_cq[j]), row2(g_ckv[j]), wuq, wuk, wuv, tabs_cd)
            sh = lambda a: a.reshape(b, s, a.shape[-1])
            tab = _bias_table(rpb_d[j].reshape(-1).astype(F32))
            o1 = _attn_c(sh(qc), sh(kc), sh(vc)).reshape(t, 512)
            o2 = _attn_d(sh(qd), sh(kd), sh(vd), tab).reshape(t, 512)
            w_out = w_out_cd[j].astype(BF16)

        kv = _norm_proj(mem_f, row2(g_mem[i]), w_xkv[i].astype(BF16))
        kv = kv.reshape(b, MEM_LEN, 2 * X_HEADS * X_HEAD_DIM)
        xf = _cross(xf, o1, o2, w_out[:512], w_out[512:], row2(g_xq[i]),
                    w_xq[i].astype(BF16), kv, w_xo[i].astype(BF16))
        xf = _ffn(xf, row2(g_ffn[i]), w_gate_up[i].astype(BF16), w_down[i].astype(BF16),
                  row2(g_final), final_norm=(i == depth - 1))
    return xf.reshape(b, s, d)
```

```python
import functools

import numpy as np
import jax
import jax.numpy as jnp
from jax import lax
from jax.experimental import pallas as pl
from jax.experimental.pallas import tpu as pltpu

D_MODEL = 1024
SEQ = 2048
HEAD_DIM = 64
GRID_W = 64
MEM_LEN = 256
ROPE_THETA = 10000.0
EPS = 1e-6
NEG = -1e30

A_HEADS = 8
B_HEADS = 8
B_WINDOW = 128
C_HEADS = 8
C_Q_RANK = 256
C_KV_RANK = 128
C_NOPE = 64
C_ROPE = 32
C_V = 64
D_HEADS = 8
D_WIN_R = 8
D_WIN_C = 16
X_HEADS = 4
X_HEAD_DIM = 128
D_FF = 2816

LANES = 128
TM = 256
FFN_TM = 512
TQ_DENSE = 256
FF_CHUNK = 256
VMEM_LIMIT = 56 * 1024 * 1024

F32 = jnp.float32
BF16 = jnp.bfloat16

_NT = (((1,), (1,)), ((), ()))


def _layer_spec(w, layer, rows=None, row_block=0):
    k = w.shape[1] if rows is None else rows
    return pl.BlockSpec((None, k, w.shape[2]), lambda *_: (layer, row_block, 0))


def _dot(a, b):
    return jnp.dot(a, b, preferred_element_type=F32)


def _dot_nt(a, b):
    return lax.dot_general(a, b, _NT, preferred_element_type=F32)


def _rms_rows(xf, g):
    return xf * lax.rsqrt(jnp.mean(xf * xf, axis=-1, keepdims=True) + EPS) * g


def _lane_iota(shape):
    return lax.broadcasted_iota(jnp.int32, shape, len(shape) - 1)


def _rope_chunk(x, cos, sin_signed, first_mask, half):
    rot = jnp.where(first_mask, pltpu.roll(x, LANES - half, 1), pltpu.roll(x, half, 1))
    return x * cos + rot * sin_signed


def _half_masks(dtype):
    lane = _lane_iota((1, LANES))
    return (lane < 64).astype(dtype), (lane >= 64).astype(dtype)


def _softmax_parts(s):
    m = jnp.max(s, axis=-1, keepdims=True)
    p = jnp.exp(s - m)
    return p, jnp.sum(p, axis=-1, keepdims=True), m


def _pair_out(p_a, l_a, p_b, l_b, v_lo, v_hi):
    o = _dot(p_a.astype(BF16), v_lo) + _dot(p_b.astype(BF16), v_hi)
    lane = _lane_iota(o.shape)
    return o * jnp.where(lane < 64, 1.0 / l_a, 1.0 / l_b)


def _inproj_ab_kernel(x_ref, g_ref, w_ref, gq_ref, gk_ref, bd_ref, tab_ref,
                      qa_ref, ka_ref, va_ref, qb_ref, kb_ref, vb_ref):
    h = _rms_rows(x_ref[...], g_ref[...]).astype(BF16)
    z = _dot(h, w_ref[...])
    lane = _lane_iota((TM, LANES))
    first = (lane & 63) < 32
    lo = lane < 64
    bd = bd_ref[...]

    def head_norm(zc, gain):
        zz = zc * zc
        hi = zz.astype(BF16)
        rest = (zz - hi.astype(F32)).astype(BF16)
        ss = _dot(hi, bd) + _dot(rest, bd)
        return zc * lax.rsqrt(ss * (1.0 / HEAD_DIM) + EPS) * gain

    def dup(c):
        r = pltpu.roll(c, 64, 1)
        return jnp.where(lo, c, r), jnp.where(lo, r, c)

    def chunk(i):
        return z[:, i * LANES:(i + 1) * LANES]

    for c in range(4):
        n = head_norm(chunk(c), gq_ref[...])
        qa_ref[:, c * LANES:(c + 1) * LANES] = _rope_chunk(
            n, tab_ref[0], tab_ref[1], first, 32).astype(BF16)
    k = _rope_chunk(head_norm(chunk(4), gk_ref[...]), tab_ref[2], tab_ref[3], first, 32)
    k0, k1 = dup(k)
    ka_ref[:, 0:LANES] = k0.astype(BF16)
    ka_ref[:, LANES:2 * LANES] = k1.astype(BF16)
    v0, v1 = dup(chunk(5))
    va_ref[:, 0:LANES] = v0.astype(BF16)
    va_ref[:, LANES:2 * LANES] = v1.astype(BF16)
    for c in range(4):
        qb_ref[:, c * LANES:(c + 1) * LANES] = _rope_chunk(
            chunk(6 + c), tab_ref[4], tab_ref[5], first, 32).astype(BF16)
    k = _rope_chunk(chunk(10), tab_ref[6], tab_ref[7], first, 32)
    k0, k1 = dup(k)
    kb_ref[:, 0:LANES] = k0.astype(BF16)
    kb_ref[:, LANES:2 * LANES] = k1.astype(BF16)
    v0, v1 = dup(chunk(11))
    vb_ref[:, 0:LANES] = v0.astype(BF16)
    vb_ref[:, LANES:2 * LANES] = v1.astype(BF16)


def _inproj_ab(x, g, w, layer, gq, gk, bd, tabs):
    t = x.shape[0]
    n_pos = SEQ // TM
    row = lambda i: (i, 0)
    const = lambda i: (0, 0)
    outs = [(t, 512), (t, 256), (t, 256), (t, 512), (t, 256), (t, 256)]
    return pl.pallas_call(
        _inproj_ab_kernel,
        grid=(t // TM,),
        in_specs=[
            pl.BlockSpec((TM, D_MODEL), row),
            pl.BlockSpec((1, D_MODEL), const),
            _layer_spec(w, layer),
            pl.BlockSpec((1, LANES), const),
            pl.BlockSpec((1, LANES), const),
            pl.BlockSpec((LANES, LANES), const),
            pl.BlockSpec((8, TM, LANES), lambda i: (0, i % n_pos, 0)),
        ],
        out_specs=[pl.BlockSpec((TM, n), row) for _, n in outs],
        out_shape=[jax.ShapeDtypeStruct(s, BF16) for s in outs],
        compiler_params=pltpu.CompilerParams(
            dimension_semantics=("arbitrary",), vmem_limit_bytes=VMEM_LIMIT),
        name="inproj_ab",
    )(x, g, w, gq, gk, bd, tabs)


def _attn_a_kernel(q_ref, k_ref, v_ref, o_ref):
    m_lo, m_hi = _half_masks(BF16)
    for kvh in range(2):
        k2 = k_ref[0, :, kvh * LANES:(kvh + 1) * LANES]
        v2 = v_ref[0, :, kvh * LANES:(kvh + 1) * LANES]
        v_lo, v_hi = v2 * m_lo, v2 * m_hi
        for c in (2 * kvh, 2 * kvh + 1):
            qc = q_ref[0, :, c * LANES:(c + 1) * LANES]
            p_a, l_a, _ = _softmax_parts(_dot_nt(qc * m_lo, k2))
            p_b, l_b, _ = _softmax_parts(_dot_nt(qc * m_hi, k2))
            o_ref[0, :, c * LANES:(c + 1) * LANES] = _pair_out(
                p_a, l_a, p_b, l_b, v_lo, v_hi).astype(BF16)


def _attn_a(q, k, v):
    b = q.shape[0]
    return pl.pallas_call(
        _attn_a_kernel,
        grid=(b, SEQ // TQ_DENSE),
        in_specs=[
            pl.BlockSpec((1, TQ_DENSE, 512), lambda i, j: (i, j, 0)),
            pl.BlockSpec((1, SEQ, 256), lambda i, j: (i, 0, 0)),
            pl.BlockSpec((1, SEQ, 256), lambda i, j: (i, 0, 0)),
        ],
        out_specs=pl.BlockSpec((1, TQ_DENSE, 512), lambda i, j: (i, j, 0)),
        out_shape=jax.ShapeDtypeStruct((b, SEQ, 512), BF16),
        compiler_params=pltpu.CompilerParams(
            dimension_semantics=("arbitrary", "arbitrary"), vmem_limit_bytes=VMEM_LIMIT),
        name="attn_a",
    )(q, k, v)


B_BLOCK = 256
B_KEYS = B_BLOCK + 2 * B_WINDOW


def _attn_b_kernel(sink_ref, q_ref, k_ref, v_ref, o_ref):
    n = pl.program_id(1)
    start = pl.multiple_of(jnp.clip(n * B_BLOCK - B_WINDOW, 0, SEQ - B_KEYS), B_WINDOW)
    q_pos = n * B_BLOCK + lax.broadcasted_iota(jnp.int32, (B_BLOCK, B_KEYS), 0)
    k_pos = start + lax.broadcasted_iota(jnp.int32, (B_BLOCK, B_KEYS), 1)
    delta = k_pos - q_pos
    valid = (delta <= B_WINDOW) & (delta >= -B_WINDOW)
    m_lo, m_hi = _half_masks(BF16)

    def head(qh, k2, sink):
        s = jnp.where(valid, _dot_nt(qh, k2), NEG)
        m = jnp.maximum(jnp.max(s, axis=-1, keepdims=True), sink)
        p = jnp.exp(s - m)
        return p, jnp.sum(p, axis=-1, keepdims=True) + jnp.exp(sink - m)

    for kvh in range(2):
        k2 = k_ref[0, pl.ds(start, B_KEYS), kvh * LANES:(kvh + 1) * LANES]
        v2 = v_ref[0, pl.ds(start, B_KEYS), kvh * LANES:(kvh + 1) * LANES]
        v_lo, v_hi = v2 * m_lo, v2 * m_hi
        for c in (2 * kvh, 2 * kvh + 1):
            qc = q_ref[0, :, c * LANES:(c + 1) * LANES]
            p_a, l_a = head(qc * m_lo, k2, sink_ref[2 * c])
            p_b, l_b = head(qc * m_hi, k2, sink_ref[2 * c + 1])
            o_ref[0, :, c * LANES:(c + 1) * LANES] = _pair_out(
                p_a, l_a, p_b, l_b, v_lo, v_hi).astype(BF16)


def _attn_b(sink, q, k, v):
    b = q.shape[0]
    return pl.pallas_call(
        _attn_b_kernel,
        grid=(b, SEQ // B_BLOCK),
        in_specs=[
            pl.BlockSpec(memory_space=pltpu.SMEM),
            pl.BlockSpec((1, B_BLOCK, 512), lambda i, j: (i, j, 0)),
            pl.BlockSpec((1, SEQ, 256), lambda i, j: (i, 0, 0)),
            pl.BlockSpec((1, SEQ, 256), lambda i, j: (i, 0, 0)),
        ],
        out_specs=pl.BlockSpec((1, B_BLOCK, 512), lambda i, j: (i, j, 0)),
        out_shape=jax.ShapeDtypeStruct((b, SEQ, 512), BF16),
        compiler_params=pltpu.CompilerParams(
            dimension_semantics=("arbitrary", "arbitrary"), vmem_limit_bytes=VMEM_LIMIT),
        name="attn_b",
    )(sink, q, k, v)


def _inproj_cd_kernel(x_ref, g_ref, w_ref, gcq_ref, gckv_ref, wuq_ref, wuk_ref, wuv_ref,
                      tab_ref, qc_ref, kc_ref, vc_ref, qd_ref, kd_ref, vd_ref):
    h = _rms_rows(x_ref[...], g_ref[...]).astype(BF16)
    z = _dot(h, w_ref[...])
    lane = _lane_iota((TM, LANES))
    first = lane < 80
    cq = _rms_rows(z[:, 0:256], gcq_ref[...]).astype(BF16)
    ckv = _rms_rows(z[:, 256:384], gckv_ref[...]).astype(BF16)
    q = _dot(cq, wuq_ref[...])
    kn = _dot(ckv, wuk_ref[...])
    kr = _rope_chunk(z[:, 384:512], tab_ref[2], tab_ref[3], first, 16)
    for hd in range(C_HEADS):
        sl = slice(hd * LANES, (hd + 1) * LANES)
        qc_ref[:, sl] = _rope_chunk(q[:, sl], tab_ref[0], tab_ref[1], first, 16).astype(BF16)
        kc_ref[:, sl] = (kn[:, sl] + kr).astype(BF16)
    vc_ref[...] = _dot(ckv, wuv_ref[...]).astype(BF16)
    qd_ref[...] = (z[:, 512:1024] * (HEAD_DIM ** -0.5)).astype(BF16)
    kd_ref[...] = z[:, 1024:1536].astype(BF16)
    vd_ref[...] = z[:, 1536:2048].astype(BF16)


def _inproj_cd(x, g, w, gcq, gckv, wuq, wuk, wuv, tabs):
    t = x.shape[0]
    n_pos = SEQ // TM
    row = lambda i: (i, 0)
    const = lambda i: (0, 0)
    outs = [(t, 1024), (t, 1024), (t, 512), (t, 512), (t, 512), (t, 512)]
    return pl.pallas_call(
        _inproj_cd_kernel,
        grid=(t // TM,),
        in_specs=[
            pl.BlockSpec((TM, D_MODEL), row),
            pl.BlockSpec((1, D_MODEL), const),
            pl.BlockSpec(w.shape, const),
            pl.BlockSpec((1, C_Q_RANK), const),
            pl.BlockSpec((1, C_KV_RANK), const),
            pl.BlockSpec(wuq.shape, const),
            pl.BlockSpec(wuk.shape, const),
            pl.BlockSpec(wuv.shape, const),
            pl.BlockSpec((4, TM, LANES), lambda i: (0, i % n_pos, 0)),
        ],
        out_specs=[pl.BlockSpec((TM, n), row) for _, n in outs],
        out_shape=[jax.ShapeDtypeStruct(s, BF16) for s in outs],
        compiler_params=pltpu.CompilerParams(
            dimension_semantics=("arbitrary",), vmem_limit_bytes=VMEM_LIMIT),
        name="inproj_cd",
    )(x, g, w, gcq, gckv, wuq, wuk, wuv, tabs)


def _attn_c_kernel(q_ref, k_ref, v_ref, o_ref):
    m_lo, m_hi = _half_masks(BF16)
    for c in range(4):
        v2 = v_ref[0, :, c * LANES:(c + 1) * LANES]
        v_lo, v_hi = v2 * m_lo, v2 * m_hi
        ha, hb = 2 * c, 2 * c + 1
        p_a, l_a, _ = _softmax_parts(_dot_nt(q_ref[0, :, ha * LANES:(ha + 1) * LANES],
                                             k_ref[0, :, ha * LANES:(ha + 1) * LANES]))
        p_b, l_b, _ = _softmax_parts(_dot_nt(q_ref[0, :, hb * LANES:(hb + 1) * LANES],
                                             k_ref[0, :, hb * LANES:(hb + 1) * LANES]))
        o_ref[0, :, c * LANES:(c + 1) * LANES] = _pair_out(
            p_a, l_a, p_b, l_b, v_lo, v_hi).astype(BF16)


def _attn_c(q, k, v):
    b = q.shape[0]
    return pl.pallas_call(
        _attn_c_kernel,
        grid=(b, SEQ // TQ_DENSE),
        in_specs=[
            pl.BlockSpec((1, TQ_DENSE, 1024), lambda i, j: (i, j, 0)),
            pl.BlockSpec((1, SEQ, 1024), lambda i, j: (i, 0, 0)),
            pl.BlockSpec((1, SEQ, 512), lambda i, j: (i, 0, 0)),
        ],
        out_specs=pl.BlockSpec((1, TQ_DENSE, 512), lambda i, j: (i, j, 0)),
        out_shape=jax.ShapeDtypeStruct((b, SEQ, 512), BF16),
        compiler_params=pltpu.CompilerParams(
            dimension_semantics=("arbitrary", "arbitrary"), vmem_limit_bytes=VMEM_LIMIT),
        name="attn_c",
    )(q, k, v)


N_ROWS = SEQ // GRID_W
N_DR = 2 * D_WIN_R - 1
N_DC = 2 * D_WIN_C - 1
D_QROWS = 4
D_WROWS = D_QROWS + D_WIN_R
D_Q = D_QROWS * GRID_W
D_KEYS = D_WROWS * GRID_W
D_BLOCKS = N_ROWS // D_QROWS
D_KINDS = 3


def _d_window_row(kind, a, w):
    if kind == 0:
        return w < D_WIN_R, w - a + (D_WIN_R - 1)
    if kind == 1:
        return a <= w < a + D_WIN_R, w - a + (D_WIN_R - 1) - D_WIN_R // 2
    lead = D_WROWS - D_WIN_R
    return w >= lead, w - a - 1


def _bias_table_kernel(rpb_ref, tab_ref):
    h = pl.program_id(0)
    shape = (GRID_W, LANES)
    qc = lax.broadcasted_iota(jnp.int32, shape, 0)
    lane = lax.broadcasted_iota(jnp.int32, shape, 1)
    kc = lane & (GRID_W - 1)
    lower = lane < GRID_W
    idx = jnp.clip(kc - qc + (D_WIN_C - 1), 0, N_DC - 1)
    c0 = jnp.clip(qc - D_WIN_C // 2, 0, GRID_W - D_WIN_C)
    col_ok = (kc >= c0) & (kc < c0 + D_WIN_C)
    base = h * (N_DR * N_DC)
    tiles = []
    for d in range(N_DR):
        acc = jnp.zeros(shape, F32)
        for j in range(N_DC):
            acc = jnp.where(idx == j, rpb_ref[base + d * N_DC + j], acc)
        tiles.append(jnp.where(col_ok, acc, NEG))
    neg = jnp.full(shape, NEG, F32)
    for kind in range(D_KINDS):
        for a in range(D_QROWS):
            for wp in range(D_WROWS // 2):
                ok_lo, d_lo = _d_window_row(kind, a, 2 * wp)
                ok_hi, d_hi = _d_window_row(kind, a, 2 * wp + 1)
                t_lo = tiles[d_lo] if ok_lo else neg
                t_hi = tiles[d_hi] if ok_hi else neg
                tab_ref[0, kind, a * GRID_W:(a + 1) * GRID_W, wp * LANES:(wp + 1) * LANES] = (
                    jnp.where(lower, t_lo, t_hi))


def _bias_table(rpb_flat):
    return pl.pallas_call(
        _bias_table_kernel,
        grid=(D_HEADS,),
        in_specs=[pl.BlockSpec(memory_space=pltpu.SMEM)],
        out_specs=pl.BlockSpec((1, D_KINDS, D_Q, D_KEYS), lambda i: (i, 0, 0, 0)),
        out_shape=jax.ShapeDtypeStruct((D_HEADS, D_KINDS, D_Q, D_KEYS), F32),
        compiler_params=pltpu.CompilerParams(dimension_semantics=("arbitrary",)),
        name="bias_table",
    )(rpb_flat)


def _attn_d_kernel(q_ref, k_ref, v_ref, tab_ref, o_ref):
    blk = pl.program_id(1)
    w0 = jnp.clip(blk * D_QROWS - D_WIN_R // 2, 0, N_ROWS - D_WROWS)
    start = pl.multiple_of(w0 * GRID_W, D_Q)
    m_lo, m_hi = _half_masks(BF16)

    def head(qh, kwin, hd):
        p, l, _ = _softmax_parts(_dot_nt(qh, kwin) + tab_ref[hd, 0])
        return p, l

    for c in range(4):
        kwin = k_ref[0, pl.ds(start, D_KEYS), c * LANES:(c + 1) * LANES]
        v2 = v_ref[0, pl.ds(start, D_KEYS), c * LANES:(c + 1) * LANES]
        v_lo, v_hi = v2 * m_lo, v2 * m_hi
        qc = q_ref[0, :, c * LANES:(c + 1) * LANES]
        p_a, l_a = head(qc * m_lo, kwin, 2 * c)
        p_b, l_b = head(qc * m_hi, kwin, 2 * c + 1)
        o_ref[0, :, c * LANES:(c + 1) * LANES] = _pair_out(
            p_a, l_a, p_b, l_b, v_lo, v_hi).astype(BF16)


def _attn_d(q, k, v, tab):
    b = q.shape[0]

    def kind(i, j):
        return (0, jnp.where(j == 0, 0, jnp.where(j == D_BLOCKS - 1, 2, 1)), 0, 0)

    return pl.pallas_call(
        _attn_d_kernel,
        grid=(b, D_BLOCKS),
        in_specs=[
            pl.BlockSpec((1, D_Q, 512), lambda i, j: (i, j, 0)),
            pl.BlockSpec((1, SEQ, 512), lambda i, j: (i, 0, 0)),
            pl.BlockSpec((1, SEQ, 512), lambda i, j: (i, 0, 0)),
            pl.BlockSpec((D_HEADS, 1, D_Q, D_KEYS), kind),
        ],
        out_specs=pl.BlockSpec((1, D_Q, 512), lambda i, j: (i, j, 0)),
        out_shape=jax.ShapeDtypeStruct((b, SEQ, 512), BF16),
        compiler_params=pltpu.CompilerParams(
            dimension_semantics=("arbitrary", "arbitrary"), vmem_limit_bytes=VMEM_LIMIT),
        name="attn_d",
    )(q, k, v, tab)


def _norm_proj_kernel(x_ref, g_ref, w_ref, o_ref):
    h = _rms_rows(x_ref[...], g_ref[...]).astype(BF16)
    o_ref[...] = _dot(h, w_ref[...]).astype(BF16)


def _norm_proj(x, g, w, layer):
    t = x.shape[0]
    n = w.shape[2]
    return pl.pallas_call(
        _norm_proj_kernel,
        grid=(t // TM,),
        in_specs=[
            pl.BlockSpec((TM, D_MODEL), lambda i: (i, 0)),
            pl.BlockSpec((1, D_MODEL), lambda i: (0, 0)),
            _layer_spec(w, layer),
        ],
        out_specs=pl.BlockSpec((TM, n), lambda i: (i, 0)),
        out_shape=jax.ShapeDtypeStruct((t, n), BF16),
        compiler_params=pltpu.CompilerParams(
            dimension_semantics=("arbitrary",), vmem_limit_bytes=VMEM_LIMIT),
        name="mem_kv_proj",
    )(x, g, w)


def _cross_kernel(x_ref, a1_ref, a2_ref, wo1_ref, wo2_ref, g_ref, wq_ref, kv_ref, wxo_ref, o_ref):
    x1 = x_ref[...] + _dot(a1_ref[...], wo1_ref[...]) + _dot(a2_ref[...], wo2_ref[...])
    h = _rms_rows(x1, g_ref[...]).astype(BF16)
    q = (_dot(h, wq_ref[...]) * (X_HEAD_DIM ** -0.5)).astype(BF16)
    heads = []
    for hd in range(X_HEADS):
        sl = slice(hd * LANES, (hd + 1) * LANES)
        k = kv_ref[0, :, sl]
        v = kv_ref[0, :, X_HEADS * LANES + hd * LANES:X_HEADS * LANES + (hd + 1) * LANES]
        p, l, _ = _softmax_parts(_dot_nt(q[:, sl], k))
        heads.append((_dot(p.astype(BF16), v) * (1.0 / l)).astype(BF16))
    o = jnp.concatenate(heads, axis=1)
    o_ref[...] = x1 + _dot(o, wxo_ref[...])


def _cross(x, a1, a2, w_out, mix_layer, g, wq, kv, wxo, layer):
    t = x.shape[0]
    per_b = SEQ // TM
    row = lambda i: (i, 0)
    const = lambda i: (0, 0)
    return pl.pallas_call(
        _cross_kernel,
        grid=(t // TM,),
        in_specs=[
            pl.BlockSpec((TM, D_MODEL), row),
            pl.BlockSpec((TM, 512), row),
            pl.BlockSpec((TM, 512), row),
            _layer_spec(w_out, mix_layer, rows=512, row_block=0),
            _layer_spec(w_out, mix_layer, rows=512, row_block=1),
            pl.BlockSpec((1, D_MODEL), const),
            _layer_spec(wq, layer),
            pl.BlockSpec((1, MEM_LEN, 2 * X_HEADS * X_HEAD_DIM), lambda i: (i // per_b, 0, 0)),
            _layer_spec(wxo, layer),
        ],
        out_specs=pl.BlockSpec((TM, D_MODEL), row),
        out_shape=jax.ShapeDtypeStruct((t, D_MODEL), F32),
        compiler_params=pltpu.CompilerParams(
            dimension_semantics=("arbitrary",), vmem_limit_bytes=VMEM_LIMIT),
        name="outproj_cross",
    )(x, a1, a2, w_out, w_out, g, wq, kv, wxo)


def _ffn_kernel(x_ref, g_ref, wgu_ref, wd_ref, gf_ref, o_ref, *, final_norm):
    x = x_ref[...]
    h = _rms_rows(x, g_ref[...]).astype(BF16)
    acc = x
    for c in range(D_FF // FF_CHUNK):
        lo = c * FF_CHUNK
        gate = _dot(h, wgu_ref[:, lo:lo + FF_CHUNK])
        up = _dot(h, wgu_ref[:, D_FF + lo:D_FF + lo + FF_CHUNK])
        act = (gate * jax.nn.sigmoid(gate) * up).astype(BF16)
        acc = acc + _dot(act, wd_ref[lo:lo + FF_CHUNK, :])
    if final_norm:
        acc = _rms_rows(acc, gf_ref[...])
    o_ref[...] = acc


def _ffn(x, g, wgu, wd, layer, gf, final_norm):
    t = x.shape[0]
    row = lambda i: (i, 0)
    const = lambda i: (0, 0)
    return pl.pallas_call(
        functools.partial(_ffn_kernel, final_norm=final_norm),
        grid=(t // FFN_TM,),
        in_specs=[
            pl.BlockSpec((FFN_TM, D_MODEL), row),
            pl.BlockSpec((1, D_MODEL), const),
            _layer_spec(wgu, layer),
            _layer_spec(wd, layer),
            pl.BlockSpec((1, D_MODEL), const),
        ],
        out_specs=pl.BlockSpec((FFN_TM, D_MODEL), row),
        out_shape=jax.ShapeDtypeStruct((t, D_MODEL), F32),
        compiler_params=pltpu.CompilerParams(
            dimension_semantics=("arbitrary",), vmem_limit_bytes=VMEM_LIMIT),
        name="ffn_final" if final_norm else "ffn",
    )(x, g, wgu, wd, gf)


def _rope_angles(pos, dim):
    inv = ROPE_THETA ** (-jnp.arange(0, dim, 2, dtype=F32) / dim)
    return pos.astype(F32)[:, None] * inv[None, :]


def _head64_tables(ang, scale):
    d = np.arange(LANES) % HEAD_DIM
    sign = np.where(d < HEAD_DIM // 2, -1.0, 1.0).astype(np.float32)
    a = ang[:, d % (HEAD_DIM // 2)]
    return jnp.cos(a) * scale, jnp.sin(a) * sign[None, :] * scale


def _latent_tables(ang, scale):
    lane = np.arange(LANES)
    in_rope = (lane >= C_NOPE) & (lane < C_NOPE + C_ROPE)
    a = ang[:, (lane - C_NOPE) % (C_ROPE // 2)]
    sign = np.where(lane < C_NOPE + C_ROPE // 2, -1.0, 1.0).astype(np.float32)
    cos = jnp.where(in_rope[None, :], jnp.cos(a), 1.0) * scale
    sin = jnp.where(in_rope[None, :], jnp.sin(a) * sign[None, :], 0.0) * scale
    return cos, sin


def _pad_cols(w, width):
    return jnp.pad(w, ((0, 0), (0, width - w.shape[1])))


def kernel(x, mem, g_mix, w_in_ab, g_qa, g_ka, sink_b, w_out_ab, w_in_cd, g_cq, g_ckv,
           w_uq, w_ukv, rpb_d, w_out_cd, g_xq, g_mem, w_xq, w_xkv, w_xo, g_ffn,
           w_gate_up, w_down, g_final):
    b, s, d = x.shape
    assert (s, d) == (SEQ, D_MODEL) and mem.shape == (b, MEM_LEN, D_MODEL)
    depth = g_mix.shape[0]
    t = b * s

    pos = jnp.arange(s)
    ang_1d = _rope_angles(pos, HEAD_DIM)
    ang_2d = jnp.concatenate([_rope_angles(pos // GRID_W, HEAD_DIM // 2),
                              _rope_angles(pos % GRID_W, HEAD_DIM // 2)], axis=-1)
    ang_c = _rope_angles(pos, C_ROPE)
    q_scale = HEAD_DIM ** -0.5
    tabs_ab = jnp.stack([*_head64_tables(ang_2d, q_scale), *_head64_tables(ang_2d, 1.0),
                         *_head64_tables(ang_1d, q_scale), *_head64_tables(ang_1d, 1.0)])
    tabs_cd = jnp.stack([*_latent_tables(ang_c, (C_NOPE + C_ROPE) ** -0.5),
                         *_latent_tables(ang_c, 1.0)])
    lane = np.arange(LANES)
    block_ones = jnp.asarray(lane[:, None] // HEAD_DIM == lane[None, :] // HEAD_DIM, BF16)

    row2 = lambda v: v.reshape(1, -1).astype(F32)
    xf = x.reshape(t, d)
    mem_f = mem.reshape(b * MEM_LEN, d)
    w_in_ab_h, w_out_ab_h, w_out_cd_h = (w.astype(BF16) for w in (w_in_ab, w_out_ab, w_out_cd))
    w_xq_h, w_xkv_h, w_xo_h = (w.astype(BF16) for w in (w_xq, w_xkv, w_xo))
    w_gate_up_h, w_down_h = w_gate_up.astype(BF16), w_down.astype(BF16)

    for i in range(depth):
        j = i // 2
        if i % 2 == 0:
            gq = row2(jnp.tile(g_qa[j], 2))
            gk = row2(jnp.tile(g_ka[j], 2))
            qa, ka, va, qb, kb, vb = _inproj_ab(
                xf, row2(g_mix[i]), w_in_ab_h, j, gq, gk, block_ones, tabs_ab)
            sh = lambda a: a.reshape(b, s, a.shape[-1])
            o1 = _attn_a(sh(qa), sh(ka), sh(va)).reshape(t, 512)
            o2 = _attn_b(sink_b[j].astype(F32), sh(qb), sh(kb), sh(vb)).reshape(t, 512)
            w_out = w_out_ab_h
        else:
            w_in = w_in_cd[j]
            w_in = jnp.concatenate(
                [w_in[:, :C_Q_RANK + C_KV_RANK],
                 _pad_cols(jnp.pad(w_in[:, 384:416], ((0, 0), (C_NOPE, 0))), LANES),
                 w_in[:, 416:]], axis=1).astype(BF16)
            wuq = w_uq[j].reshape(C_Q_RANK, C_HEADS, C_NOPE + C_ROPE)
            wuq = jnp.pad(wuq, ((0, 0), (0, 0), (0, LANES - C_NOPE - C_ROPE)))
            wuq = wuq.reshape(C_Q_RANK, C_HEADS * LANES).astype(BF16)
            wukv = w_ukv[j].reshape(C_KV_RANK, C_HEADS, C_NOPE + C_V)
            wuk = jnp.pad(wukv[:, :, :C_NOPE], ((0, 0), (0, 0), (0, LANES - C_NOPE)))
            wuk = wuk.reshape(C_KV_RANK, C_HEADS * LANES).astype(BF16)
            wuv = wukv[:, :, C_NOPE:].reshape(C_KV_RANK, C_HEADS * C_V).astype(BF16)
            qc, kc, vc, qd, kd, vd = _inproj_cd(
                xf, row2(g_mix[i]), w_in, row2(g_cq[j]), row2(g_ckv[j]), wuq, wuk, wuv, tabs_cd)
            sh = lambda a: a.reshape(b, s, a.shape[-1])
            tab = _bias_table(rpb_d[j].reshape(-1).astype(F32))
            o1 = _attn_c(sh(qc), sh(kc), sh(vc)).reshape(t, 512)
            o2 = _attn_d(sh(qd), sh(kd), sh(vd), tab).reshape(t, 512)
            w_out = w_out_cd_h

        kv = _norm_proj(mem_f, row2(g_mem[i]), w_xkv_h, i)
        kv = kv.reshape(b, MEM_LEN, 2 * X_HEADS * X_HEAD_DIM)
        xf = _cross(xf, o1, o2, w_out, j, row2(g_xq[i]), w_xq_h, kv, w_xo_h, i)
        xf = _ffn(xf, row2(g_ffn[i]), w_gate_up_h, w_down_h, i,
                  row2(g_final), final_norm=(i == depth - 1))
    return xf.reshape(b, s, d)
```

```python
import functools

import numpy as np
import jax
import jax.numpy as jnp
from jax import lax
from jax.experimental import pallas as pl
from jax.experimental.pallas import tpu as pltpu

D_MODEL = 1024
SEQ = 2048
HEAD_DIM = 64
GRID_W = 64
MEM_LEN = 256
ROPE_THETA = 10000.0
EPS = 1e-6
NEG = -1e30

A_HEADS = 8
B_HEADS = 8
B_WINDOW = 128
C_HEADS = 8
C_Q_RANK = 256
C_KV_RANK = 128
C_NOPE = 64
C_ROPE = 32
C_V = 64
D_HEADS = 8
D_WIN_R = 8
D_WIN_C = 16
X_HEADS = 4
X_HEAD_DIM = 128
D_FF = 2816

LANES = 128
QUAD = 256
LOG2E = 1.4426950408889634
TM = 256
FFN_TM = 512
TQ_DENSE = 256
FF_CHUNK = 256
VMEM_LIMIT = 56 * 1024 * 1024

F32 = jnp.float32
BF16 = jnp.bfloat16

_NT = (((1,), (1,)), ((), ()))


def _layer_spec(w, layer, rows=None, row_block=0):
    k = w.shape[1] if rows is None else rows
    return pl.BlockSpec((None, k, w.shape[2]), lambda *_: (layer, row_block, 0))


def _dot(a, b):
    return jnp.dot(a, b, preferred_element_type=F32)


def _dot_nt(a, b):
    return lax.dot_general(a, b, _NT, preferred_element_type=F32)


def _rms_rows(xf, g):
    return xf * lax.rsqrt(jnp.mean(xf * xf, axis=-1, keepdims=True) + EPS) * g


def _lane_iota(shape):
    return lax.broadcasted_iota(jnp.int32, shape, len(shape) - 1)


def _rope_chunk(x, cos, sin_signed, first_mask, half):
    rot = jnp.where(first_mask, pltpu.roll(x, LANES - half, 1), pltpu.roll(x, half, 1))
    return x * cos + rot * sin_signed


def _half_masks(dtype):
    lane = _lane_iota((1, LANES))
    return (lane < 64).astype(dtype), (lane >= 64).astype(dtype)


def _quarter_masks(dtype):
    lane = _lane_iota((1, QUAD))
    return [((lane >= HEAD_DIM * i) & (lane < HEAD_DIM * (i + 1))).astype(dtype) for i in range(4)]


def _softmax2_parts(s):
    m = jnp.max(s, axis=-1, keepdims=True)
    p = jnp.exp2(s - m)
    return p.astype(BF16), jnp.sum(p, axis=-1, keepdims=True), m


def _quad_out(ps, ls, ws):
    o = _dot(ps[0], ws[0]) * (1.0 / ls[0])
    for p, l, w in zip(ps[1:], ls[1:], ws[1:]):
        o = o + _dot(p, w) * (1.0 / l)
    return o


def _inproj_ab_kernel(x_ref, g_ref, w_ref, gq_ref, gk_ref, bd_ref, tab_ref,
                      qa_ref, ka_ref, va_ref, qb_ref, kb_ref, vb_ref):
    h = _rms_rows(x_ref[...], g_ref[...]).astype(BF16)
    z = _dot(h, w_ref[...])
    lane = _lane_iota((TM, LANES))
    first = (lane & 63) < 32
    lo = lane < 64
    bd = bd_ref[...]

    def head_norm(zc, gain):
        zz = zc * zc
        hi = zz.astype(BF16)
        rest = (zz - hi.astype(F32)).astype(BF16)
        ss = _dot(hi, bd) + _dot(rest, bd)
        return zc * lax.rsqrt(ss * (1.0 / HEAD_DIM) + EPS) * gain

    def dup(c):
        r = pltpu.roll(c, 64, 1)
        return jnp.where(lo, c, r), jnp.where(lo, r, c)

    def chunk(i):
        return z[:, i * LANES:(i + 1) * LANES]

    for c in range(4):
        n = head_norm(chunk(c), gq_ref[...])
        qa_ref[:, c * LANES:(c + 1) * LANES] = _rope_chunk(
            n, tab_ref[0], tab_ref[1], first, 32).astype(BF16)
    k = _rope_chunk(head_norm(chunk(4), gk_ref[...]), tab_ref[2], tab_ref[3], first, 32)
    k0, k1 = dup(k)
    ka_ref[:, 0:LANES] = k0.astype(BF16)
    ka_ref[:, LANES:2 * LANES] = k1.astype(BF16)
    v0, v1 = dup(chunk(5))
    va_ref[:, 0:LANES] = v0.astype(BF16)
    va_ref[:, LANES:2 * LANES] = v1.astype(BF16)
    for c in range(4):
        qb_ref[:, c * LANES:(c + 1) * LANES] = _rope_chunk(
            chunk(6 + c), tab_ref[4], tab_ref[5], first, 32).astype(BF16)
    k = _rope_chunk(chunk(10), tab_ref[6], tab_ref[7], first, 32)
    k0, k1 = dup(k)
    kb_ref[:, 0:LANES] = k0.astype(BF16)
    kb_ref[:, LANES:2 * LANES] = k1.astype(BF16)
    v0, v1 = dup(chunk(11))
    vb_ref[:, 0:LANES] = v0.astype(BF16)
    vb_ref[:, LANES:2 * LANES] = v1.astype(BF16)


def _inproj_ab(x, g, w, layer, gq, gk, bd, tabs):
    t = x.shape[0]
    n_pos = SEQ // TM
    row = lambda i: (i, 0)
    const = lambda i: (0, 0)
    outs = [(t, 512), (t, 256), (t, 256), (t, 512), (t, 256), (t, 256)]
    return pl.pallas_call(
        _inproj_ab_kernel,
        grid=(t // TM,),
        in_specs=[
            pl.BlockSpec((TM, D_MODEL), row),
            pl.BlockSpec((1, D_MODEL), const),
            _layer_spec(w, layer),
            pl.BlockSpec((1, LANES), const),
            pl.BlockSpec((1, LANES), const),
            pl.BlockSpec((LANES, LANES), const),
            pl.BlockSpec((8, TM, LANES), lambda i: (0, i % n_pos, 0)),
        ],
        out_specs=[pl.BlockSpec((TM, n), row) for _, n in outs],
        out_shape=[jax.ShapeDtypeStruct(s, BF16) for s in outs],
        compiler_params=pltpu.CompilerParams(
            dimension_semantics=("arbitrary",), vmem_limit_bytes=VMEM_LIMIT),
        name="inproj_ab",
    )(x, g, w, gq, gk, bd, tabs)


def _place_shared_v(v2, masks):
    zero = jnp.zeros_like(v2)
    lo, hi = v2 * masks[0], v2 * masks[1]
    return [jnp.concatenate(parts, axis=1)
            for parts in ((lo, zero), (hi, zero), (zero, lo), (zero, hi))]


def _attn_a_kernel(q_ref, k_ref, v_ref, o_ref, w_ref):
    masks = _half_masks(BF16)

    @pl.when(pl.program_id(1) == 0)
    def _():
        for kvh in range(2):
            placed = _place_shared_v(v_ref[0, :, kvh * LANES:(kvh + 1) * LANES], masks)
            for i in range(4):
                w_ref[kvh, i] = placed[i]

    for kvh in range(2):
        k2 = k_ref[0, :, kvh * LANES:(kvh + 1) * LANES]
        ps, ls = [], []
        for i in range(4):
            c = 2 * kvh + i // 2
            qc = q_ref[0, :, c * LANES:(c + 1) * LANES]
            p, l, _ = _softmax2_parts(_dot_nt(qc * masks[i % 2], k2))
            ps.append(p)
            ls.append(l)
        o_ref[0, :, kvh * QUAD:(kvh + 1) * QUAD] = _quad_out(
            ps, ls, [w_ref[kvh, i] for i in range(4)]).astype(BF16)


def _attn_a(q, k, v):
    b = q.shape[0]
    return pl.pallas_call(
        _attn_a_kernel,
        grid=(b, SEQ // TQ_DENSE),
        in_specs=[
            pl.BlockSpec((1, TQ_DENSE, 512), lambda i, j: (i, j, 0)),
            pl.BlockSpec((1, SEQ, 256), lambda i, j: (i, 0, 0)),
            pl.BlockSpec((1, SEQ, 256), lambda i, j: (i, 0, 0)),
        ],
        out_specs=pl.BlockSpec((1, TQ_DENSE, 512), lambda i, j: (i, j, 0)),
        out_shape=jax.ShapeDtypeStruct((b, SEQ, 512), BF16),
        scratch_shapes=[pltpu.VMEM((2, 4, SEQ, QUAD), BF16)],
        compiler_params=pltpu.CompilerParams(
            dimension_semantics=("arbitrary", "arbitrary"), vmem_limit_bytes=VMEM_LIMIT),
        name="attn_a",
    )(q, k, v)


B_BLOCK = 256
B_KEYS = B_BLOCK + 2 * B_WINDOW


def _attn_b_kernel(sink_ref, q_ref, k_ref, v_ref, o_ref):
    n = pl.program_id(1)
    start = pl.multiple_of(jnp.clip(n * B_BLOCK - B_WINDOW, 0, SEQ - B_KEYS), B_WINDOW)
    q_pos = n * B_BLOCK + lax.broadcasted_iota(jnp.int32, (B_BLOCK, B_KEYS), 0)
    k_pos = start + lax.broadcasted_iota(jnp.int32, (B_BLOCK, B_KEYS), 1)
    delta = k_pos - q_pos
    valid = (delta <= B_WINDOW) & (delta >= -B_WINDOW)
    masks = _half_masks(BF16)

    def head(qh, k2, sink):
        s = jnp.where(valid, _dot_nt(qh, k2), NEG)
        m = jnp.maximum(jnp.max(s, axis=-1, keepdims=True), sink)
        p = jnp.exp2(s - m)
        return p.astype(BF16), jnp.sum(p, axis=-1, keepdims=True) + jnp.exp2(sink - m)

    for kvh in range(2):
        k2 = k_ref[0, pl.ds(start, B_KEYS), kvh * LANES:(kvh + 1) * LANES]
        ws = _place_shared_v(v_ref[0, pl.ds(start, B_KEYS), kvh * LANES:(kvh + 1) * LANES], masks)
        ps, ls = [], []
        for i in range(4):
            c = 2 * kvh + i // 2
            qc = q_ref[0, :, c * LANES:(c + 1) * LANES]
            p, l = head(qc * masks[i % 2], k2, sink_ref[4 * kvh + i] * LOG2E)
            ps.append(p)
            ls.append(l)
        o_ref[0, :, kvh * QUAD:(kvh + 1) * QUAD] = _quad_out(ps, ls, ws).astype(BF16)


def _attn_b(sink, q, k, v):
    b = q.shape[0]
    return pl.pallas_call(
        _attn_b_kernel,
        grid=(b, SEQ // B_BLOCK),
        in_specs=[
            pl.BlockSpec(memory_space=pltpu.SMEM),
            pl.BlockSpec((1, B_BLOCK, 512), lambda i, j: (i, j, 0)),
            pl.BlockSpec((1, SEQ, 256), lambda i, j: (i, 0, 0)),
            pl.BlockSpec((1, SEQ, 256), lambda i, j: (i, 0, 0)),
        ],
        out_specs=pl.BlockSpec((1, B_BLOCK, 512), lambda i, j: (i, j, 0)),
        out_shape=jax.ShapeDtypeStruct((b, SEQ, 512), BF16),
        compiler_params=pltpu.CompilerParams(
            dimension_semantics=("arbitrary", "arbitrary"), vmem_limit_bytes=VMEM_LIMIT),
        name="attn_b",
    )(sink, q, k, v)


def _inproj_cd_kernel(x_ref, g_ref, w_ref, gcq_ref, gckv_ref, wuq_ref, wuk_ref, wuv_ref,
                      tab_ref, qc_ref, kc_ref, vc_ref, qd_ref, kd_ref, vd_ref):
    h = _rms_rows(x_ref[...], g_ref[...]).astype(BF16)
    z = _dot(h, w_ref[...])
    lane = _lane_iota((TM, LANES))
    first = lane < 80
    cq = _rms_rows(z[:, 0:256], gcq_ref[...]).astype(BF16)
    ckv = _rms_rows(z[:, 256:384], gckv_ref[...]).astype(BF16)
    q = _dot(cq, wuq_ref[...])
    kn = _dot(ckv, wuk_ref[...])
    kr = _rope_chunk(z[:, 384:512], tab_ref[2], tab_ref[3], first, 16)
    for hd in range(C_HEADS):
        sl = slice(hd * LANES, (hd + 1) * LANES)
        qc_ref[:, sl] = _rope_chunk(q[:, sl], tab_ref[0], tab_ref[1], first, 16).astype(BF16)
        kc_ref[:, sl] = (kn[:, sl] + kr).astype(BF16)
    vc_ref[...] = _dot(ckv, wuv_ref[...]).astype(BF16)
    qd_ref[...] = (z[:, 512:1024] * (HEAD_DIM ** -0.5 * LOG2E)).astype(BF16)
    kd_ref[...] = z[:, 1024:1536].astype(BF16)
    vd_ref[...] = z[:, 1536:2048].astype(BF16)


def _inproj_cd(x, g, w, gcq, gckv, wuq, wuk, wuv, tabs):
    t = x.shape[0]
    n_pos = SEQ // TM
    row = lambda i: (i, 0)
    const = lambda i: (0, 0)
    outs = [(t, 1024), (t, 1024), (t, 512), (t, 512), (t, 512), (t, 512)]
    return pl.pallas_call(
        _inproj_cd_kernel,
        grid=(t // TM,),
        in_specs=[
            pl.BlockSpec((TM, D_MODEL), row),
            pl.BlockSpec((1, D_MODEL), const),
            pl.BlockSpec(w.shape, const),
            pl.BlockSpec((1, C_Q_RANK), const),
            pl.BlockSpec((1, C_KV_RANK), const),
            pl.BlockSpec(wuq.shape, const),
            pl.BlockSpec(wuk.shape, const),
            pl.BlockSpec(wuv.shape, const),
            pl.BlockSpec((4, TM, LANES), lambda i: (0, i % n_pos, 0)),
        ],
        out_specs=[pl.BlockSpec((TM, n), row) for _, n in outs],
        out_shape=[jax.ShapeDtypeStruct(s, BF16) for s in outs],
        compiler_params=pltpu.CompilerParams(
            dimension_semantics=("arbitrary",), vmem_limit_bytes=VMEM_LIMIT),
        name="inproj_cd",
    )(x, g, w, gcq, gckv, wuq, wuk, wuv, tabs)


def _attn_c_kernel(q_ref, k_ref, v_ref, o_ref, w_ref):
    @pl.when(pl.program_id(1) == 0)
    def _():
        masks = _quarter_masks(BF16)
        for g in range(2):
            vq = v_ref[0, :, g * QUAD:(g + 1) * QUAD]
            for i in range(4):
                w_ref[g, i] = vq * masks[i]

    for g in range(2):
        ps, ls = [], []
        for i in range(4):
            hd = 4 * g + i
            p, l, _ = _softmax2_parts(_dot_nt(q_ref[0, :, hd * LANES:(hd + 1) * LANES],
                                              k_ref[0, :, hd * LANES:(hd + 1) * LANES]))
            ps.append(p)
            ls.append(l)
        o_ref[0, :, g * QUAD:(g + 1) * QUAD] = _quad_out(
            ps, ls, [w_ref[g, i] for i in range(4)]).astype(BF16)


def _attn_c(q, k, v):
    b = q.shape[0]
    return pl.pallas_call(
        _attn_c_kernel,
        grid=(b, SEQ // TQ_DENSE),
        in_specs=[
            pl.BlockSpec((1, TQ_DENSE, 1024), lambda i, j: (i, j, 0)),
            pl.BlockSpec((1, SEQ, 1024), lambda i, j: (i, 0, 0)),
            pl.BlockSpec((1, SEQ, 512), lambda i, j: (i, 0, 0)),
        ],
        out_specs=pl.BlockSpec((1, TQ_DENSE, 512), lambda i, j: (i, j, 0)),
        out_shape=jax.ShapeDtypeStruct((b, SEQ, 512), BF16),
        scratch_shapes=[pltpu.VMEM((2, 4, SEQ, QUAD), BF16)],
        compiler_params=pltpu.CompilerParams(
            dimension_semantics=("arbitrary", "arbitrary"), vmem_limit_bytes=VMEM_LIMIT),
        name="attn_c",
    )(q, k, v)


N_ROWS = SEQ // GRID_W
N_DR = 2 * D_WIN_R - 1
N_DC = 2 * D_WIN_C - 1
D_QROWS = 4
D_WROWS = D_QROWS + D_WIN_R
D_Q = D_QROWS * GRID_W
D_KEYS = D_WROWS * GRID_W
D_BLOCKS = N_ROWS // D_QROWS
D_KINDS = 3


def _d_window_row(kind, a, w):
    if kind == 0:
        return w < D_WIN_R, w - a + (D_WIN_R - 1)
    if kind == 1:
        return a <= w < a + D_WIN_R, w - a + (D_WIN_R - 1) - D_WIN_R // 2
    lead = D_WROWS - D_WIN_R
    return w >= lead, w - a - 1


def _bias_table_kernel(rpb_ref, tab_ref):
    h = pl.program_id(0)
    shape = (GRID_W, LANES)
    qc = lax.broadcasted_iota(jnp.int32, shape, 0)
    lane = lax.broadcasted_iota(jnp.int32, shape, 1)
    kc = lane & (GRID_W - 1)
    lower = lane < GRID_W
    idx = jnp.clip(kc - qc + (D_WIN_C - 1), 0, N_DC - 1)
    c0 = jnp.clip(qc - D_WIN_C // 2, 0, GRID_W - D_WIN_C)
    col_ok = (kc >= c0) & (kc < c0 + D_WIN_C)
    base = h * (N_DR * N_DC)
    tiles = []
    for d in range(N_DR):
        acc = jnp.zeros(shape, F32)
        for j in range(N_DC):
            acc = jnp.where(idx == j, rpb_ref[base + d * N_DC + j], acc)
        tiles.append(jnp.where(col_ok, acc * LOG2E, NEG))
    neg = jnp.full(shape, NEG, F32)
    for kind in range(D_KINDS):
        for a in range(D_QROWS):
            for wp in range(D_WROWS // 2):
                ok_lo, d_lo = _d_window_row(kind, a, 2 * wp)
                ok_hi, d_hi = _d_window_row(kind, a, 2 * wp + 1)
                t_lo = tiles[d_lo] if ok_lo else neg
                t_hi = tiles[d_hi] if ok_hi else neg
                tab_ref[0, kind, a * GRID_W:(a + 1) * GRID_W, wp * LANES:(wp + 1) * LANES] = (
                    jnp.where(lower, t_lo, t_hi))


def _bias_table(rpb_flat):
    return pl.pallas_call(
        _bias_table_kernel,
        grid=(D_HEADS,),
        in_specs=[pl.BlockSpec(memory_space=pltpu.SMEM)],
        out_specs=pl.BlockSpec((1, D_KINDS, D_Q, D_KEYS), lambda i: (i, 0, 0, 0)),
        out_shape=jax.ShapeDtypeStruct((D_HEADS, D_KINDS, D_Q, D_KEYS), F32),
        compiler_params=pltpu.CompilerParams(dimension_semantics=("arbitrary",)),
        name="bias_table",
    )(rpb_flat)


def _attn_d_kernel(q_ref, k_ref, v_ref, tab_ref, o_ref):
    blk = pl.program_id(1)
    w0 = jnp.clip(blk * D_QROWS - D_WIN_R // 2, 0, N_ROWS - D_WROWS)
    start = pl.multiple_of(w0 * GRID_W, D_Q)
    halves = _half_masks(BF16)
    quarters = _quarter_masks(BF16)

    for g in range(2):
        vq = v_ref[0, pl.ds(start, D_KEYS), g * QUAD:(g + 1) * QUAD]
        ps, ls = [], []
        for i in range(4):
            hd = 4 * g + i
            c = hd // 2
            qh = q_ref[0, :, c * LANES:(c + 1) * LANES] * halves[i % 2]
            kwin = k_ref[0, pl.ds(start, D_KEYS), c * LANES:(c + 1) * LANES]
            p, l, _ = _softmax2_parts(_dot_nt(qh, kwin) + tab_ref[hd, 0])
            ps.append(p)
            ls.append(l)
        o_ref[0, :, g * QUAD:(g + 1) * QUAD] = _quad_out(
            ps, ls, [vq * quarters[i] for i in range(4)]).astype(BF16)


def _attn_d(q, k, v, tab):
    b = q.shape[0]

    def kind(i, j):
        return (0, jnp.where(j == 0, 0, jnp.where(j == D_BLOCKS - 1, 2, 1)), 0, 0)

    return pl.pallas_call(
        _attn_d_kernel,
        grid=(b, D_BLOCKS),
        in_specs=[
            pl.BlockSpec((1, D_Q, 512), lambda i, j: (i, j, 0)),
            pl.BlockSpec((1, SEQ, 512), lambda i, j: (i, 0, 0)),
            pl.BlockSpec((1, SEQ, 512), lambda i, j: (i, 0, 0)),
            pl.BlockSpec((D_HEADS, 1, D_Q, D_KEYS), kind),
        ],
        out_specs=pl.BlockSpec((1, D_Q, 512), lambda i, j: (i, j, 0)),
        out_shape=jax.ShapeDtypeStruct((b, SEQ, 512), BF16),
        compiler_params=pltpu.CompilerParams(
            dimension_semantics=("arbitrary", "arbitrary"), vmem_limit_bytes=VMEM_LIMIT),
        name="attn_d",
    )(q, k, v, tab)


def _norm_proj_kernel(x_ref, g_ref, w_ref, o_ref):
    h = _rms_rows(x_ref[...], g_ref[...]).astype(BF16)
    o_ref[...] = _dot(h, w_ref[...]).astype(BF16)


def _norm_proj(x, g, w, layer):
    t = x.shape[0]
    n = w.shape[2]
    return pl.pallas_call(
        _norm_proj_kernel,
        grid=(t // TM,),
        in_specs=[
            pl.BlockSpec((TM, D_MODEL), lambda i: (i, 0)),
            pl.BlockSpec((1, D_MODEL), lambda i: (0, 0)),
            _layer_spec(w, layer),
        ],
        out_specs=pl.BlockSpec((TM, n), lambda i: (i, 0)),
        out_shape=jax.ShapeDtypeStruct((t, n), BF16),
        compiler_params=pltpu.CompilerParams(
            dimension_semantics=("arbitrary",), vmem_limit_bytes=VMEM_LIMIT),
        name="mem_kv_proj",
    )(x, g, w)


def _cross_kernel(x_ref, a1_ref, a2_ref, wo1_ref, wo2_ref, g_ref, wq_ref, kv_ref, wxo_ref, o_ref):
    x1 = x_ref[...] + _dot(a1_ref[...], wo1_ref[...]) + _dot(a2_ref[...], wo2_ref[...])
    h = _rms_rows(x1, g_ref[...]).astype(BF16)
    q = (_dot(h, wq_ref[...]) * (X_HEAD_DIM ** -0.5 * LOG2E)).astype(BF16)
    heads = []
    for hd in range(X_HEADS):
        sl = slice(hd * LANES, (hd + 1) * LANES)
        k = kv_ref[0, :, sl]
        v = kv_ref[0, :, X_HEADS * LANES + hd * LANES:X_HEADS * LANES + (hd + 1) * LANES]
        p, l, _ = _softmax2_parts(_dot_nt(q[:, sl], k))
        heads.append((_dot(p, v) * (1.0 / l)).astype(BF16))
    o = jnp.concatenate(heads, axis=1)
    o_ref[...] = x1 + _dot(o, wxo_ref[...])


def _cross(x, a1, a2, w_out, mix_layer, g, wq, kv, wxo, layer):
    t = x.shape[0]
    per_b = SEQ // TM
    row = lambda i: (i, 0)
    const = lambda i: (0, 0)
    return pl.pallas_call(
        _cross_kernel,
        grid=(t // TM,),
        in_specs=[
            pl.BlockSpec((TM, D_MODEL), row),
            pl.BlockSpec((TM, 512), row),
            pl.BlockSpec((TM, 512), row),
            _layer_spec(w_out, mix_layer, rows=512, row_block=0),
            _layer_spec(w_out, mix_layer, rows=512, row_block=1),
            pl.BlockSpec((1, D_MODEL), const),
            _layer_spec(wq, layer),
            pl.BlockSpec((1, MEM_LEN, 2 * X_HEADS * X_HEAD_DIM), lambda i: (i // per_b, 0, 0)),
            _layer_spec(wxo, layer),
        ],
        out_specs=pl.BlockSpec((TM, D_MODEL), row),
        out_shape=jax.ShapeDtypeStruct((t, D_MODEL), F32),
        compiler_params=pltpu.CompilerParams(
            dimension_semantics=("arbitrary",), vmem_limit_bytes=VMEM_LIMIT),
        name="outproj_cross",
    )(x, a1, a2, w_out, w_out, g, wq, kv, wxo)


def _ffn_kernel(x_ref, g_ref, wgu_ref, wd_ref, gf_ref, o_ref, *, final_norm):
    x = x_ref[...]
    h = _rms_rows(x, g_ref[...]).astype(BF16)
    acc = x
    for c in range(D_FF // FF_CHUNK):
        lo = c * FF_CHUNK
        gate = _dot(h, wgu_ref[:, lo:lo + FF_CHUNK])
        up = _dot(h, wgu_ref[:, D_FF + lo:D_FF + lo + FF_CHUNK])
        act = (gate * jax.nn.sigmoid(gate) * up).astype(BF16)
        acc = acc + _dot(act, wd_ref[lo:lo + FF_CHUNK, :])
    if final_norm:
        acc = _rms_rows(acc, gf_ref[...])
    o_ref[...] = acc


def _ffn(x, g, wgu, wd, layer, gf, final_norm):
    t = x.shape[0]
    row = lambda i: (i, 0)
    const = lambda i: (0, 0)
    return pl.pallas_call(
        functools.partial(_ffn_kernel, final_norm=final_norm),
        grid=(t // FFN_TM,),
        in_specs=[
            pl.BlockSpec((FFN_TM, D_MODEL), row),
            pl.BlockSpec((1, D_MODEL), const),
            _layer_spec(wgu, layer),
            _layer_spec(wd, layer),
            pl.BlockSpec((1, D_MODEL), const),
        ],
        out_specs=pl.BlockSpec((FFN_TM, D_MODEL), row),
        out_shape=jax.ShapeDtypeStruct((t, D_MODEL), F32),
        compiler_params=pltpu.CompilerParams(
            dimension_semantics=("arbitrary",), vmem_limit_bytes=VMEM_LIMIT),
        name="ffn_final" if final_norm else "ffn",
    )(x, g, wgu, wd, gf)


def _rope_angles(pos, dim):
    inv = ROPE_THETA ** (-jnp.arange(0, dim, 2, dtype=F32) / dim)
    return pos.astype(F32)[:, None] * inv[None, :]


def _head64_tables(ang, scale):
    d = np.arange(LANES) % HEAD_DIM
    sign = np.where(d < HEAD_DIM // 2, -1.0, 1.0).astype(np.float32)
    a = ang[:, d % (HEAD_DIM // 2)]
    return jnp.cos(a) * scale, jnp.sin(a) * sign[None, :] * scale


def _latent_tables(ang, scale):
    lane = np.arange(LANES)
    in_rope = (lane >= C_NOPE) & (lane < C_NOPE + C_ROPE)
    a = ang[:, (lane - C_NOPE) % (C_ROPE // 2)]
    sign = np.where(lane < C_NOPE + C_ROPE // 2, -1.0, 1.0).astype(np.float32)
    cos = jnp.where(in_rope[None, :], jnp.cos(a), 1.0) * scale
    sin = jnp.where(in_rope[None, :], jnp.sin(a) * sign[None, :], 0.0) * scale
    return cos, sin


def _pad_cols(w, width):
    return jnp.pad(w, ((0, 0), (0, width - w.shape[1])))


def kernel(x, mem, g_mix, w_in_ab, g_qa, g_ka, sink_b, w_out_ab, w_in_cd, g_cq, g_ckv,
           w_uq, w_ukv, rpb_d, w_out_cd, g_xq, g_mem, w_xq, w_xkv, w_xo, g_ffn,
           w_gate_up, w_down, g_final):
    b, s, d = x.shape
    assert (s, d) == (SEQ, D_MODEL) and mem.shape == (b, MEM_LEN, D_MODEL)
    depth = g_mix.shape[0]
    t = b * s

    pos = jnp.arange(s)
    ang_1d = _rope_angles(pos, HEAD_DIM)
    ang_2d = jnp.concatenate([_rope_angles(pos // GRID_W, HEAD_DIM // 2),
                              _rope_angles(pos % GRID_W, HEAD_DIM // 2)], axis=-1)
    ang_c = _rope_angles(pos, C_ROPE)
    q_scale = HEAD_DIM ** -0.5 * LOG2E
    tabs_ab = jnp.stack([*_head64_tables(ang_2d, q_scale), *_head64_tables(ang_2d, 1.0),
                         *_head64_tables(ang_1d, q_scale), *_head64_tables(ang_1d, 1.0)])
    tabs_cd = jnp.stack([*_latent_tables(ang_c, (C_NOPE + C_ROPE) ** -0.5 * LOG2E),
                         *_latent_tables(ang_c, 1.0)])
    lane = np.arange(LANES)
    block_ones = jnp.asarray(lane[:, None] // HEAD_DIM == lane[None, :] // HEAD_DIM, BF16)

    row2 = lambda v: v.reshape(1, -1).astype(F32)
    xf = x.reshape(t, d)
    mem_f = mem.reshape(b * MEM_LEN, d)
    w_in_ab_h, w_out_ab_h, w_out_cd_h = (w.astype(BF16) for w in (w_in_ab, w_out_ab, w_out_cd))
    w_xq_h, w_xkv_h, w_xo_h = (w.astype(BF16) for w in (w_xq, w_xkv, w_xo))
    w_gate_up_h, w_down_h = w_gate_up.astype(BF16), w_down.astype(BF16)

    for i in range(depth):
        j = i // 2
        if i % 2 == 0:
            gq = row2(jnp.tile(g_qa[j], 2))
            gk = row2(jnp.tile(g_ka[j], 2))
            qa, ka, va, qb, kb, vb = _inproj_ab(
                xf, row2(g_mix[i]), w_in_ab_h, j, gq, gk, block_ones, tabs_ab)
            sh = lambda a: a.reshape(b, s, a.shape[-1])
            o1 = _attn_a(sh(qa), sh(ka), sh(va)).reshape(t, 512)
            o2 = _attn_b(sink_b[j].astype(F32), sh(qb), sh(kb), sh(vb)).reshape(t, 512)
            w_out = w_out_ab_h
        else:
            w_in = w_in_cd[j]
            w_in = jnp.concatenate(
                [w_in[:, :C_Q_RANK + C_KV_RANK],
                 _pad_cols(jnp.pad(w_in[:, 384:416], ((0, 0), (C_NOPE, 0))), LANES),
                 w_in[:, 416:]], axis=1).astype(BF16)
            wuq = w_uq[j].reshape(C_Q_RANK, C_HEADS, C_NOPE + C_ROPE)
            wuq = jnp.pad(wuq, ((0, 0), (0, 0), (0, LANES - C_NOPE - C_ROPE)))
            wuq = wuq.reshape(C_Q_RANK, C_HEADS * LANES).astype(BF16)
            wukv = w_ukv[j].reshape(C_KV_RANK, C_HEADS, C_NOPE + C_V)
            wuk = jnp.pad(wukv[:, :, :C_NOPE], ((0, 0), (0, 0), (0, LANES - C_NOPE)))
            wuk = wuk.reshape(C_KV_RANK, C_HEADS * LANES).astype(BF16)
            wuv = wukv[:, :, C_NOPE:].reshape(C_KV_RANK, C_HEADS * C_V).astype(BF16)
            qc, kc, vc, qd, kd, vd = _inproj_cd(
                xf, row2(g_mix[i]), w_in, row2(g_cq[j]), row2(g_ckv[j]), wuq, wuk, wuv, tabs_cd)
            sh = lambda a: a.reshape(b, s, a.shape[-1])
            tab = _bias_table(rpb_d[j].reshape(-1).astype(F32))
            o1 = _attn_c(sh(qc), sh(kc), sh(vc)).reshape(t, 512)
            o2 = _attn_d(sh(qd), sh(kd), sh(vd), tab).reshape(t, 512)
            w_out = w_out_cd_h

        kv = _norm_proj(mem_f, row2(g_mem[i]), w_xkv_h, i)
        kv = kv.reshape(b, MEM_LEN, 2 * X_HEADS * X_HEAD_DIM)
        xf = _cross(xf, o1, o2, w_out, j, row2(g_xq[i]), w_xq_h, kv, w_xo_h, i)
        xf = _ffn(xf, row2(g_ffn[i]), w_gate_up_h, w_down_h, i,
                  row2(g_final), final_norm=(i == depth - 1))
    return xf.reshape(b, s, d)
```

```python
import functools

import numpy as np
import jax
import jax.numpy as jnp
from jax import lax
from jax.experimental import pallas as pl
from jax.experimental.pallas import tpu as pltpu

D_MODEL = 1024
SEQ = 2048
HEAD_DIM = 64
GRID_W = 64
MEM_LEN = 256
ROPE_THETA = 10000.0
EPS = 1e-6
NEG = -1e30

A_HEADS = 8
B_HEADS = 8
B_WINDOW = 128
C_HEADS = 8
C_Q_RANK = 256
C_KV_RANK = 128
C_NOPE = 64
C_ROPE = 32
C_V = 64
D_HEADS = 8
D_WIN_R = 8
D_WIN_C = 16
X_HEADS = 4
X_HEAD_DIM = 128
D_FF = 2816

LANES = 128
QUAD = 256
LOG2E = 1.4426950408889634
TM = 512
TM_SUB = 128
CD_SUB = 256
FFN_TM = 512
TQ_DENSE = 512
TQ_SUB = 256
FF_CHUNK = 256
VMEM_LIMIT = 56 * 1024 * 1024

F32 = jnp.float32
BF16 = jnp.bfloat16

_NT = (((1,), (1,)), ((), ()))


def _layer_spec(w, layer, rows=None, row_block=0):
    k = w.shape[1] if rows is None else rows
    return pl.BlockSpec((None, k, w.shape[2]), lambda *_: (layer, row_block, 0))


def _dot(a, b):
    return jnp.dot(a, b, preferred_element_type=F32)


def _dot_nt(a, b):
    return lax.dot_general(a, b, _NT, preferred_element_type=F32)


def _rms_rows(xf, g):
    return xf * lax.rsqrt(jnp.mean(xf * xf, axis=-1, keepdims=True) + EPS) * g


def _lane_iota(shape):
    return lax.broadcasted_iota(jnp.int32, shape, len(shape) - 1)


def _rope_chunk(x, cos, sin_signed, first_mask, half):
    rot = jnp.where(first_mask, pltpu.roll(x, LANES - half, 1), pltpu.roll(x, half, 1))
    return x * cos + rot * sin_signed


def _half_masks(dtype):
    lane = _lane_iota((1, LANES))
    return (lane < 64).astype(dtype), (lane >= 64).astype(dtype)


def _quarter_masks(dtype):
    lane = _lane_iota((1, QUAD))
    return [((lane >= HEAD_DIM * i) & (lane < HEAD_DIM * (i + 1))).astype(dtype) for i in range(4)]


def _softmax2_parts(s):
    m = jnp.max(s, axis=-1, keepdims=True)
    p = jnp.exp2(s - m)
    return p.astype(BF16), jnp.sum(p, axis=-1, keepdims=True), m


def _quad_out(ps, ls, ws):
    o = _dot(ps[0], ws[0]) * (1.0 / ls[0])
    for p, l, w in zip(ps[1:], ls[1:], ws[1:]):
        o = o + _dot(p, w) * (1.0 / l)
    return o


def _inproj_ab_kernel(x_ref, g_ref, w_ref, gq_ref, gk_ref, bd_ref, tab_ref,
                      qa_ref, ka_ref, va_ref, qb_ref, kb_ref, vb_ref):
    lane = _lane_iota((TM_SUB, LANES))
    first = (lane & 63) < 32
    lo = lane < 64
    bd = bd_ref[...]

    def head_norm(zc, gain):
        ss = _dot((zc * zc).astype(BF16), bd)
        return zc * lax.rsqrt(ss * (1.0 / HEAD_DIM) + EPS) * gain

    def dup(c):
        r = pltpu.roll(c, 64, 1)
        return jnp.where(lo, c, r), jnp.where(lo, r, c)

    for sub in range(TM // TM_SUB):
        rows = slice(sub * TM_SUB, (sub + 1) * TM_SUB)
        h = _rms_rows(x_ref[rows, :], g_ref[...]).astype(BF16)
        z = _dot(h, w_ref[...])

        def chunk(i, z=z):
            return z[:, i * LANES:(i + 1) * LANES]

        def tab(i, rows=rows):
            return tab_ref[i, rows, :]

        for c in range(4):
            n = head_norm(chunk(c), gq_ref[...])
            qa_ref[rows, c * LANES:(c + 1) * LANES] = _rope_chunk(
                n, tab(0), tab(1), first, 32).astype(BF16)
        k = _rope_chunk(head_norm(chunk(4), gk_ref[...]), tab(2), tab(3), first, 32)
        k0, k1 = dup(k)
        ka_ref[rows, 0:LANES] = k0.astype(BF16)
        ka_ref[rows, LANES:2 * LANES] = k1.astype(BF16)
        v0, v1 = dup(chunk(5))
        va_ref[rows, 0:LANES] = v0.astype(BF16)
        va_ref[rows, LANES:2 * LANES] = v1.astype(BF16)
        for c in range(4):
            qb_ref[rows, c * LANES:(c + 1) * LANES] = _rope_chunk(
                chunk(6 + c), tab(4), tab(5), first, 32).astype(BF16)
        k = _rope_chunk(chunk(10), tab(6), tab(7), first, 32)
        k0, k1 = dup(k)
        kb_ref[rows, 0:LANES] = k0.astype(BF16)
        kb_ref[rows, LANES:2 * LANES] = k1.astype(BF16)
        v0, v1 = dup(chunk(11))
        vb_ref[rows, 0:LANES] = v0.astype(BF16)
        vb_ref[rows, LANES:2 * LANES] = v1.astype(BF16)


def _inproj_ab(x, g, w, layer, gq, gk, bd, tabs):
    t = x.shape[0]
    n_pos = SEQ // TM
    row = lambda i: (i, 0)
    const = lambda i: (0, 0)
    outs = [(t, 512), (t, 256), (t, 256), (t, 512), (t, 256), (t, 256)]
    return pl.pallas_call(
        _inproj_ab_kernel,
        grid=(t // TM,),
        in_specs=[
            pl.BlockSpec((TM, D_MODEL), row),
            pl.BlockSpec((1, D_MODEL), const),
            _layer_spec(w, layer),
            pl.BlockSpec((1, LANES), const),
            pl.BlockSpec((1, LANES), const),
            pl.BlockSpec((LANES, LANES), const),
            pl.BlockSpec((8, TM, LANES), lambda i: (0, i % n_pos, 0)),
        ],
        out_specs=[pl.BlockSpec((TM, n), row) for _, n in outs],
        out_shape=[jax.ShapeDtypeStruct(s, BF16) for s in outs],
        compiler_params=pltpu.CompilerParams(
            dimension_semantics=("arbitrary",), vmem_limit_bytes=VMEM_LIMIT),
        name="inproj_ab",
    )(x, g, w, gq, gk, bd, tabs)


def _place_shared_v(v2, masks):
    zero = jnp.zeros_like(v2)
    lo, hi = v2 * masks[0], v2 * masks[1]
    return [jnp.concatenate(parts, axis=1)
            for parts in ((lo, zero), (hi, zero), (zero, lo), (zero, hi))]


def _attn_a_kernel(q_ref, k_ref, v_ref, o_ref, w_ref):
    masks = _half_masks(BF16)

    @pl.when(pl.program_id(1) == 0)
    def _():
        for kvh in range(2):
            placed = _place_shared_v(v_ref[0, :, kvh * LANES:(kvh + 1) * LANES], masks)
            for i in range(4):
                w_ref[kvh, i] = placed[i]

    for kvh in range(2):
        k2 = k_ref[0, :, kvh * LANES:(kvh + 1) * LANES]
        for sub in range(TQ_DENSE // TQ_SUB):
            rows = slice(sub * TQ_SUB, (sub + 1) * TQ_SUB)
            ps, ls = [], []
            for i in range(4):
                c = 2 * kvh + i // 2
                qc = q_ref[0, rows, c * LANES:(c + 1) * LANES]
                p, l, _ = _softmax2_parts(_dot_nt(qc * masks[i % 2], k2))
                ps.append(p)
                ls.append(l)
            o_ref[0, rows, kvh * QUAD:(kvh + 1) * QUAD] = _quad_out(
                ps, ls, [w_ref[kvh, i] for i in range(4)]).astype(BF16)


def _attn_a(q, k, v):
    b = q.shape[0]
    return pl.pallas_call(
        _attn_a_kernel,
        grid=(b, SEQ // TQ_DENSE),
        in_specs=[
            pl.BlockSpec((1, TQ_DENSE, 512), lambda i, j: (i, j, 0)),
            pl.BlockSpec((1, SEQ, 256), lambda i, j: (i, 0, 0)),
            pl.BlockSpec((1, SEQ, 256), lambda i, j: (i, 0, 0)),
        ],
        out_specs=pl.BlockSpec((1, TQ_DENSE, 512), lambda i, j: (i, j, 0)),
        out_shape=jax.ShapeDtypeStruct((b, SEQ, 512), BF16),
        scratch_shapes=[pltpu.VMEM((2, 4, SEQ, QUAD), BF16)],
        compiler_params=pltpu.CompilerParams(
            dimension_semantics=("arbitrary", "arbitrary"), vmem_limit_bytes=VMEM_LIMIT),
        name="attn_a",
    )(q, k, v)


B_STEP = 512
B_BLOCK = 256
B_KEYS = B_BLOCK + 2 * B_WINDOW


def _attn_b_kernel(sink_ref, q_ref, k_ref, v_ref, o_ref):
    masks = _half_masks(BF16)
    for sub in range(B_STEP // B_BLOCK):
        n = pl.program_id(1) * (B_STEP // B_BLOCK) + sub
        rows = slice(sub * B_BLOCK, (sub + 1) * B_BLOCK)
        start = pl.multiple_of(jnp.clip(n * B_BLOCK - B_WINDOW, 0, SEQ - B_KEYS), B_WINDOW)
        q_pos = n * B_BLOCK + lax.broadcasted_iota(jnp.int32, (B_BLOCK, B_KEYS), 0)
        k_pos = start + lax.broadcasted_iota(jnp.int32, (B_BLOCK, B_KEYS), 1)
        delta = k_pos - q_pos
        valid = (delta <= B_WINDOW) & (delta >= -B_WINDOW)

        def head(qh, k2, sink, valid=valid):
            s = jnp.where(valid, _dot_nt(qh, k2), NEG)
            m = jnp.maximum(jnp.max(s, axis=-1, keepdims=True), sink)
            p = jnp.exp2(s - m)
            return p.astype(BF16), jnp.sum(p, axis=-1, keepdims=True) + jnp.exp2(sink - m)

        for kvh in range(2):
            k2 = k_ref[0, pl.ds(start, B_KEYS), kvh * LANES:(kvh + 1) * LANES]
            ws = _place_shared_v(
                v_ref[0, pl.ds(start, B_KEYS), kvh * LANES:(kvh + 1) * LANES], masks)
            ps, ls = [], []
            for i in range(4):
                c = 2 * kvh + i // 2
                qc = q_ref[0, rows, c * LANES:(c + 1) * LANES]
                p, l = head(qc * masks[i % 2], k2, sink_ref[4 * kvh + i] * LOG2E)
                ps.append(p)
                ls.append(l)
            o_ref[0, rows, kvh * QUAD:(kvh + 1) * QUAD] = _quad_out(ps, ls, ws).astype(BF16)


def _attn_b(sink, q, k, v):
    b = q.shape[0]
    return pl.pallas_call(
        _attn_b_kernel,
        grid=(b, SEQ // B_STEP),
        in_specs=[
            pl.BlockSpec(memory_space=pltpu.SMEM),
            pl.BlockSpec((1, B_STEP, 512), lambda i, j: (i, j, 0)),
            pl.BlockSpec((1, SEQ, 256), lambda i, j: (i, 0, 0)),
            pl.BlockSpec((1, SEQ, 256), lambda i, j: (i, 0, 0)),
        ],
        out_specs=pl.BlockSpec((1, B_STEP, 512), lambda i, j: (i, j, 0)),
        out_shape=jax.ShapeDtypeStruct((b, SEQ, 512), BF16),
        compiler_params=pltpu.CompilerParams(
            dimension_semantics=("arbitrary", "arbitrary"), vmem_limit_bytes=VMEM_LIMIT),
        name="attn_b",
    )(sink, q, k, v)


def _inproj_cd_kernel(x_ref, g_ref, w_ref, gcq_ref, gckv_ref, wuq_ref, wuk_ref, wuv_ref,
                      tab_ref, qc_ref, kc_ref, vc_ref, qd_ref, kd_ref, vd_ref):
    lane = _lane_iota((CD_SUB, LANES))
    first = lane < 80
    for sub in range(TM // CD_SUB):
        rows = slice(sub * CD_SUB, (sub + 1) * CD_SUB)
        h = _rms_rows(x_ref[rows, :], g_ref[...]).astype(BF16)
        z = _dot(h, w_ref[...])
        cq = _rms_rows(z[:, 0:256], gcq_ref[...]).astype(BF16)
        ckv = _rms_rows(z[:, 256:384], gckv_ref[...]).astype(BF16)
        q = _dot(cq, wuq_ref[...])
        kn = _dot(ckv, wuk_ref[...])
        tabs = [tab_ref[i, rows, :] for i in range(4)]
        kr = _rope_chunk(z[:, 384:512], tabs[2], tabs[3], first, 16)
        for hd in range(C_HEADS):
            sl = slice(hd * LANES, (hd + 1) * LANES)
            qc_ref[rows, sl] = _rope_chunk(q[:, sl], tabs[0], tabs[1], first, 16).astype(BF16)
            kc_ref[rows, sl] = (kn[:, sl] + kr).astype(BF16)
        vc_ref[rows, :] = _dot(ckv, wuv_ref[...]).astype(BF16)
        qd_ref[rows, :] = (z[:, 512:1024] * (HEAD_DIM ** -0.5 * LOG2E)).astype(BF16)
        kd_ref[rows, :] = z[:, 1024:1536].astype(BF16)
        vd_ref[rows, :] = z[:, 1536:2048].astype(BF16)


def _inproj_cd(x, g, w, gcq, gckv, wuq, wuk, wuv, tabs):
    t = x.shape[0]
    n_pos = SEQ // TM
    row = lambda i: (i, 0)
    const = lambda i: (0, 0)
    outs = [(t, 1024), (t, 1024), (t, 512), (t, 512), (t, 512), (t, 512)]
    return pl.pallas_call(
        _inproj_cd_kernel,
        grid=(t // TM,),
        in_specs=[
            pl.BlockSpec((TM, D_MODEL), row),
            pl.BlockSpec((1, D_MODEL), const),
            pl.BlockSpec(w.shape, const),
            pl.BlockSpec((1, C_Q_RANK), const),
            pl.BlockSpec((1, C_KV_RANK), const),
            pl.BlockSpec(wuq.shape, const),
            pl.BlockSpec(wuk.shape, const),
            pl.BlockSpec(wuv.shape, const),
            pl.BlockSpec((4, TM, LANES), lambda i: (0, i % n_pos, 0)),
        ],
        out_specs=[pl.BlockSpec((TM, n), row) for _, n in outs],
        out_shape=[jax.ShapeDtypeStruct(s, BF16) for s in outs],
        compiler_params=pltpu.CompilerParams(
            dimension_semantics=("arbitrary",), vmem_limit_bytes=VMEM_LIMIT),
        name="inproj_cd",
    )(x, g, w, gcq, gckv, wuq, wuk, wuv, tabs)


def _attn_c_kernel(q_ref, k_ref, v_ref, o_ref, w_ref):
    @pl.when(pl.program_id(1) == 0)
    def _():
        masks = _quarter_masks(BF16)
        for g in range(2):
            vq = v_ref[0, :, g * QUAD:(g + 1) * QUAD]
            for i in range(4):
                w_ref[g, i] = vq * masks[i]

    for g in range(2):
        for sub in range(TQ_DENSE // TQ_SUB):
            rows = slice(sub * TQ_SUB, (sub + 1) * TQ_SUB)
            ps, ls = [], []
            for i in range(4):
                hd = 4 * g + i
                p, l, _ = _softmax2_parts(_dot_nt(q_ref[0, rows, hd * LANES:(hd + 1) * LANES],
                                                  k_ref[0, :, hd * LANES:(hd + 1) * LANES]))
                ps.append(p)
                ls.append(l)
            o_ref[0, rows, g * QUAD:(g + 1) * QUAD] = _quad_out(
                ps, ls, [w_ref[g, i] for i in range(4)]).astype(BF16)


def _attn_c(q, k, v):
    b = q.shape[0]
    return pl.pallas_call(
        _attn_c_kernel,
        grid=(b, SEQ // TQ_DENSE),
        in_specs=[
            pl.BlockSpec((1, TQ_DENSE, 1024), lambda i, j: (i, j, 0)),
            pl.BlockSpec((1, SEQ, 1024), lambda i, j: (i, 0, 0)),
            pl.BlockSpec((1, SEQ, 512), lambda i, j: (i, 0, 0)),
        ],
        out_specs=pl.BlockSpec((1, TQ_DENSE, 512), lambda i, j: (i, j, 0)),
        out_shape=jax.ShapeDtypeStruct((b, SEQ, 512), BF16),
        scratch_shapes=[pltpu.VMEM((2, 4, SEQ, QUAD), BF16)],
        compiler_params=pltpu.CompilerParams(
            dimension_semantics=("arbitrary", "arbitrary"), vmem_limit_bytes=VMEM_LIMIT),
        name="attn_c",
    )(q, k, v)


N_ROWS = SEQ // GRID_W
N_DR = 2 * D_WIN_R - 1
N_DC = 2 * D_WIN_C - 1
D_QROWS = 4
D_WROWS = D_QROWS + D_WIN_R
D_Q = D_QROWS * GRID_W
D_KEYS = D_WROWS * GRID_W
D_BLOCKS = N_ROWS // D_QROWS
D_SUBS = 2
D_KINDS = 3


def _d_window_row(kind, a, w):
    if kind == 0:
        return w < D_WIN_R, w - a + (D_WIN_R - 1)
    if kind == 1:
        return a <= w < a + D_WIN_R, w - a + (D_WIN_R - 1) - D_WIN_R // 2
    lead = D_WROWS - D_WIN_R
    return w >= lead, w - a - 1


def _bias_table_kernel(rpb_ref, tab_ref):
    h = pl.program_id(0)
    shape = (GRID_W, LANES)
    qc = lax.broadcasted_iota(jnp.int32, shape, 0)
    lane = lax.broadcasted_iota(jnp.int32, shape, 1)
    kc = lane & (GRID_W - 1)
    lower = lane < GRID_W
    idx = jnp.clip(kc - qc + (D_WIN_C - 1), 0, N_DC - 1)
    c0 = jnp.clip(qc - D_WIN_C // 2, 0, GRID_W - D_WIN_C)
    col_ok = (kc >= c0) & (kc < c0 + D_WIN_C)
    base = h * (N_DR * N_DC)
    tiles = []
    for d in range(N_DR):
        acc = jnp.zeros(shape, F32)
        for j in range(N_DC):
            acc = jnp.where(idx == j, rpb_ref[base + d * N_DC + j], acc)
        tiles.append(jnp.where(col_ok, acc * LOG2E, NEG))
    neg = jnp.full(shape, NEG, F32)
    for kind in range(D_KINDS):
        for a in range(D_QROWS):
            for wp in range(D_WROWS // 2):
                ok_lo, d_lo = _d_window_row(kind, a, 2 * wp)
                ok_hi, d_hi = _d_window_row(kind, a, 2 * wp + 1)
                t_lo = tiles[d_lo] if ok_lo else neg
                t_hi = tiles[d_hi] if ok_hi else neg
                tab_ref[0, kind, a * GRID_W:(a + 1) * GRID_W, wp * LANES:(wp + 1) * LANES] = (
                    jnp.where(lower, t_lo, t_hi))


def _bias_table(rpb_flat):
    return pl.pallas_call(
        _bias_table_kernel,
        grid=(D_HEADS,),
        in_specs=[pl.BlockSpec(memory_space=pltpu.SMEM)],
        out_specs=pl.BlockSpec((1, D_KINDS, D_Q, D_KEYS), lambda i: (i, 0, 0, 0)),
        out_shape=jax.ShapeDtypeStruct((D_HEADS, D_KINDS, D_Q, D_KEYS), F32),
        compiler_params=pltpu.CompilerParams(dimension_semantics=("arbitrary",)),
        name="bias_table",
    )(rpb_flat)


def _attn_d_kernel(q_ref, k_ref, v_ref, tab0_ref, tab1_ref, o_ref):
    halves = _half_masks(BF16)
    quarters = _quarter_masks(BF16)

    for sub, tab_ref in enumerate((tab0_ref, tab1_ref)):
        blk = pl.program_id(1) * D_SUBS + sub
        rows = slice(sub * D_Q, (sub + 1) * D_Q)
        w0 = jnp.clip(blk * D_QROWS - D_WIN_R // 2, 0, N_ROWS - D_WROWS)
        start = pl.multiple_of(w0 * GRID_W, D_Q)
        for g in range(2):
            vq = v_ref[0, pl.ds(start, D_KEYS), g * QUAD:(g + 1) * QUAD]
            ps, ls = [], []
            for i in range(4):
                hd = 4 * g + i
                c = hd // 2
                qh = q_ref[0, rows, c * LANES:(c + 1) * LANES] * halves[i % 2]
                kwin = k_ref[0, pl.ds(start, D_KEYS), c * LANES:(c + 1) * LANES]
                p, l, _ = _softmax2_parts(_dot_nt(qh, kwin) + tab_ref[hd, 0])
                ps.append(p)
                ls.append(l)
            o_ref[0, rows, g * QUAD:(g + 1) * QUAD] = _quad_out(
                ps, ls, [vq * quarters[i] for i in range(4)]).astype(BF16)


def _attn_d(q, k, v, tab):
    b = q.shape[0]

    def kind(sub):
        def index(i, j):
            blk = j * D_SUBS + sub
            return (0, jnp.where(blk == 0, 0, jnp.where(blk == D_BLOCKS - 1, 2, 1)), 0, 0)
        return index

    return pl.pallas_call(
        _attn_d_kernel,
        grid=(b, D_BLOCKS // D_SUBS),
        in_specs=[
            pl.BlockSpec((1, D_SUBS * D_Q, 512), lambda i, j: (i, j, 0)),
            pl.BlockSpec((1, SEQ, 512), lambda i, j: (i, 0, 0)),
            pl.BlockSpec((1, SEQ, 512), lambda i, j: (i, 0, 0)),
            pl.BlockSpec((D_HEADS, 1, D_Q, D_KEYS), kind(0)),
            pl.BlockSpec((D_HEADS, 1, D_Q, D_KEYS), kind(1)),
        ],
        out_specs=pl.BlockSpec((1, D_SUBS * D_Q, 512), lambda i, j: (i, j, 0)),
        out_shape=jax.ShapeDtypeStruct((b, SEQ, 512), BF16),
        compiler_params=pltpu.CompilerParams(
            dimension_semantics=("arbitrary", "arbitrary"), vmem_limit_bytes=VMEM_LIMIT),
        name="attn_d",
    )(q, k, v, tab, tab)


def _norm_proj_kernel(x_ref, g_ref, w_ref, o_ref):
    h = _rms_rows(x_ref[...], g_ref[...]).astype(BF16)
    o_ref[...] = _dot(h, w_ref[...]).astype(BF16)


def _norm_proj(x, g, w, layer):
    t = x.shape[0]
    n = w.shape[2]
    return pl.pallas_call(
        _norm_proj_kernel,
        grid=(t // TM,),
        in_specs=[
            pl.BlockSpec((TM, D_MODEL), lambda i: (i, 0)),
            pl.BlockSpec((1, D_MODEL), lambda i: (0, 0)),
            _layer_spec(w, layer),
        ],
        out_specs=pl.BlockSpec((TM, n), lambda i: (i, 0)),
        out_shape=jax.ShapeDtypeStruct((t, n), BF16),
        compiler_params=pltpu.CompilerParams(
            dimension_semantics=("arbitrary",), vmem_limit_bytes=VMEM_LIMIT),
        name="mem_kv_proj",
    )(x, g, w)


def _cross_kernel(x_ref, a1_ref, a2_ref, wo1_ref, wo2_ref, g_ref, wq_ref, kv_ref, wxo_ref, o_ref):
    x1 = x_ref[...] + _dot(a1_ref[...], wo1_ref[...]) + _dot(a2_ref[...], wo2_ref[...])
    h = _rms_rows(x1, g_ref[...]).astype(BF16)
    q = (_dot(h, wq_ref[...]) * (X_HEAD_DIM ** -0.5 * LOG2E)).astype(BF16)
    heads = []
    for hd in range(X_HEADS):
        sl = slice(hd * LANES, (hd + 1) * LANES)
        k = kv_ref[0, :, sl]
        v = kv_ref[0, :, X_HEADS * LANES + hd * LANES:X_HEADS * LANES + (hd + 1) * LANES]
        p, l, _ = _softmax2_parts(_dot_nt(q[:, sl], k))
        heads.append((_dot(p, v) * (1.0 / l)).astype(BF16))
    o = jnp.concatenate(heads, axis=1)
    o_ref[...] = x1 + _dot(o, wxo_ref[...])


def _cross(x, a1, a2, w_out, mix_layer, g, wq, kv, wxo, layer):
    t = x.shape[0]
    per_b = SEQ // TM
    row = lambda i: (i, 0)
    const = lambda i: (0, 0)
    return pl.pallas_call(
        _cross_kernel,
        grid=(t // TM,),
        in_specs=[
            pl.BlockSpec((TM, D_MODEL), row),
            pl.BlockSpec((TM, 512), row),
            pl.BlockSpec((TM, 512), row),
            _layer_spec(w_out, mix_layer, rows=512, row_block=0),
            _layer_spec(w_out, mix_layer, rows=512, row_block=1),
            pl.BlockSpec((1, D_MODEL), const),
            _layer_spec(wq, layer),
            pl.BlockSpec((1, MEM_LEN, 2 * X_HEADS * X_HEAD_DIM), lambda i: (i // per_b, 0, 0)),
            _layer_spec(wxo, layer),
        ],
        out_specs=pl.BlockSpec((TM, D_MODEL), row),
        out_shape=jax.ShapeDtypeStruct((t, D_MODEL), F32),
        compiler_params=pltpu.CompilerParams(
            dimension_semantics=("arbitrary",), vmem_limit_bytes=VMEM_LIMIT),
        name="outproj_cross",
    )(x, a1, a2, w_out, w_out, g, wq, kv, wxo)


def _ffn_kernel(x_ref, g_ref, wgu_ref, wd_ref, gf_ref, o_ref, *, final_norm):
    x = x_ref[...]
    h = _rms_rows(x, g_ref[...]).astype(BF16)
    acc = x
    for c in range(D_FF // FF_CHUNK):
        lo = c * FF_CHUNK
        gate = _dot(h, wgu_ref[:, lo:lo + FF_CHUNK])
        up = _dot(h, wgu_ref[:, D_FF + lo:D_FF + lo + FF_CHUNK])
        act = (gate * jax.nn.sigmoid(gate) * up).astype(BF16)
        acc = acc + _dot(act, wd_ref[lo:lo + FF_CHUNK, :])
    if final_norm:
        acc = _rms_rows(acc, gf_ref[...])
    o_ref[...] = acc


def _ffn(x, g, wgu, wd, layer, gf, final_norm):
    t = x.shape[0]
    row = lambda i: (i, 0)
    const = lambda i: (0, 0)
    return pl.pallas_call(
        functools.partial(_ffn_kernel, final_norm=final_norm),
        grid=(t // FFN_TM,),
        in_specs=[
            pl.BlockSpec((FFN_TM, D_MODEL), row),
            pl.BlockSpec((1, D_MODEL), const),
            _layer_spec(wgu, layer),
            _layer_spec(wd, layer),
            pl.BlockSpec((1, D_MODEL), const),
        ],
        out_specs=pl.BlockSpec((FFN_TM, D_MODEL), row),
        out_shape=jax.ShapeDtypeStruct((t, D_MODEL), F32),
        compiler_params=pltpu.CompilerParams(
            dimension_semantics=("arbitrary",), vmem_limit_bytes=VMEM_LIMIT),
        name="ffn_final" if final_norm else "ffn",
    )(x, g, wgu, wd, gf)


def _rope_angles(pos, dim):
    inv = ROPE_THETA ** (-jnp.arange(0, dim, 2, dtype=F32) / dim)
    return pos.astype(F32)[:, None] * inv[None, :]


def _head64_tables(ang, scale):
    d = np.arange(LANES) % HEAD_DIM
    sign = np.where(d < HEAD_DIM // 2, -1.0, 1.0).astype(np.float32)
    a = ang[:, d % (HEAD_DIM // 2)]
    return jnp.cos(a) * scale, jnp.sin(a) * sign[None, :] * scale


def _latent_tables(ang, scale):
    lane = np.arange(LANES)
    in_rope = (lane >= C_NOPE) & (lane < C_NOPE + C_ROPE)
    a = ang[:, (lane - C_NOPE) % (C_ROPE // 2)]
    sign = np.where(lane < C_NOPE + C_ROPE // 2, -1.0, 1.0).astype(np.float32)
    cos = jnp.where(in_rope[None, :], jnp.cos(a), 1.0) * scale
    sin = jnp.where(in_rope[None, :], jnp.sin(a) * sign[None, :], 0.0) * scale
    return cos, sin


def _pad_cols(w, width):
    return jnp.pad(w, ((0, 0), (0, width - w.shape[1])))


def kernel(x, mem, g_mix, w_in_ab, g_qa, g_ka, sink_b, w_out_ab, w_in_cd, g_cq, g_ckv,
           w_uq, w_ukv, rpb_d, w_out_cd, g_xq, g_mem, w_xq, w_xkv, w_xo, g_ffn,
           w_gate_up, w_down, g_final):
    b, s, d = x.shape
    assert (s, d) == (SEQ, D_MODEL) and mem.shape == (b, MEM_LEN, D_MODEL)
    depth = g_mix.shape[0]
    t = b * s

    pos = jnp.arange(s)
    ang_1d = _rope_angles(pos, HEAD_DIM)
    ang_2d = jnp.concatenate([_rope_angles(pos // GRID_W, HEAD_DIM // 2),
                              _rope_angles(pos % GRID_W, HEAD_DIM // 2)], axis=-1)
    ang_c = _rope_angles(pos, C_ROPE)
    q_scale = HEAD_DIM ** -0.5 * LOG2E
    tabs_ab = jnp.stack([*_head64_tables(ang_2d, q_scale), *_head64_tables(ang_2d, 1.0),
                         *_head64_tables(ang_1d, q_scale), *_head64_tables(ang_1d, 1.0)])
    tabs_cd = jnp.stack([*_latent_tables(ang_c, (C_NOPE + C_ROPE) ** -0.5 * LOG2E),
                         *_latent_tables(ang_c, 1.0)])
    lane = np.arange(LANES)
    block_ones = jnp.asarray(lane[:, None] // HEAD_DIM == lane[None, :] // HEAD_DIM, BF16)

    row2 = lambda v: v.reshape(1, -1).astype(F32)
    xf = x.reshape(t, d)
    mem_f = mem.reshape(b * MEM_LEN, d)
    w_in_ab_h, w_out_ab_h, w_out_cd_h = (w.astype(BF16) for w in (w_in_ab, w_out_ab, w_out_cd))
    w_xq_h, w_xkv_h, w_xo_h = (w.astype(BF16) for w in (w_xq, w_xkv, w_xo))
    w_gate_up_h, w_down_h = w_gate_up.astype(BF16), w_down.astype(BF16)

    for i in range(depth):
        j = i // 2
        if i % 2 == 0:
            gq = row2(jnp.tile(g_qa[j], 2))
            gk = row2(jnp.tile(g_ka[j], 2))
            qa, ka, va, qb, kb, vb = _inproj_ab(
                xf, row2(g_mix[i]), w_in_ab_h, j, gq, gk, block_ones, tabs_ab)
            sh = lambda a: a.reshape(b, s, a.shape[-1])
            o1 = _attn_a(sh(qa), sh(ka), sh(va)).reshape(t, 512)
            o2 = _attn_b(sink_b[j].astype(F32), sh(qb), sh(kb), sh(vb)).reshape(t, 512)
            w_out = w_out_ab_h
        else:
            w_in = w_in_cd[j]
            w_in = jnp.concatenate(
                [w_in[:, :C_Q_RANK + C_KV_RANK],
                 _pad_cols(jnp.pad(w_in[:, 384:416], ((0, 0), (C_NOPE, 0))), LANES),
                 w_in[:, 416:]], axis=1).astype(BF16)
            wuq = w_uq[j].reshape(C_Q_RANK, C_HEADS, C_NOPE + C_ROPE)
            wuq = jnp.pad(wuq, ((0, 0), (0, 0), (0, LANES - C_NOPE - C_ROPE)))
            wuq = wuq.reshape(C_Q_RANK, C_HEADS * LANES).astype(BF16)
            wukv = w_ukv[j].reshape(C_KV_RANK, C_HEADS, C_NOPE + C_V)
            wuk = jnp.pad(wukv[:, :, :C_NOPE], ((0, 0), (0, 0), (0, LANES - C_NOPE)))
            wuk = wuk.reshape(C_KV_RANK, C_HEADS * LANES).astype(BF16)
            wuv = wukv[:, :, C_NOPE:].reshape(C_KV_RANK, C_HEADS * C_V).astype(BF16)
            qc, kc, vc, qd, kd, vd = _inproj_cd(
                xf, row2(g_mix[i]), w_in, row2(g_cq[j]), row2(g_ckv[j]), wuq, wuk, wuv, tabs_cd)
            sh = lambda a: a.reshape(b, s, a.shape[-1])
            tab = _bias_table(rpb_d[j].reshape(-1).astype(F32))
            o1 = _attn_c(sh(qc), sh(kc), sh(vc)).reshape(t, 512)
            o2 = _attn_d(sh(qd), sh(kd), sh(vd), tab).reshape(t, 512)
            w_out = w_out_cd_h

        kv = _norm_proj(mem_f, row2(g_mem[i]), w_xkv_h, i)
        kv = kv.reshape(b, MEM_LEN, 2 * X_HEADS * X_HEAD_DIM)
        xf = _cross(xf, o1, o2, w_out, j, row2(g_xq[i]), w_xq_h, kv, w_xo_h, i)
        xf = _ffn(xf, row2(g_ffn[i]), w_gate_up_h, w_down_h, i,
                  row2(g_final), final_norm=(i == depth - 1))
    return xf.reshape(b, s, d)
```

```python
import functools

import numpy as np
import jax
import jax.numpy as jnp
from jax import lax
from jax.experimental import pallas as pl
from jax.experimental.pallas import tpu as pltpu

D_MODEL = 1024
SEQ = 2048
HEAD_DIM = 64
GRID_W = 64
MEM_LEN = 256
ROPE_THETA = 10000.0
EPS = 1e-6
NEG = -1e30

A_HEADS = 8
B_HEADS = 8
B_WINDOW = 128
C_HEADS = 8
C_Q_RANK = 256
C_KV_RANK = 128
C_NOPE = 64
C_ROPE = 32
C_V = 64
D_HEADS = 8
D_WIN_R = 8
D_WIN_C = 16
X_HEADS = 4
X_HEAD_DIM = 128
D_FF = 2816

LANES = 128
QUAD = 256
LOG2E = 1.4426950408889634
TM = 512
TM_SUB = 128
CD_SUB = 256
FFN_TM = 1024
TQ_DENSE = 512
TQ_SUB = 256
FF_CHUNK = 256
VMEM_LIMIT = 56 * 1024 * 1024

F32 = jnp.float32
BF16 = jnp.bfloat16

_NT = (((1,), (1,)), ((), ()))


def _layer_spec(w, layer, rows=None, row_block=0):
    k = w.shape[1] if rows is None else rows
    return pl.BlockSpec((None, k, w.shape[2]), lambda *_: (layer, row_block, 0),
                        pipeline_mode=pl.Buffered(1))


def _dot(a, b):
    return jnp.dot(a, b, preferred_element_type=F32)


def _dot_nt(a, b):
    return lax.dot_general(a, b, _NT, preferred_element_type=F32)


def _rms_rows(xf, g):
    return xf * lax.rsqrt(jnp.mean(xf * xf, axis=-1, keepdims=True) + EPS) * g


def _lane_iota(shape):
    return lax.broadcasted_iota(jnp.int32, shape, len(shape) - 1)


def _rope_chunk(x, cos, sin_signed, first_mask, half):
    rot = jnp.where(first_mask, pltpu.roll(x, LANES - half, 1), pltpu.roll(x, half, 1))
    return x * cos + rot * sin_signed


def _half_masks(dtype):
    lane = _lane_iota((1, LANES))
    return (lane < 64).astype(dtype), (lane >= 64).astype(dtype)


def _quarter_masks(dtype):
    lane = _lane_iota((1, QUAD))
    return [((lane >= HEAD_DIM * i) & (lane < HEAD_DIM * (i + 1))).astype(dtype) for i in range(4)]


def _softmax2_parts(s):
    m = jnp.max(s, axis=-1, keepdims=True)
    p = jnp.exp2(s - m)
    return p.astype(BF16), jnp.sum(p, axis=-1, keepdims=True), m


def _quad_out(ps, ls, ws):
    o = _dot(ps[0], ws[0]) * (1.0 / ls[0])
    for p, l, w in zip(ps[1:], ls[1:], ws[1:]):
        o = o + _dot(p, w) * (1.0 / l)
    return o


def _inproj_ab_kernel(x_ref, g_ref, w_ref, gq_ref, gk_ref, bd_ref, tab_ref,
                      qa_ref, ka_ref, va_ref, qb_ref, kb_ref, vb_ref):
    lane = _lane_iota((TM_SUB, LANES))
    first = (lane & 63) < 32
    lo = lane < 64
    bd = bd_ref[...]

    def head_norm(zc, gain):
        ss = _dot((zc * zc).astype(BF16), bd)
        return zc * lax.rsqrt(ss * (1.0 / HEAD_DIM) + EPS) * gain

    def dup(c):
        r = pltpu.roll(c, 64, 1)
        return jnp.where(lo, c, r), jnp.where(lo, r, c)

    for sub in range(TM // TM_SUB):
        rows = slice(sub * TM_SUB, (sub + 1) * TM_SUB)
        h = _rms_rows(x_ref[rows, :], g_ref[...]).astype(BF16)
        z = _dot(h, w_ref[...])

        def chunk(i, z=z):
            return z[:, i * LANES:(i + 1) * LANES]

        def tab(i, rows=rows):
            return tab_ref[i, rows, :]

        for c in range(4):
            n = head_norm(chunk(c), gq_ref[...])
            qa_ref[rows, c * LANES:(c + 1) * LANES] = _rope_chunk(
                n, tab(0), tab(1), first, 32).astype(BF16)
        k = _rope_chunk(head_norm(chunk(4), gk_ref[...]), tab(2), tab(3), first, 32)
        k0, k1 = dup(k)
        ka_ref[rows, 0:LANES] = k0.astype(BF16)
        ka_ref[rows, LANES:2 * LANES] = k1.astype(BF16)
        v0, v1 = dup(chunk(5))
        va_ref[rows, 0:LANES] = v0.astype(BF16)
        va_ref[rows, LANES:2 * LANES] = v1.astype(BF16)
        for c in range(4):
            qb_ref[rows, c * LANES:(c + 1) * LANES] = _rope_chunk(
                chunk(6 + c), tab(4), tab(5), first, 32).astype(BF16)
        k = _rope_chunk(chunk(10), tab(6), tab(7), first, 32)
        k0, k1 = dup(k)
        kb_ref[rows, 0:LANES] = k0.astype(BF16)
        kb_ref[rows, LANES:2 * LANES] = k1.astype(BF16)
        v0, v1 = dup(chunk(11))
        vb_ref[rows, 0:LANES] = v0.astype(BF16)
        vb_ref[rows, LANES:2 * LANES] = v1.astype(BF16)


def _inproj_ab(x, g, w, layer, gq, gk, bd, tabs):
    t = x.shape[0]
    n_pos = SEQ // TM
    row = lambda i: (i, 0)
    const = lambda i: (0, 0)
    outs = [(t, 512), (t, 256), (t, 256), (t, 512), (t, 256), (t, 256)]
    return pl.pallas_call(
        _inproj_ab_kernel,
        grid=(t // TM,),
        in_specs=[
            pl.BlockSpec((TM, D_MODEL), row),
            pl.BlockSpec((1, D_MODEL), const),
            _layer_spec(w, layer),
            pl.BlockSpec((1, LANES), const),
            pl.BlockSpec((1, LANES), const),
            pl.BlockSpec((LANES, LANES), const),
            pl.BlockSpec((8, TM, LANES), lambda i: (0, i % n_pos, 0)),
        ],
        out_specs=[pl.BlockSpec((TM, n), row) for _, n in outs],
        out_shape=[jax.ShapeDtypeStruct(s, BF16) for s in outs],
        compiler_params=pltpu.CompilerParams(
            dimension_semantics=("arbitrary",), vmem_limit_bytes=VMEM_LIMIT),
        name="inproj_ab",
    )(x, g, w, gq, gk, bd, tabs)


def _place_shared_v(v2, masks):
    zero = jnp.zeros_like(v2)
    lo, hi = v2 * masks[0], v2 * masks[1]
    return [jnp.concatenate(parts, axis=1)
            for parts in ((lo, zero), (hi, zero), (zero, lo), (zero, hi))]


def _attn_a_units(q_ref, k_ref, o_ref, w_ref, masks):
    def unit(kvh, sub):
        rows = slice(sub * TQ_SUB, (sub + 1) * TQ_SUB)
        k2 = k_ref[0, :, kvh * LANES:(kvh + 1) * LANES]
        ps, ls = [], []
        for i in range(4):
            c = 2 * kvh + i // 2
            qc = q_ref[0, rows, c * LANES:(c + 1) * LANES]
            p, l, _ = _softmax2_parts(_dot_nt(qc * masks[i % 2], k2))
            ps.append(p)
            ls.append(l)
        o_ref[0, rows, kvh * QUAD:(kvh + 1) * QUAD] = _quad_out(
            ps, ls, [w_ref[kvh, i] for i in range(4)]).astype(BF16)

    return [functools.partial(unit, kvh, sub)
            for kvh in range(2) for sub in range(TQ_DENSE // TQ_SUB)]


B_BLOCK = 256
B_KEYS = B_BLOCK + 2 * B_WINDOW


def _attn_b_units(sink_ref, q_ref, k_ref, v_ref, o_ref, masks):
    def unit(sub, kvh):
        n = pl.program_id(1) * (TQ_DENSE // B_BLOCK) + sub
        rows = slice(sub * B_BLOCK, (sub + 1) * B_BLOCK)
        start = pl.multiple_of(jnp.clip(n * B_BLOCK - B_WINDOW, 0, SEQ - B_KEYS), B_WINDOW)
        q_pos = n * B_BLOCK + lax.broadcasted_iota(jnp.int32, (B_BLOCK, B_KEYS), 0)
        k_pos = start + lax.broadcasted_iota(jnp.int32, (B_BLOCK, B_KEYS), 1)
        delta = k_pos - q_pos
        valid = (delta <= B_WINDOW) & (delta >= -B_WINDOW)
        k2 = k_ref[0, pl.ds(start, B_KEYS), kvh * LANES:(kvh + 1) * LANES]
        ws = _place_shared_v(v_ref[0, pl.ds(start, B_KEYS), kvh * LANES:(kvh + 1) * LANES], masks)
        ps, ls = [], []
        for i in range(4):
            c = 2 * kvh + i // 2
            qc = q_ref[0, rows, c * LANES:(c + 1) * LANES]
            sink = sink_ref[4 * kvh + i] * LOG2E
            s = jnp.where(valid, _dot_nt(qc * masks[i % 2], k2), NEG)
            m = jnp.maximum(jnp.max(s, axis=-1, keepdims=True), sink)
            p = jnp.exp2(s - m)
            ps.append(p.astype(BF16))
            ls.append(jnp.sum(p, axis=-1, keepdims=True) + jnp.exp2(sink - m))
        o_ref[0, rows, kvh * QUAD:(kvh + 1) * QUAD] = _quad_out(ps, ls, ws).astype(BF16)

    return [functools.partial(unit, sub, kvh)
            for sub in range(TQ_DENSE // B_BLOCK) for kvh in range(2)]


def _attn_ab_kernel(sink_ref, qa_ref, ka_ref, va_ref, qb_ref, kb_ref, vb_ref,
                    oa_ref, ob_ref, w_ref):
    masks = _half_masks(BF16)

    @pl.when(pl.program_id(1) == 0)
    def _():
        for kvh in range(2):
            placed = _place_shared_v(va_ref[0, :, kvh * LANES:(kvh + 1) * LANES], masks)
            for i in range(4):
                w_ref[kvh, i] = placed[i]

    a_units = _attn_a_units(qa_ref, ka_ref, oa_ref, w_ref, masks)
    b_units = _attn_b_units(sink_ref, qb_ref, kb_ref, vb_ref, ob_ref, masks)
    for a_unit, b_unit in zip(a_units, b_units):
        a_unit()
        b_unit()


def _attn_ab(sink, qa, ka, va, qb, kb, vb):
    b = qa.shape[0]
    tile = lambda i, j: (i, j, 0)
    whole = lambda i, j: (i, 0, 0)
    out = jax.ShapeDtypeStruct((b, SEQ, 512), BF16)
    return pl.pallas_call(
        _attn_ab_kernel,
        grid=(b, SEQ // TQ_DENSE),
        in_specs=[
            pl.BlockSpec(memory_space=pltpu.SMEM),
            pl.BlockSpec((1, TQ_DENSE, 512), tile),
            pl.BlockSpec((1, SEQ, 256), whole),
            pl.BlockSpec((1, SEQ, 256), whole),
            pl.BlockSpec((1, TQ_DENSE, 512), tile),
            pl.BlockSpec((1, SEQ, 256), whole),
            pl.BlockSpec((1, SEQ, 256), whole),
        ],
        out_specs=[pl.BlockSpec((1, TQ_DENSE, 512), tile)] * 2,
        out_shape=[out, out],
        scratch_shapes=[pltpu.VMEM((2, 4, SEQ, QUAD), BF16)],
        compiler_params=pltpu.CompilerParams(
            dimension_semantics=("arbitrary", "arbitrary"), vmem_limit_bytes=VMEM_LIMIT),
        name="attn_ab",
    )(sink, qa, ka, va, qb, kb, vb)


def _inproj_cd_kernel(x_ref, g_ref, w_ref, gcq_ref, gckv_ref, wuq_ref, wuk_ref, wuv_ref,
                      tab_ref, qc_ref, kc_ref, vc_ref, qd_ref, kd_ref, vd_ref):
    lane = _lane_iota((CD_SUB, LANES))
    first = lane < 80
    for sub in range(TM // CD_SUB):
        rows = slice(sub * CD_SUB, (sub + 1) * CD_SUB)
        h = _rms_rows(x_ref[rows, :], g_ref[...]).astype(BF16)
        z = _dot(h, w_ref[...])
        cq = _rms_rows(z[:, 0:256], gcq_ref[...]).astype(BF16)
        ckv = _rms_rows(z[:, 256:384], gckv_ref[...]).astype(BF16)
        q = _dot(cq, wuq_ref[...])
        kn = _dot(ckv, wuk_ref[...])
        tabs = [tab_ref[i, rows, :] for i in range(4)]
        kr = _rope_chunk(z[:, 384:512], tabs[2], tabs[3], first, 16)
        for hd in range(C_HEADS):
            sl = slice(hd * LANES, (hd + 1) * LANES)
            qc_ref[rows, sl] = _rope_chunk(q[:, sl], tabs[0], tabs[1], first, 16).astype(BF16)
            kc_ref[rows, sl] = (kn[:, sl] + kr).astype(BF16)
        vc_ref[rows, :] = _dot(ckv, wuv_ref[...]).astype(BF16)
        qd_ref[rows, :] = (z[:, 512:1024] * (HEAD_DIM ** -0.5 * LOG2E)).astype(BF16)
        kd_ref[rows, :] = z[:, 1024:1536].astype(BF16)
        vd_ref[rows, :] = z[:, 1536:2048].astype(BF16)


def _inproj_cd(x, g, w, gcq, gckv, wuq, wuk, wuv, tabs):
    t = x.shape[0]
    n_pos = SEQ // TM
    row = lambda i: (i, 0)
    const = lambda i: (0, 0)
    outs = [(t, 1024), (t, 1024), (t, 512), (t, 512), (t, 512), (t, 512)]
    return pl.pallas_call(
        _inproj_cd_kernel,
        grid=(t // TM,),
        in_specs=[
            pl.BlockSpec((TM, D_MODEL), row),
            pl.BlockSpec((1, D_MODEL), const),
            pl.BlockSpec(w.shape, const),
            pl.BlockSpec((1, C_Q_RANK), const),
            pl.BlockSpec((1, C_KV_RANK), const),
            pl.BlockSpec(wuq.shape, const),
            pl.BlockSpec(wuk.shape, const),
            pl.BlockSpec(wuv.shape, const),
            pl.BlockSpec((4, TM, LANES), lambda i: (0, i % n_pos, 0)),
        ],
        out_specs=[pl.BlockSpec((TM, n), row) for _, n in outs],
        out_shape=[jax.ShapeDtypeStruct(s, BF16) for s in outs],
        compiler_params=pltpu.CompilerParams(
            dimension_semantics=("arbitrary",), vmem_limit_bytes=VMEM_LIMIT),
        name="inproj_cd",
    )(x, g, w, gcq, gckv, wuq, wuk, wuv, tabs)


def _attn_c_kernel(q_ref, k_ref, v_ref, o_ref, w_ref):
    @pl.when(pl.program_id(1) == 0)
    def _():
        masks = _quarter_masks(BF16)
        for g in range(2):
            vq = v_ref[0, :, g * QUAD:(g + 1) * QUAD]
            for i in range(4):
                w_ref[g, i] = vq * masks[i]

    for g in range(2):
        for sub in range(TQ_DENSE // TQ_SUB):
            rows = slice(sub * TQ_SUB, (sub + 1) * TQ_SUB)
            ps, ls = [], []
            for i in range(4):
                hd = 4 * g + i
                p, l, _ = _softmax2_parts(_dot_nt(q_ref[0, rows, hd * LANES:(hd + 1) * LANES],
                                                  k_ref[0, :, hd * LANES:(hd + 1) * LANES]))
                ps.append(p)
                ls.append(l)
            o_ref[0, rows, g * QUAD:(g + 1) * QUAD] = _quad_out(
                ps, ls, [w_ref[g, i] for i in range(4)]).astype(BF16)


def _attn_c(q, k, v):
    b = q.shape[0]
    return pl.pallas_call(
        _attn_c_kernel,
        grid=(b, SEQ // TQ_DENSE),
        in_specs=[
            pl.BlockSpec((1, TQ_DENSE, 1024), lambda i, j: (i, j, 0)),
            pl.BlockSpec((1, SEQ, 1024), lambda i, j: (i, 0, 0)),
            pl.BlockSpec((1, SEQ, 512), lambda i, j: (i, 0, 0)),
        ],
        out_specs=pl.BlockSpec((1, TQ_DENSE, 512), lambda i, j: (i, j, 0)),
        out_shape=jax.ShapeDtypeStruct((b, SEQ, 512), BF16),
        scratch_shapes=[pltpu.VMEM((2, 4, SEQ, QUAD), BF16)],
        compiler_params=pltpu.CompilerParams(
            dimension_semantics=("arbitrary", "arbitrary"), vmem_limit_bytes=VMEM_LIMIT),
        name="attn_c",
    )(q, k, v)


N_ROWS = SEQ // GRID_W
N_DR = 2 * D_WIN_R - 1
N_DC = 2 * D_WIN_C - 1
D_QROWS = 4
D_WROWS = D_QROWS + D_WIN_R
D_Q = D_QROWS * GRID_W
D_KEYS = D_WROWS * GRID_W
D_BLOCKS = N_ROWS // D_QROWS
D_SUBS = 2
D_KINDS = 3


def _d_window_row(kind, a, w):
    if kind == 0:
        return w < D_WIN_R, w - a + (D_WIN_R - 1)
    if kind == 1:
        return a <= w < a + D_WIN_R, w - a + (D_WIN_R - 1) - D_WIN_R // 2
    lead = D_WROWS - D_WIN_R
    return w >= lead, w - a - 1


def _bias_table_kernel(rpb_ref, tab_ref):
    h = pl.program_id(0)
    shape = (GRID_W, LANES)
    qc = lax.broadcasted_iota(jnp.int32, shape, 0)
    lane = lax.broadcasted_iota(jnp.int32, shape, 1)
    kc = lane & (GRID_W - 1)
    lower = lane < GRID_W
    idx = jnp.clip(kc - qc + (D_WIN_C - 1), 0, N_DC - 1)
    c0 = jnp.clip(qc - D_WIN_C // 2, 0, GRID_W - D_WIN_C)
    col_ok = (kc >= c0) & (kc < c0 + D_WIN_C)
    base = h * (N_DR * N_DC)
    tiles = []
    for d in range(N_DR):
        acc = jnp.zeros(shape, F32)
        for j in range(N_DC):
            acc = jnp.where(idx == j, rpb_ref[base + d * N_DC + j], acc)
        tiles.append(jnp.where(col_ok, acc * LOG2E, NEG))
    neg = jnp.full(shape, NEG, F32)
    for kind in range(D_KINDS):
        for a in range(D_QROWS):
            for wp in range(D_WROWS // 2):
                ok_lo, d_lo = _d_window_row(kind, a, 2 * wp)
                ok_hi, d_hi = _d_window_row(kind, a, 2 * wp + 1)
                t_lo = tiles[d_lo] if ok_lo else neg
                t_hi = tiles[d_hi] if ok_hi else neg
                tab_ref[0, kind, a * GRID_W:(a + 1) * GRID_W, wp * LANES:(wp + 1) * LANES] = (
                    jnp.where(lower, t_lo, t_hi))


def _bias_table(rpb_flat, n_layers):
    return pl.pallas_call(
        _bias_table_kernel,
        grid=(n_layers * D_HEADS,),
        in_specs=[pl.BlockSpec(memory_space=pltpu.SMEM)],
        out_specs=pl.BlockSpec((1, D_KINDS, D_Q, D_KEYS), lambda i: (i, 0, 0, 0)),
        out_shape=jax.ShapeDtypeStruct((n_layers * D_HEADS, D_KINDS, D_Q, D_KEYS), F32),
        compiler_params=pltpu.CompilerParams(dimension_semantics=("arbitrary",)),
        name="bias_table",
    )(rpb_flat)


def _attn_d_kernel(q_ref, k_ref, v_ref, tab0_ref, tab1_ref, o_ref):
    halves = _half_masks(BF16)
    quarters = _quarter_masks(BF16)

    for sub, tab_ref in enumerate((tab0_ref, tab1_ref)):
        blk = pl.program_id(1) * D_SUBS + sub
        rows = slice(sub * D_Q, (sub + 1) * D_Q)
        w0 = jnp.clip(blk * D_QROWS - D_WIN_R // 2, 0, N_ROWS - D_WROWS)
        start = pl.multiple_of(w0 * GRID_W, D_Q)
        for g in range(2):
            vq = v_ref[0, pl.ds(start, D_KEYS), g * QUAD:(g + 1) * QUAD]
            ps, ls = [], []
            for i in range(4):
                hd = 4 * g + i
                c = hd // 2
                qh = q_ref[0, rows, c * LANES:(c + 1) * LANES] * halves[i % 2]
                kwin = k_ref[0, pl.ds(start, D_KEYS), c * LANES:(c + 1) * LANES]
                p, l, _ = _softmax2_parts(_dot_nt(qh, kwin) + tab_ref[hd, 0])
                ps.append(p)
                ls.append(l)
            o_ref[0, rows, g * QUAD:(g + 1) * QUAD] = _quad_out(
                ps, ls, [vq * quarters[i] for i in range(4)]).astype(BF16)


def _attn_d(q, k, v, tab, layer):
    b = q.shape[0]

    def kind(sub):
        def index(i, j):
            blk = j * D_SUBS + sub
            return (layer, jnp.where(blk == 0, 0, jnp.where(blk == D_BLOCKS - 1, 2, 1)), 0, 0)
        return index

    return pl.pallas_call(
        _attn_d_kernel,
        grid=(b, D_BLOCKS // D_SUBS),
        in_specs=[
            pl.BlockSpec((1, D_SUBS * D_Q, 512), lambda i, j: (i, j, 0)),
            pl.BlockSpec((1, SEQ, 512), lambda i, j: (i, 0, 0)),
            pl.BlockSpec((1, SEQ, 512), lambda i, j: (i, 0, 0)),
            pl.BlockSpec((D_HEADS, 1, D_Q, D_KEYS), kind(0)),
            pl.BlockSpec((D_HEADS, 1, D_Q, D_KEYS), kind(1)),
        ],
        out_specs=pl.BlockSpec((1, D_SUBS * D_Q, 512), lambda i, j: (i, j, 0)),
        out_shape=jax.ShapeDtypeStruct((b, SEQ, 512), BF16),
        compiler_params=pltpu.CompilerParams(
            dimension_semantics=("arbitrary", "arbitrary"), vmem_limit_bytes=VMEM_LIMIT),
        name="attn_d",
    )(q, k, v, tab, tab)


def _norm_proj_kernel(x_ref, g_ref, w_ref, o_ref):
    h = _rms_rows(x_ref[...], g_ref[...]).astype(BF16)
    o_ref[...] = _dot(h, w_ref[...]).astype(BF16)


def _norm_proj(x, g, w):
    t = x.shape[0]
    depth, _, n = w.shape
    return pl.pallas_call(
        _norm_proj_kernel,
        grid=(depth, t // TM),
        in_specs=[
            pl.BlockSpec((TM, D_MODEL), lambda l, i: (i, 0)),
            pl.BlockSpec((None, 1, D_MODEL), lambda l, i: (l, 0, 0)),
            pl.BlockSpec((None, D_MODEL, n), lambda l, i: (l, 0, 0)),
        ],
        out_specs=pl.BlockSpec((None, TM, n), lambda l, i: (l, i, 0)),
        out_shape=jax.ShapeDtypeStruct((depth, t, n), BF16),
        compiler_params=pltpu.CompilerParams(
            dimension_semantics=("arbitrary", "arbitrary"), vmem_limit_bytes=VMEM_LIMIT),
        name="mem_kv_proj",
    )(x, g, w)


def _cross_kernel(x_ref, a1_ref, a2_ref, wo1_ref, wo2_ref, g_ref, wq_ref, kv_ref, wxo_ref, o_ref):
    x1 = x_ref[...] + _dot(a1_ref[...], wo1_ref[...]) + _dot(a2_ref[...], wo2_ref[...])
    h = _rms_rows(x1, g_ref[...]).astype(BF16)
    q = (_dot(h, wq_ref[...]) * (X_HEAD_DIM ** -0.5 * LOG2E)).astype(BF16)
    heads = []
    for hd in range(X_HEADS):
        sl = slice(hd * LANES, (hd + 1) * LANES)
        k = kv_ref[0, :, sl]
        v = kv_ref[0, :, X_HEADS * LANES + hd * LANES:X_HEADS * LANES + (hd + 1) * LANES]
        p, l, _ = _softmax2_parts(_dot_nt(q[:, sl], k))
        heads.append((_dot(p, v) * (1.0 / l)).astype(BF16))
    o = jnp.concatenate(heads, axis=1)
    o_ref[...] = x1 + _dot(o, wxo_ref[...])


def _cross(x, a1, a2, w_out, mix_layer, g, wq, kv, wxo, layer):
    t = x.shape[0]
    per_b = SEQ // TM
    row = lambda i: (i, 0)
    const = lambda i: (0, 0)
    return pl.pallas_call(
        _cross_kernel,
        grid=(t // TM,),
        in_specs=[
            pl.BlockSpec((TM, D_MODEL), row),
            pl.BlockSpec((TM, 512), row),
            pl.BlockSpec((TM, 512), row),
            _layer_spec(w_out, mix_layer, rows=512, row_block=0),
            _layer_spec(w_out, mix_layer, rows=512, row_block=1),
            pl.BlockSpec((1, D_MODEL), const),
            _layer_spec(wq, layer),
            pl.BlockSpec((None, 1, MEM_LEN, 2 * X_HEADS * X_HEAD_DIM),
                         lambda i: (layer, i // per_b, 0, 0)),
            _layer_spec(wxo, layer),
        ],
        out_specs=pl.BlockSpec((TM, D_MODEL), row),
        out_shape=jax.ShapeDtypeStruct((t, D_MODEL), F32),
        compiler_params=pltpu.CompilerParams(
            dimension_semantics=("arbitrary",), vmem_limit_bytes=VMEM_LIMIT),
        name="outproj_cross",
    )(x, a1, a2, w_out, w_out, g, wq, kv, wxo)


def _ffn_kernel(x_ref, g_ref, wgu_ref, wd_ref, gf_ref, o_ref, *, final_norm):
    x = x_ref[...]
    h = _rms_rows(x, g_ref[...]).astype(BF16)
    acc = x
    for c in range(D_FF // FF_CHUNK):
        lo = c * FF_CHUNK
        gate = _dot(h, wgu_ref[:, lo:lo + FF_CHUNK])
        up = _dot(h, wgu_ref[:, D_FF + lo:D_FF + lo + FF_CHUNK])
        act = (gate * jax.nn.sigmoid(gate) * up).astype(BF16)
        acc = acc + _dot(act, wd_ref[lo:lo + FF_CHUNK, :])
    if final_norm:
        acc = _rms_rows(acc, gf_ref[...])
    o_ref[...] = acc


def _ffn(x, g, wgu, wd, layer, gf, final_norm):
    t = x.shape[0]
    row = lambda i: (i, 0)
    const = lambda i: (0, 0)
    return pl.pallas_call(
        functools.partial(_ffn_kernel, final_norm=final_norm),
        grid=(t // FFN_TM,),
        in_specs=[
            pl.BlockSpec((FFN_TM, D_MODEL), row),
            pl.BlockSpec((1, D_MODEL), const),
            _layer_spec(wgu, layer),
            _layer_spec(wd, layer),
            pl.BlockSpec((1, D_MODEL), const),
        ],
        out_specs=pl.BlockSpec((FFN_TM, D_MODEL), row),
        out_shape=jax.ShapeDtypeStruct((t, D_MODEL), F32),
        compiler_params=pltpu.CompilerParams(
            dimension_semantics=("arbitrary",), vmem_limit_bytes=VMEM_LIMIT),
        name="ffn_final" if final_norm else "ffn",
    )(x, g, wgu, wd, gf)


def _rope_angles(pos, dim):
    inv = ROPE_THETA ** (-jnp.arange(0, dim, 2, dtype=F32) / dim)
    return pos.astype(F32)[:, None] * inv[None, :]


def _head64_tables(ang, scale):
    d = np.arange(LANES) % HEAD_DIM
    sign = np.where(d < HEAD_DIM // 2, -1.0, 1.0).astype(np.float32)
    a = ang[:, d % (HEAD_DIM // 2)]
    return jnp.cos(a) * scale, jnp.sin(a) * sign[None, :] * scale


def _latent_tables(ang, scale):
    lane = np.arange(LANES)
    in_rope = (lane >= C_NOPE) & (lane < C_NOPE + C_ROPE)
    a = ang[:, (lane - C_NOPE) % (C_ROPE // 2)]
    sign = np.where(lane < C_NOPE + C_ROPE // 2, -1.0, 1.0).astype(np.float32)
    cos = jnp.where(in_rope[None, :], jnp.cos(a), 1.0) * scale
    sin = jnp.where(in_rope[None, :], jnp.sin(a) * sign[None, :], 0.0) * scale
    return cos, sin


def _pad_cols(w, width):
    return jnp.pad(w, ((0, 0), (0, width - w.shape[1])))


def kernel(x, mem, g_mix, w_in_ab, g_qa, g_ka, sink_b, w_out_ab, w_in_cd, g_cq, g_ckv,
           w_uq, w_ukv, rpb_d, w_out_cd, g_xq, g_mem, w_xq, w_xkv, w_xo, g_ffn,
           w_gate_up, w_down, g_final):
    b, s, d = x.shape
    assert (s, d) == (SEQ, D_MODEL) and mem.shape == (b, MEM_LEN, D_MODEL)
    depth = g_mix.shape[0]
    t = b * s

    pos = jnp.arange(s)
    ang_1d = _rope_angles(pos, HEAD_DIM)
    ang_2d = jnp.concatenate([_rope_angles(pos // GRID_W, HEAD_DIM // 2),
                              _rope_angles(pos % GRID_W, HEAD_DIM // 2)], axis=-1)
    ang_c = _rope_angles(pos, C_ROPE)
    q_scale = HEAD_DIM ** -0.5 * LOG2E
    tabs_ab = jnp.stack([*_head64_tables(ang_2d, q_scale), *_head64_tables(ang_2d, 1.0),
                         *_head64_tables(ang_1d, q_scale), *_head64_tables(ang_1d, 1.0)])
    tabs_cd = jnp.stack([*_latent_tables(ang_c, (C_NOPE + C_ROPE) ** -0.5 * LOG2E),
                         *_latent_tables(ang_c, 1.0)])
    lane = np.arange(LANES)
    block_ones = jnp.asarray(lane[:, None] // HEAD_DIM == lane[None, :] // HEAD_DIM, BF16)

    row2 = lambda v: v.reshape(1, -1).astype(F32)
    xf = x.reshape(t, d)
    mem_f = mem.reshape(b * MEM_LEN, d)
    w_in_ab_h, w_out_ab_h, w_out_cd_h = (w.astype(BF16) for w in (w_in_ab, w_out_ab, w_out_cd))
    w_xq_h, w_xkv_h, w_xo_h = (w.astype(BF16) for w in (w_xq, w_xkv, w_xo))
    w_gate_up_h, w_down_h = w_gate_up.astype(BF16), w_down.astype(BF16)
    kv_all = _norm_proj(mem_f, g_mem.reshape(depth, 1, d).astype(F32), w_xkv_h)
    kv_all = kv_all.reshape(depth, b, MEM_LEN, 2 * X_HEADS * X_HEAD_DIM)
    bias_tabs = _bias_table(rpb_d.reshape(-1).astype(F32), rpb_d.shape[0])

    for i in range(depth):
        j = i // 2
        if i % 2 == 0:
            gq = row2(jnp.tile(g_qa[j], 2))
            gk = row2(jnp.tile(g_ka[j], 2))
            qa, ka, va, qb, kb, vb = _inproj_ab(
                xf, row2(g_mix[i]), w_in_ab_h, j, gq, gk, block_ones, tabs_ab)
            sh = lambda a: a.reshape(b, s, a.shape[-1])
            o1, o2 = _attn_ab(sink_b[j].astype(F32), sh(qa), sh(ka), sh(va),
                              sh(qb), sh(kb), sh(vb))
            o1, o2 = o1.reshape(t, 512), o2.reshape(t, 512)
            w_out = w_out_ab_h
        else:
            w_in = w_in_cd[j]
            w_in = jnp.concatenate(
                [w_in[:, :C_Q_RANK + C_KV_RANK],
                 _pad_cols(jnp.pad(w_in[:, 384:416], ((0, 0), (C_NOPE, 0))), LANES),
                 w_in[:, 416:]], axis=1).astype(BF16)
            wuq = w_uq[j].reshape(C_Q_RANK, C_HEADS, C_NOPE + C_ROPE)
            wuq = jnp.pad(wuq, ((0, 0), (0, 0), (0, LANES - C_NOPE - C_ROPE)))
            wuq = wuq.reshape(C_Q_RANK, C_HEADS * LANES).astype(BF16)
            wukv = w_ukv[j].reshape(C_KV_RANK, C_HEADS, C_NOPE + C_V)
            wuk = jnp.pad(wukv[:, :, :C_NOPE], ((0, 0), (0, 0), (0, LANES - C_NOPE)))
            wuk = wuk.reshape(C_KV_RANK, C_HEADS * LANES).astype(BF16)
            wuv = wukv[:, :, C_NOPE:].reshape(C_KV_RANK, C_HEADS * C_V).astype(BF16)
            qc, kc, vc, qd, kd, vd = _inproj_cd(
                xf, row2(g_mix[i]), w_in, row2(g_cq[j]), row2(g_ckv[j]), wuq, wuk, wuv, tabs_cd)
            sh = lambda a: a.reshape(b, s, a.shape[-1])
            o1 = _attn_c(sh(qc), sh(kc), sh(vc)).reshape(t, 512)
            o2 = _attn_d(sh(qd), sh(kd), sh(vd), bias_tabs, j).reshape(t, 512)
            w_out = w_out_cd_h

        xf = _cross(xf, o1, o2, w_out, j, row2(g_xq[i]), w_xq_h, kv_all, w_xo_h, i)
        xf = _ffn(xf, row2(g_ffn[i]), w_gate_up_h, w_down_h, i,
                  row2(g_final), final_norm=(i == depth - 1))
    return xf.reshape(b, s, d)
```

```python
import functools

import numpy as np
import jax
import jax.numpy as jnp
from jax import lax
from jax.experimental import pallas as pl
from jax.experimental.pallas import tpu as pltpu

D_MODEL = 1024
SEQ = 2048
HEAD_DIM = 64
GRID_W = 64
MEM_LEN = 256
ROPE_THETA = 10000.0
EPS = 1e-6
NEG = -1e30

A_HEADS = 8
B_HEADS = 8
B_WINDOW = 128
C_HEADS = 8
C_Q_RANK = 256
C_KV_RANK = 128
C_NOPE = 64
C_ROPE = 32
C_V = 64
D_HEADS = 8
D_WIN_R = 8
D_WIN_C = 16
X_HEADS = 4
X_HEAD_DIM = 128
D_FF = 2816

LANES = 128
QUAD = 256
LOG2E = 1.4426950408889634
TM = 512
TM_SUB = 128
CD_SUB = 256
TAIL_TM = 1024
TAIL_SUB = 512
TQ_DENSE = 512
TQ_SUB = 256
FF_CHUNK = 256
VMEM_LIMIT = 56 * 1024 * 1024

F32 = jnp.float32
BF16 = jnp.bfloat16

_NT = (((1,), (1,)), ((), ()))


def _layer_spec(w, layer, rows=None, row_block=0):
    k = w.shape[1] if rows is None else rows
    return pl.BlockSpec((None, k, w.shape[2]), lambda *_: (layer, row_block, 0),
                        pipeline_mode=pl.Buffered(1))


def _dot(a, b):
    return jnp.dot(a, b, preferred_element_type=F32)


def _dot_nt(a, b):
    return lax.dot_general(a, b, _NT, preferred_element_type=F32)


def _rms_rows(xf, g):
    return xf * lax.rsqrt(jnp.mean(xf * xf, axis=-1, keepdims=True) + EPS) * g


def _lane_iota(shape):
    return lax.broadcasted_iota(jnp.int32, shape, len(shape) - 1)


def _rope_chunk(x, cos, sin_signed, first_mask, half):
    rot = jnp.where(first_mask, pltpu.roll(x, LANES - half, 1), pltpu.roll(x, half, 1))
    return x * cos + rot * sin_signed


def _half_masks(dtype):
    lane = _lane_iota((1, LANES))
    return (lane < 64).astype(dtype), (lane >= 64).astype(dtype)


def _quarter_masks(dtype):
    lane = _lane_iota((1, QUAD))
    return [((lane >= HEAD_DIM * i) & (lane < HEAD_DIM * (i + 1))).astype(dtype) for i in range(4)]


def _softmax2_parts(s):
    m = jnp.max(s, axis=-1, keepdims=True)
    p = jnp.exp2(s - m)
    return p.astype(BF16), jnp.sum(p, axis=-1, keepdims=True), m


def _quad_out(ps, ls, ws):
    o = _dot(ps[0], ws[0]) * (1.0 / ls[0])
    for p, l, w in zip(ps[1:], ls[1:], ws[1:]):
        o = o + _dot(p, w) * (1.0 / l)
    return o


def _inproj_ab_kernel(x_ref, g_ref, w_ref, gq_ref, gk_ref, bd_ref, tab_ref,
                      qa_ref, ka_ref, va_ref, qb_ref, kb_ref, vb_ref):
    lane = _lane_iota((TM_SUB, LANES))
    first = (lane & 63) < 32
    lo = lane < 64
    bd = bd_ref[...]

    def head_norm(zc, gain):
        ss = _dot((zc * zc).astype(BF16), bd)
        return zc * lax.rsqrt(ss * (1.0 / HEAD_DIM) + EPS) * gain

    def dup(c):
        r = pltpu.roll(c, 64, 1)
        return jnp.where(lo, c, r), jnp.where(lo, r, c)

    for sub in range(TM // TM_SUB):
        rows = slice(sub * TM_SUB, (sub + 1) * TM_SUB)
        h = _rms_rows(x_ref[rows, :], g_ref[...]).astype(BF16)
        z = _dot(h, w_ref[...])

        def chunk(i, z=z):
            return z[:, i * LANES:(i + 1) * LANES]

        def tab(i, rows=rows):
            return tab_ref[i, rows, :]

        for c in range(4):
            n = head_norm(chunk(c), gq_ref[...])
            qa_ref[rows, c * LANES:(c + 1) * LANES] = _rope_chunk(
                n, tab(0), tab(1), first, 32).astype(BF16)
        k = _rope_chunk(head_norm(chunk(4), gk_ref[...]), tab(2), tab(3), first, 32)
        k0, k1 = dup(k)
        ka_ref[rows, 0:LANES] = k0.astype(BF16)
        ka_ref[rows, LANES:2 * LANES] = k1.astype(BF16)
        v0, v1 = dup(chunk(5))
        va_ref[rows, 0:LANES] = v0.astype(BF16)
        va_ref[rows, LANES:2 * LANES] = v1.astype(BF16)
        for c in range(4):
            qb_ref[rows, c * LANES:(c + 1) * LANES] = _rope_chunk(
                chunk(6 + c), tab(4), tab(5), first, 32).astype(BF16)
        k = _rope_chunk(chunk(10), tab(6), tab(7), first, 32)
        k0, k1 = dup(k)
        kb_ref[rows, 0:LANES] = k0.astype(BF16)
        kb_ref[rows, LANES:2 * LANES] = k1.astype(BF16)
        v0, v1 = dup(chunk(11))
        vb_ref[rows, 0:LANES] = v0.astype(BF16)
        vb_ref[rows, LANES:2 * LANES] = v1.astype(BF16)


def _inproj_ab(x, g, w, layer, gq, gk, bd, tabs):
    t = x.shape[0]
    n_pos = SEQ // TM
    row = lambda i: (i, 0)
    const = lambda i: (0, 0)
    outs = [(t, 512), (t, 256), (t, 256), (t, 512), (t, 256), (t, 256)]
    return pl.pallas_call(
        _inproj_ab_kernel,
        grid=(t // TM,),
        in_specs=[
            pl.BlockSpec((TM, D_MODEL), row),
            pl.BlockSpec((1, D_MODEL), const),
            _layer_spec(w, layer),
            pl.BlockSpec((1, LANES), const),
            pl.BlockSpec((1, LANES), const),
            pl.BlockSpec((LANES, LANES), const),
            pl.BlockSpec((8, TM, LANES), lambda i: (0, i % n_pos, 0)),
        ],
        out_specs=[pl.BlockSpec((TM, n), row) for _, n in outs],
        out_shape=[jax.ShapeDtypeStruct(s, BF16) for s in outs],
        compiler_params=pltpu.CompilerParams(
            dimension_semantics=("arbitrary",), vmem_limit_bytes=VMEM_LIMIT),
        name="inproj_ab",
    )(x, g, w, gq, gk, bd, tabs)


def _place_shared_v(v2, masks):
    zero = jnp.zeros_like(v2)
    lo, hi = v2 * masks[0], v2 * masks[1]
    return [jnp.concatenate(parts, axis=1)
            for parts in ((lo, zero), (hi, zero), (zero, lo), (zero, hi))]


def _attn_a_units(q_ref, k_ref, o_ref, w_ref, masks):
    def unit(kvh, sub):
        rows = slice(sub * TQ_SUB, (sub + 1) * TQ_SUB)
        k2 = k_ref[0, :, kvh * LANES:(kvh + 1) * LANES]
        ps, ls = [], []
        for i in range(4):
            c = 2 * kvh + i // 2
            qc = q_ref[0, rows, c * LANES:(c + 1) * LANES]
            p, l, _ = _softmax2_parts(_dot_nt(qc * masks[i % 2], k2))
            ps.append(p)
            ls.append(l)
        o_ref[0, rows, kvh * QUAD:(kvh + 1) * QUAD] = _quad_out(
            ps, ls, [w_ref[kvh, i] for i in range(4)]).astype(BF16)

    return [functools.partial(unit, kvh, sub)
            for kvh in range(2) for sub in range(TQ_DENSE // TQ_SUB)]


B_BLOCK = 256
B_KEYS = B_BLOCK + 2 * B_WINDOW


def _attn_b_units(sink_ref, q_ref, k_ref, v_ref, o_ref, masks):
    def unit(sub, kvh):
        n = pl.program_id(1) * (TQ_DENSE // B_BLOCK) + sub
        rows = slice(sub * B_BLOCK, (sub + 1) * B_BLOCK)
        start = pl.multiple_of(jnp.clip(n * B_BLOCK - B_WINDOW, 0, SEQ - B_KEYS), B_WINDOW)
        q_pos = n * B_BLOCK + lax.broadcasted_iota(jnp.int32, (B_BLOCK, B_KEYS), 0)
        k_pos = start + lax.broadcasted_iota(jnp.int32, (B_BLOCK, B_KEYS), 1)
        delta = k_pos - q_pos
        valid = (delta <= B_WINDOW) & (delta >= -B_WINDOW)
        k2 = k_ref[0, pl.ds(start, B_KEYS), kvh * LANES:(kvh + 1) * LANES]
        ws = _place_shared_v(v_ref[0, pl.ds(start, B_KEYS), kvh * LANES:(kvh + 1) * LANES], masks)
        ps, ls = [], []
        for i in range(4):
            c = 2 * kvh + i // 2
            qc = q_ref[0, rows, c * LANES:(c + 1) * LANES]
            sink = sink_ref[4 * kvh + i] * LOG2E
            s = jnp.where(valid, _dot_nt(qc * masks[i % 2], k2), NEG)
            m = jnp.maximum(jnp.max(s, axis=-1, keepdims=True), sink)
            p = jnp.exp2(s - m)
            ps.append(p.astype(BF16))
            ls.append(jnp.sum(p, axis=-1, keepdims=True) + jnp.exp2(sink - m))
        o_ref[0, rows, kvh * QUAD:(kvh + 1) * QUAD] = _quad_out(ps, ls, ws).astype(BF16)

    return [functools.partial(unit, sub, kvh)
            for sub in range(TQ_DENSE // B_BLOCK) for kvh in range(2)]


def _attn_ab_kernel(sink_ref, qa_ref, ka_ref, va_ref, qb_ref, kb_ref, vb_ref,
                    oa_ref, ob_ref, w_ref):
    masks = _half_masks(BF16)

    @pl.when(pl.program_id(1) == 0)
    def _():
        for kvh in range(2):
            placed = _place_shared_v(va_ref[0, :, kvh * LANES:(kvh + 1) * LANES], masks)
            for i in range(4):
                w_ref[kvh, i] = placed[i]

    a_units = _attn_a_units(qa_ref, ka_ref, oa_ref, w_ref, masks)
    b_units = _attn_b_units(sink_ref, qb_ref, kb_ref, vb_ref, ob_ref, masks)
    for a_unit, b_unit in zip(a_units, b_units):
        a_unit()
        b_unit()


def _attn_ab(sink, qa, ka, va, qb, kb, vb):
    b = qa.shape[0]
    tile = lambda i, j: (i, j, 0)
    whole = lambda i, j: (i, 0, 0)
    out = jax.ShapeDtypeStruct((b, SEQ, 512), BF16)
    return pl.pallas_call(
        _attn_ab_kernel,
        grid=(b, SEQ // TQ_DENSE),
        in_specs=[
            pl.BlockSpec(memory_space=pltpu.SMEM),
            pl.BlockSpec((1, TQ_DENSE, 512), tile),
            pl.BlockSpec((1, SEQ, 256), whole),
            pl.BlockSpec((1, SEQ, 256), whole),
            pl.BlockSpec((1, TQ_DENSE, 512), tile),
            pl.BlockSpec((1, SEQ, 256), whole),
            pl.BlockSpec((1, SEQ, 256), whole),
        ],
        out_specs=[pl.BlockSpec((1, TQ_DENSE, 512), tile)] * 2,
        out_shape=[out, out],
        scratch_shapes=[pltpu.VMEM((2, 4, SEQ, QUAD), BF16)],
        compiler_params=pltpu.CompilerParams(
            dimension_semantics=("arbitrary", "arbitrary"), vmem_limit_bytes=VMEM_LIMIT),
        name="attn_ab",
    )(sink, qa, ka, va, qb, kb, vb)


def _inproj_cd_kernel(x_ref, g_ref, w_ref, gcq_ref, gckv_ref, wuq_ref, wuk_ref, wuv_ref,
                      tab_ref, qc_ref, kc_ref, vc_ref, qd_ref, kd_ref, vd_ref):
    lane = _lane_iota((CD_SUB, LANES))
    first = lane < 80
    for sub in range(TM // CD_SUB):
        rows = slice(sub * CD_SUB, (sub + 1) * CD_SUB)
        h = _rms_rows(x_ref[rows, :], g_ref[...]).astype(BF16)
        z = _dot(h, w_ref[...])
        cq = _rms_rows(z[:, 0:256], gcq_ref[...]).astype(BF16)
        ckv = _rms_rows(z[:, 256:384], gckv_ref[...]).astype(BF16)
        q = _dot(cq, wuq_ref[...])
        kn = _dot(ckv, wuk_ref[...])
        tabs = [tab_ref[i, rows, :] for i in range(4)]
        kr = _rope_chunk(z[:, 384:512], tabs[2], tabs[3], first, 16)
        for hd in range(C_HEADS):
            sl = slice(hd * LANES, (hd + 1) * LANES)
            qc_ref[rows, sl] = _rope_chunk(q[:, sl], tabs[0], tabs[1], first, 16).astype(BF16)
            kc_ref[rows, sl] = (kn[:, sl] + kr).astype(BF16)
        vc_ref[rows, :] = _dot(ckv, wuv_ref[...]).astype(BF16)
        qd_ref[rows, :] = (z[:, 512:1024] * (HEAD_DIM ** -0.5 * LOG2E)).astype(BF16)
        kd_ref[rows, :] = z[:, 1024:1536].astype(BF16)
        vd_ref[rows, :] = z[:, 1536:2048].astype(BF16)


def _inproj_cd(x, g, w, gcq, gckv, wuq, wuk, wuv, tabs):
    t = x.shape[0]
    n_pos = SEQ // TM
    row = lambda i: (i, 0)
    const = lambda i: (0, 0)
    outs = [(t, 1024), (t, 1024), (t, 512), (t, 512), (t, 512), (t, 512)]
    return pl.pallas_call(
        _inproj_cd_kernel,
        grid=(t // TM,),
        in_specs=[
            pl.BlockSpec((TM, D_MODEL), row),
            pl.BlockSpec((1, D_MODEL), const),
            pl.BlockSpec(w.shape, const),
            pl.BlockSpec((1, C_Q_RANK), const),
            pl.BlockSpec((1, C_KV_RANK), const),
            pl.BlockSpec(wuq.shape, const),
            pl.BlockSpec(wuk.shape, const),
            pl.BlockSpec(wuv.shape, const),
            pl.BlockSpec((4, TM, LANES), lambda i: (0, i % n_pos, 0)),
        ],
        out_specs=[pl.BlockSpec((TM, n), row) for _, n in outs],
        out_shape=[jax.ShapeDtypeStruct(s, BF16) for s in outs],
        compiler_params=pltpu.CompilerParams(
            dimension_semantics=("arbitrary",), vmem_limit_bytes=VMEM_LIMIT),
        name="inproj_cd",
    )(x, g, w, gcq, gckv, wuq, wuk, wuv, tabs)


def _attn_c_kernel(q_ref, k_ref, v_ref, o_ref, w_ref):
    @pl.when(pl.program_id(1) == 0)
    def _():
        masks = _quarter_masks(BF16)
        for g in range(2):
            vq = v_ref[0, :, g * QUAD:(g + 1) * QUAD]
            for i in range(4):
                w_ref[g, i] = vq * masks[i]

    for g in range(2):
        for sub in range(TQ_DENSE // TQ_SUB):
            rows = slice(sub * TQ_SUB, (sub + 1) * TQ_SUB)
            ps, ls = [], []
            for i in range(4):
                hd = 4 * g + i
                p, l, _ = _softmax2_parts(_dot_nt(q_ref[0, rows, hd * LANES:(hd + 1) * LANES],
                                                  k_ref[0, :, hd * LANES:(hd + 1) * LANES]))
                ps.append(p)
                ls.append(l)
            o_ref[0, rows, g * QUAD:(g + 1) * QUAD] = _quad_out(
                ps, ls, [w_ref[g, i] for i in range(4)]).astype(BF16)


def _attn_c(q, k, v):
    b = q.shape[0]
    return pl.pallas_call(
        _attn_c_kernel,
        grid=(b, SEQ // TQ_DENSE),
        in_specs=[
            pl.BlockSpec((1, TQ_DENSE, 1024), lambda i, j: (i, j, 0)),
            pl.BlockSpec((1, SEQ, 1024), lambda i, j: (i, 0, 0)),
            pl.BlockSpec((1, SEQ, 512), lambda i, j: (i, 0, 0)),
        ],
        out_specs=pl.BlockSpec((1, TQ_DENSE, 512), lambda i, j: (i, j, 0)),
        out_shape=jax.ShapeDtypeStruct((b, SEQ, 512), BF16),
        scratch_shapes=[pltpu.VMEM((2, 4, SEQ, QUAD), BF16)],
        compiler_params=pltpu.CompilerParams(
            dimension_semantics=("arbitrary", "arbitrary"), vmem_limit_bytes=VMEM_LIMIT),
        name="attn_c",
    )(q, k, v)


N_ROWS = SEQ // GRID_W
N_DR = 2 * D_WIN_R - 1
N_DC = 2 * D_WIN_C - 1
D_QROWS = 4
D_WROWS = D_QROWS + D_WIN_R
D_Q = D_QROWS * GRID_W
D_KEYS = D_WROWS * GRID_W
D_BLOCKS = N_ROWS // D_QROWS
D_SUBS = 2
D_KINDS = 3


def _d_window_row(kind, a, w):
    if kind == 0:
        return w < D_WIN_R, w - a + (D_WIN_R - 1)
    if kind == 1:
        return a <= w < a + D_WIN_R, w - a + (D_WIN_R - 1) - D_WIN_R // 2
    lead = D_WROWS - D_WIN_R
    return w >= lead, w - a - 1


def _bias_table_kernel(rpb_ref, tab_ref):
    h = pl.program_id(0)
    shape = (GRID_W, LANES)
    qc = lax.broadcasted_iota(jnp.int32, shape, 0)
    lane = lax.broadcasted_iota(jnp.int32, shape, 1)
    kc = lane & (GRID_W - 1)
    lower = lane < GRID_W
    idx = jnp.clip(kc - qc + (D_WIN_C - 1), 0, N_DC - 1)
    c0 = jnp.clip(qc - D_WIN_C // 2, 0, GRID_W - D_WIN_C)
    col_ok = (kc >= c0) & (kc < c0 + D_WIN_C)
    base = h * (N_DR * N_DC)
    tiles = []
    for d in range(N_DR):
        acc = jnp.zeros(shape, F32)
        for j in range(N_DC):
            acc = jnp.where(idx == j, rpb_ref[base + d * N_DC + j], acc)
        tiles.append(jnp.where(col_ok, acc * LOG2E, NEG))
    neg = jnp.full(shape, NEG, F32)
    for kind in range(D_KINDS):
        for a in range(D_QROWS):
            for wp in range(D_WROWS // 2):
                ok_lo, d_lo = _d_window_row(kind, a, 2 * wp)
                ok_hi, d_hi = _d_window_row(kind, a, 2 * wp + 1)
                t_lo = tiles[d_lo] if ok_lo else neg
                t_hi = tiles[d_hi] if ok_hi else neg
                tab_ref[0, kind, a * GRID_W:(a + 1) * GRID_W, wp * LANES:(wp + 1) * LANES] = (
                    jnp.where(lower, t_lo, t_hi))


def _bias_table(rpb_flat, n_layers):
    return pl.pallas_call(
        _bias_table_kernel,
        grid=(n_layers * D_HEADS,),
        in_specs=[pl.BlockSpec(memory_space=pltpu.SMEM)],
        out_specs=pl.BlockSpec((1, D_KINDS, D_Q, D_KEYS), lambda i: (i, 0, 0, 0)),
        out_shape=jax.ShapeDtypeStruct((n_layers * D_HEADS, D_KINDS, D_Q, D_KEYS), F32),
        compiler_params=pltpu.CompilerParams(dimension_semantics=("arbitrary",)),
        name="bias_table",
    )(rpb_flat)


def _attn_d_kernel(q_ref, k_ref, v_ref, tab0_ref, tab1_ref, o_ref):
    halves = _half_masks(BF16)
    quarters = _quarter_masks(BF16)

    for sub, tab_ref in enumerate((tab0_ref, tab1_ref)):
        blk = pl.program_id(1) * D_SUBS + sub
        rows = slice(sub * D_Q, (sub + 1) * D_Q)
        w0 = jnp.clip(blk * D_QROWS - D_WIN_R // 2, 0, N_ROWS - D_WROWS)
        start = pl.multiple_of(w0 * GRID_W, D_Q)
        for g in range(2):
            vq = v_ref[0, pl.ds(start, D_KEYS), g * QUAD:(g + 1) * QUAD]
            ps, ls = [], []
            for i in range(4):
                hd = 4 * g + i
                c = hd // 2
                qh = q_ref[0, rows, c * LANES:(c + 1) * LANES] * halves[i % 2]
                kwin = k_ref[0, pl.ds(start, D_KEYS), c * LANES:(c + 1) * LANES]
                p, l, _ = _softmax2_parts(_dot_nt(qh, kwin) + tab_ref[hd, 0])
                ps.append(p)
                ls.append(l)
            o_ref[0, rows, g * QUAD:(g + 1) * QUAD] = _quad_out(
                ps, ls, [vq * quarters[i] for i in range(4)]).astype(BF16)


def _attn_d(q, k, v, tab, layer):
    b = q.shape[0]

    def kind(sub):
        def index(i, j):
            blk = j * D_SUBS + sub
            return (layer, jnp.where(blk == 0, 0, jnp.where(blk == D_BLOCKS - 1, 2, 1)), 0, 0)
        return index

    return pl.pallas_call(
        _attn_d_kernel,
        grid=(b, D_BLOCKS // D_SUBS),
        in_specs=[
            pl.BlockSpec((1, D_SUBS * D_Q, 512), lambda i, j: (i, j, 0)),
            pl.BlockSpec((1, SEQ, 512), lambda i, j: (i, 0, 0)),
            pl.BlockSpec((1, SEQ, 512), lambda i, j: (i, 0, 0)),
            pl.BlockSpec((D_HEADS, 1, D_Q, D_KEYS), kind(0)),
            pl.BlockSpec((D_HEADS, 1, D_Q, D_KEYS), kind(1)),
        ],
        out_specs=pl.BlockSpec((1, D_SUBS * D_Q, 512), lambda i, j: (i, j, 0)),
        out_shape=jax.ShapeDtypeStruct((b, SEQ, 512), BF16),
        compiler_params=pltpu.CompilerParams(
            dimension_semantics=("arbitrary", "arbitrary"), vmem_limit_bytes=VMEM_LIMIT),
        name="attn_d",
    )(q, k, v, tab, tab)


def _norm_proj_kernel(x_ref, g_ref, w_ref, o_ref):
    h = _rms_rows(x_ref[...], g_ref[...]).astype(BF16)
    o_ref[...] = _dot(h, w_ref[...]).astype(BF16)


def _norm_proj(x, g, w):
    t = x.shape[0]
    depth, _, n = w.shape
    return pl.pallas_call(
        _norm_proj_kernel,
        grid=(depth, t // TM),
        in_specs=[
            pl.BlockSpec((TM, D_MODEL), lambda l, i: (i, 0)),
            pl.BlockSpec((None, 1, D_MODEL), lambda l, i: (l, 0, 0)),
            pl.BlockSpec((None, D_MODEL, n), lambda l, i: (l, 0, 0)),
        ],
        out_specs=pl.BlockSpec((None, TM, n), lambda l, i: (l, i, 0)),
        out_shape=jax.ShapeDtypeStruct((depth, t, n), BF16),
        compiler_params=pltpu.CompilerParams(
            dimension_semantics=("arbitrary", "arbitrary"), vmem_limit_bytes=VMEM_LIMIT),
        name="mem_kv_proj",
    )(x, g, w)


def _cross_block(x, a1, a2, wo1_ref, wo2_ref, g_ref, wq_ref, kv_ref, wxo_ref):
    x1 = x + _dot(a1, wo1_ref[...]) + _dot(a2, wo2_ref[...])
    h = _rms_rows(x1, g_ref[...]).astype(BF16)
    q = (_dot(h, wq_ref[...]) * (X_HEAD_DIM ** -0.5 * LOG2E)).astype(BF16)
    heads = []
    for hd in range(X_HEADS):
        sl = slice(hd * LANES, (hd + 1) * LANES)
        k = kv_ref[0, :, sl]
        v = kv_ref[0, :, X_HEADS * LANES + hd * LANES:X_HEADS * LANES + (hd + 1) * LANES]
        p, l, _ = _softmax2_parts(_dot_nt(q[:, sl], k))
        heads.append((_dot(p, v) * (1.0 / l)).astype(BF16))
    return x1 + _dot(jnp.concatenate(heads, axis=1), wxo_ref[...])


def _ffn_block(x, g_ref, wgu_ref, wd_ref):
    h = _rms_rows(x, g_ref[...]).astype(BF16)
    acc = x
    for c in range(D_FF // FF_CHUNK):
        lo = c * FF_CHUNK
        gate = _dot(h, wgu_ref[:, lo:lo + FF_CHUNK])
        up = _dot(h, wgu_ref[:, D_FF + lo:D_FF + lo + FF_CHUNK])
        act = (gate * jax.nn.sigmoid(gate) * up).astype(BF16)
        acc = acc + _dot(act, wd_ref[lo:lo + FF_CHUNK, :])
    return acc


def _tail_kernel(x_ref, a1_ref, a2_ref, wo1_ref, wo2_ref, gx_ref, wq_ref, kv_ref, wxo_ref,
                 gf_ref, wgu_ref, wd_ref, gfin_ref, o_ref, *, final_norm):
    for sub in range(TAIL_TM // TAIL_SUB):
        rows = slice(sub * TAIL_SUB, (sub + 1) * TAIL_SUB)
        x2 = _cross_block(x_ref[rows, :], a1_ref[rows, :], a2_ref[rows, :],
                          wo1_ref, wo2_ref, gx_ref, wq_ref, kv_ref, wxo_ref)
        x3 = _ffn_block(x2, gf_ref, wgu_ref, wd_ref)
        if final_norm:
            x3 = _rms_rows(x3, gfin_ref[...])
        o_ref[rows, :] = x3


def _layer_tail(x, a1, a2, w_out, mix_layer, gx, wq, kv, wxo, gf, wgu, wd, layer, gfin,
                final_norm):
    t = x.shape[0]
    per_b = SEQ // TAIL_TM
    row = lambda i: (i, 0)
    const = lambda i: (0, 0)
    return pl.pallas_call(
        functools.partial(_tail_kernel, final_norm=final_norm),
        grid=(t // TAIL_TM,),
        in_specs=[
            pl.BlockSpec((TAIL_TM, D_MODEL), row),
            pl.BlockSpec((TAIL_TM, 512), row),
            pl.BlockSpec((TAIL_TM, 512), row),
            _layer_spec(w_out, mix_layer, rows=512, row_block=0),
            _layer_spec(w_out, mix_layer, rows=512, row_block=1),
            pl.BlockSpec((1, D_MODEL), const),
            _layer_spec(wq, layer),
            pl.BlockSpec((None, 1, MEM_LEN, 2 * X_HEADS * X_HEAD_DIM),
                         lambda i: (layer, i // per_b, 0, 0)),
            _layer_spec(wxo, layer),
            pl.BlockSpec((1, D_MODEL), const),
            _layer_spec(wgu, layer),
            _layer_spec(wd, layer),
            pl.BlockSpec((1, D_MODEL), const),
        ],
        out_specs=pl.BlockSpec((TAIL_TM, D_MODEL), row),
        out_shape=jax.ShapeDtypeStruct((t, D_MODEL), F32),
        compiler_params=pltpu.CompilerParams(
            dimension_semantics=("arbitrary",), vmem_limit_bytes=VMEM_LIMIT),
        name="tail_final" if final_norm else "tail",
    )(x, a1, a2, w_out, w_out, gx, wq, kv, wxo, gf, wgu, wd, gfin)


def _rope_angles(pos, dim):
    inv = ROPE_THETA ** (-jnp.arange(0, dim, 2, dtype=F32) / dim)
    return pos.astype(F32)[:, None] * inv[None, :]


def _head64_tables(ang, scale):
    d = np.arange(LANES) % HEAD_DIM
    sign = np.where(d < HEAD_DIM // 2, -1.0, 1.0).astype(np.float32)
    a = ang[:, d % (HEAD_DIM // 2)]
    return jnp.cos(a) * scale, jnp.sin(a) * sign[None, :] * scale


def _latent_tables(ang, scale):
    lane = np.arange(LANES)
    in_rope = (lane >= C_NOPE) & (lane < C_NOPE + C_ROPE)
    a = ang[:, (lane - C_NOPE) % (C_ROPE // 2)]
    sign = np.where(lane < C_NOPE + C_ROPE // 2, -1.0, 1.0).astype(np.float32)
    cos = jnp.where(in_rope[None, :], jnp.cos(a), 1.0) * scale
    sin = jnp.where(in_rope[None, :], jnp.sin(a) * sign[None, :], 0.0) * scale
    return cos, sin


def _pad_cols(w, width):
    return jnp.pad(w, ((0, 0), (0, width - w.shape[1])))


def kernel(x, mem, g_mix, w_in_ab, g_qa, g_ka, sink_b, w_out_ab, w_in_cd, g_cq, g_ckv,
           w_uq, w_ukv, rpb_d, w_out_cd, g_xq, g_mem, w_xq, w_xkv, w_xo, g_ffn,
           w_gate_up, w_down, g_final):
    b, s, d = x.shape
    assert (s, d) == (SEQ, D_MODEL) and mem.shape == (b, MEM_LEN, D_MODEL)
    depth = g_mix.shape[0]
    t = b * s

    pos = jnp.arange(s)
    ang_1d = _rope_angles(pos, HEAD_DIM)
    ang_2d = jnp.concatenate([_rope_angles(pos // GRID_W, HEAD_DIM // 2),
                              _rope_angles(pos % GRID_W, HEAD_DIM // 2)], axis=-1)
    ang_c = _rope_angles(pos, C_ROPE)
    q_scale = HEAD_DIM ** -0.5 * LOG2E
    tabs_ab = jnp.stack([*_head64_tables(ang_2d, q_scale), *_head64_tables(ang_2d, 1.0),
                         *_head64_tables(ang_1d, q_scale), *_head64_tables(ang_1d, 1.0)])
    tabs_cd = jnp.stack([*_latent_tables(ang_c, (C_NOPE + C_ROPE) ** -0.5 * LOG2E),
                         *_latent_tables(ang_c, 1.0)])
    lane = np.arange(LANES)
    block_ones = jnp.asarray(lane[:, None] // HEAD_DIM == lane[None, :] // HEAD_DIM, BF16)

    row2 = lambda v: v.reshape(1, -1).astype(F32)
    xf = x.reshape(t, d)
    mem_f = mem.reshape(b * MEM_LEN, d)
    w_in_ab_h, w_out_ab_h, w_out_cd_h = (w.astype(BF16) for w in (w_in_ab, w_out_ab, w_out_cd))
    w_xq_h, w_xkv_h, w_xo_h = (w.astype(BF16) for w in (w_xq, w_xkv, w_xo))
    w_gate_up_h, w_down_h = w_gate_up.astype(BF16), w_down.astype(BF16)
    kv_all = _norm_proj(mem_f, g_mem.reshape(depth, 1, d).astype(F32), w_xkv_h)
    kv_all = kv_all.reshape(depth, b, MEM_LEN, 2 * X_HEADS * X_HEAD_DIM)
    bias_tabs = _bias_table(rpb_d.reshape(-1).astype(F32), rpb_d.shape[0])

    for i in range(depth):
        j = i // 2
        if i % 2 == 0:
            gq = row2(jnp.tile(g_qa[j], 2))
            gk = row2(jnp.tile(g_ka[j], 2))
            qa, ka, va, qb, kb, vb = _inproj_ab(
                xf, row2(g_mix[i]), w_in_ab_h, j, gq, gk, block_ones, tabs_ab)
            sh = lambda a: a.reshape(b, s, a.shape[-1])
            o1, o2 = _attn_ab(sink_b[j].astype(F32), sh(qa), sh(ka), sh(va),
                              sh(qb), sh(kb), sh(vb))
            o1, o2 = o1.reshape(t, 512), o2.reshape(t, 512)
            w_out = w_out_ab_h
        else:
            w_in = w_in_cd[j]
            w_in = jnp.concatenate(
                [w_in[:, :C_Q_RANK + C_KV_RANK],
                 _pad_cols(jnp.pad(w_in[:, 384:416], ((0, 0), (C_NOPE, 0))), LANES),
                 w_in[:, 416:]], axis=1).astype(BF16)
            wuq = w_uq[j].reshape(C_Q_RANK, C_HEADS, C_NOPE + C_ROPE)
            wuq = jnp.pad(wuq, ((0, 0), (0, 0), (0, LANES - C_NOPE - C_ROPE)))
            wuq = wuq.reshape(C_Q_RANK, C_HEADS * LANES).astype(BF16)
            wukv = w_ukv[j].reshape(C_KV_RANK, C_HEADS, C_NOPE + C_V)
            wuk = jnp.pad(wukv[:, :, :C_NOPE], ((0, 0), (0, 0), (0, LANES - C_NOPE)))
            wuk = wuk.reshape(C_KV_RANK, C_HEADS * LANES).astype(BF16)
            wuv = wukv[:, :, C_NOPE:].reshape(C_KV_RANK, C_HEADS * C_V).astype(BF16)
            qc, kc, vc, qd, kd, vd = _inproj_cd(
                xf, row2(g_mix[i]), w_in, row2(g_cq[j]), row2(g_ckv[j]), wuq, wuk, wuv, tabs_cd)
            sh = lambda a: a.reshape(b, s, a.shape[-1])
            o1 = _attn_c(sh(qc), sh(kc), sh(vc)).reshape(t, 512)
            o2 = _attn_d(sh(qd), sh(kd), sh(vd), bias_tabs, j).reshape(t, 512)
            w_out = w_out_cd_h

        xf = _layer_tail(xf, o1, o2, w_out, j, row2(g_xq[i]), w_xq_h, kv_all, w_xo_h,
                         row2(g_ffn[i]), w_gate_up_h, w_down_h, i, row2(g_final),
                         final_norm=(i == depth - 1))
    return xf.reshape(b, s, d)
```

```python
import functools

import numpy as np
import jax
import jax.numpy as jnp
from jax import lax
from jax.experimental import pallas as pl
from jax.experimental.pallas import tpu as pltpu

D_MODEL = 1024
SEQ = 2048
HEAD_DIM = 64
GRID_W = 64
MEM_LEN = 256
ROPE_THETA = 10000.0
EPS = 1e-6
NEG = -1e30

A_HEADS = 8
B_HEADS = 8
B_WINDOW = 128
C_HEADS = 8
C_Q_RANK = 256
C_KV_RANK = 128
C_NOPE = 64
C_ROPE = 32
C_V = 64
D_HEADS = 8
D_WIN_R = 8
D_WIN_C = 16
X_HEADS = 4
X_HEAD_DIM = 128
D_FF = 2816

LANES = 128
QUAD = 256
LOG2E = 1.4426950408889634
TM = 1024
TM_SUB = 128
CD_SUB = 256
TAIL_TM = 1024
TAIL_SUB = 512
TQ_DENSE = 512
TQ_SUB = 256
FF_CHUNK = 256
VMEM_LIMIT = 56 * 1024 * 1024

F32 = jnp.float32
BF16 = jnp.bfloat16

_NT = (((1,), (1,)), ((), ()))


def _layer_spec(w, layer, rows=None, row_block=0):
    k = w.shape[1] if rows is None else rows
    return pl.BlockSpec((None, k, w.shape[2]), lambda *_: (layer, row_block, 0),
                        pipeline_mode=pl.Buffered(1))


def _dot(a, b):
    return jnp.dot(a, b, preferred_element_type=F32)


def _dot_nt(a, b):
    return lax.dot_general(a, b, _NT, preferred_element_type=F32)


def _rms_rows(xf, g):
    return xf * lax.rsqrt(jnp.mean(xf * xf, axis=-1, keepdims=True) + EPS) * g


def _lane_iota(shape):
    return lax.broadcasted_iota(jnp.int32, shape, len(shape) - 1)


def _rope_chunk(x, cos, sin_signed, first_mask, half):
    rot = jnp.where(first_mask, pltpu.roll(x, LANES - half, 1), pltpu.roll(x, half, 1))
    return x * cos + rot * sin_signed


def _half_masks(dtype):
    lane = _lane_iota((1, LANES))
    return (lane < 64).astype(dtype), (lane >= 64).astype(dtype)


def _quarter_masks(dtype):
    lane = _lane_iota((1, QUAD))
    return [((lane >= HEAD_DIM * i) & (lane < HEAD_DIM * (i + 1))).astype(dtype) for i in range(4)]


def _softmax2_parts(s):
    m = jnp.max(s, axis=-1, keepdims=True)
    p = jnp.exp2(s - m)
    return p.astype(BF16), jnp.sum(p, axis=-1, keepdims=True), m


def _add_head(o, p, l, w):
    term = _dot(p, w) * (1.0 / l)
    return term if o is None else o + term


def _inproj_ab_kernel(x_ref, g_ref, w_ref, gq_ref, gk_ref, bd_ref, tab_ref,
                      qa_ref, ka_ref, va_ref, qb_ref, kb_ref, vb_ref):
    lane = _lane_iota((TM_SUB, LANES))
    first = (lane & 63) < 32
    lo = lane < 64
    bd = bd_ref[...]

    def head_norm(zc, gain):
        ss = _dot((zc * zc).astype(BF16), bd)
        return zc * lax.rsqrt(ss * (1.0 / HEAD_DIM) + EPS) * gain

    def dup(c):
        r = pltpu.roll(c, 64, 1)
        return jnp.where(lo, c, r), jnp.where(lo, r, c)

    for sub in range(TM // TM_SUB):
        rows = slice(sub * TM_SUB, (sub + 1) * TM_SUB)
        h = _rms_rows(x_ref[rows, :], g_ref[...]).astype(BF16)
        z = _dot(h, w_ref[...])

        def chunk(i, z=z):
            return z[:, i * LANES:(i + 1) * LANES]

        def tab(i, rows=rows):
            return tab_ref[i, rows, :]

        for c in range(4):
            n = head_norm(chunk(c), gq_ref[...])
            qa_ref[rows, c * LANES:(c + 1) * LANES] = _rope_chunk(
                n, tab(0), tab(1), first, 32).astype(BF16)
        k = _rope_chunk(head_norm(chunk(4), gk_ref[...]), tab(2), tab(3), first, 32)
        k0, k1 = dup(k)
        ka_ref[rows, 0:LANES] = k0.astype(BF16)
        ka_ref[rows, LANES:2 * LANES] = k1.astype(BF16)
        v0, v1 = dup(chunk(5))
        va_ref[rows, 0:LANES] = v0.astype(BF16)
        va_ref[rows, LANES:2 * LANES] = v1.astype(BF16)
        for c in range(4):
            qb_ref[rows, c * LANES:(c + 1) * LANES] = _rope_chunk(
                chunk(6 + c), tab(4), tab(5), first, 32).astype(BF16)
        k = _rope_chunk(chunk(10), tab(6), tab(7), first, 32)
        k0, k1 = dup(k)
        kb_ref[rows, 0:LANES] = k0.astype(BF16)
        kb_ref[rows, LANES:2 * LANES] = k1.astype(BF16)
        v0, v1 = dup(chunk(11))
        vb_ref[rows, 0:LANES] = v0.astype(BF16)
        vb_ref[rows, LANES:2 * LANES] = v1.astype(BF16)


def _inproj_ab(x, g, w, layer, gq, gk, bd, tabs):
    t = x.shape[0]
    n_pos = SEQ // TM
    row = lambda i: (i, 0)
    const = lambda i: (0, 0)
    outs = [(t, 512), (t, 256), (t, 256), (t, 512), (t, 256), (t, 256)]
    return pl.pallas_call(
        _inproj_ab_kernel,
        grid=(t // TM,),
        in_specs=[
            pl.BlockSpec((TM, D_MODEL), row),
            pl.BlockSpec((1, D_MODEL), const),
            _layer_spec(w, layer),
            pl.BlockSpec((1, LANES), const),
            pl.BlockSpec((1, LANES), const),
            pl.BlockSpec((LANES, LANES), const),
            pl.BlockSpec((8, TM, LANES), lambda i: (0, i % n_pos, 0)),
        ],
        out_specs=[pl.BlockSpec((TM, n), row) for _, n in outs],
        out_shape=[jax.ShapeDtypeStruct(s, BF16) for s in outs],
        compiler_params=pltpu.CompilerParams(
            dimension_semantics=("arbitrary",), vmem_limit_bytes=VMEM_LIMIT),
        name="inproj_ab",
    )(x, g, w, gq, gk, bd, tabs)


def _place_shared_v(v2, masks):
    zero = jnp.zeros_like(v2)
    lo, hi = v2 * masks[0], v2 * masks[1]
    return [jnp.concatenate(parts, axis=1)
            for parts in ((lo, zero), (hi, zero), (zero, lo), (zero, hi))]


def _attn_a_units(q_ref, k_ref, o_ref, w_ref, masks):
    def unit(kvh, sub):
        rows = slice(sub * TQ_SUB, (sub + 1) * TQ_SUB)
        k2 = k_ref[0, :, kvh * LANES:(kvh + 1) * LANES]
        heads = []
        for i in range(4):
            c = 2 * kvh + i // 2
            qc = q_ref[0, rows, c * LANES:(c + 1) * LANES]
            p, l, _ = _softmax2_parts(_dot_nt(qc * masks[i % 2], k2))
            heads.append((p, l, w_ref[kvh, i]))
        o = None
        for head in heads:
            o = _add_head(o, *head)
        o_ref[0, rows, kvh * QUAD:(kvh + 1) * QUAD] = o.astype(BF16)

    return [functools.partial(unit, kvh, sub)
            for kvh in range(2) for sub in range(TQ_DENSE // TQ_SUB)]


B_BLOCK = 256
B_KEYS = B_BLOCK + 2 * B_WINDOW


def _attn_b_units(sink_ref, q_ref, k_ref, v_ref, o_ref, masks):
    def unit(sub, kvh):
        n = pl.program_id(1) * (TQ_DENSE // B_BLOCK) + sub
        rows = slice(sub * B_BLOCK, (sub + 1) * B_BLOCK)
        start = pl.multiple_of(jnp.clip(n * B_BLOCK - B_WINDOW, 0, SEQ - B_KEYS), B_WINDOW)
        q_pos = n * B_BLOCK + lax.broadcasted_iota(jnp.int32, (B_BLOCK, B_KEYS), 0)
        k_pos = start + lax.broadcasted_iota(jnp.int32, (B_BLOCK, B_KEYS), 1)
        delta = k_pos - q_pos
        valid = (delta <= B_WINDOW) & (delta >= -B_WINDOW)
        k2 = k_ref[0, pl.ds(start, B_KEYS), kvh * LANES:(kvh + 1) * LANES]
        ws = _place_shared_v(v_ref[0, pl.ds(start, B_KEYS), kvh * LANES:(kvh + 1) * LANES], masks)
        heads = []
        for i in range(4):
            c = 2 * kvh + i // 2
            qc = q_ref[0, rows, c * LANES:(c + 1) * LANES]
            sink = sink_ref[4 * kvh + i] * LOG2E
            s = jnp.where(valid, _dot_nt(qc * masks[i % 2], k2), NEG)
            m = jnp.maximum(jnp.max(s, axis=-1, keepdims=True), sink)
            p = jnp.exp2(s - m)
            l = jnp.sum(p, axis=-1, keepdims=True) + jnp.exp2(sink - m)
            heads.append((p.astype(BF16), l, ws[i]))
        o = None
        for head in heads:
            o = _add_head(o, *head)
        o_ref[0, rows, kvh * QUAD:(kvh + 1) * QUAD] = o.astype(BF16)

    return [functools.partial(unit, sub, kvh)
            for sub in range(TQ_DENSE // B_BLOCK) for kvh in range(2)]


def _attn_ab_kernel(sink_ref, qa_ref, ka_ref, va_ref, qb_ref, kb_ref, vb_ref,
                    oa_ref, ob_ref, w_ref):
    masks = _half_masks(BF16)

    @pl.when(pl.program_id(1) == 0)
    def _():
        for kvh in range(2):
            placed = _place_shared_v(va_ref[0, :, kvh * LANES:(kvh + 1) * LANES], masks)
            for i in range(4):
                w_ref[kvh, i] = placed[i]

    a_units = _attn_a_units(qa_ref, ka_ref, oa_ref, w_ref, masks)
    b_units = _attn_b_units(sink_ref, qb_ref, kb_ref, vb_ref, ob_ref, masks)
    for a_unit, b_unit in zip(a_units, b_units):
        a_unit()
        b_unit()


def _attn_ab(sink, qa, ka, va, qb, kb, vb):
    b = qa.shape[0]
    tile = lambda i, j: (i, j, 0)
    whole = lambda i, j: (i, 0, 0)
    out = jax.ShapeDtypeStruct((b, SEQ, 512), BF16)
    return pl.pallas_call(
        _attn_ab_kernel,
        grid=(b, SEQ // TQ_DENSE),
        in_specs=[
            pl.BlockSpec(memory_space=pltpu.SMEM),
            pl.BlockSpec((1, TQ_DENSE, 512), tile),
            pl.BlockSpec((1, SEQ, 256), whole),
            pl.BlockSpec((1, SEQ, 256), whole),
            pl.BlockSpec((1, TQ_DENSE, 512), tile),
            pl.BlockSpec((1, SEQ, 256), whole),
            pl.BlockSpec((1, SEQ, 256), whole),
        ],
        out_specs=[pl.BlockSpec((1, TQ_DENSE, 512), tile)] * 2,
        out_shape=[out, out],
        scratch_shapes=[pltpu.VMEM((2, 4, SEQ, QUAD), BF16)],
        compiler_params=pltpu.CompilerParams(
            dimension_semantics=("arbitrary", "arbitrary"), vmem_limit_bytes=VMEM_LIMIT),
        name="attn_ab",
    )(sink, qa, ka, va, qb, kb, vb)


def _inproj_cd_kernel(x_ref, g_ref, w_ref, gcq_ref, gckv_ref, wuq_ref, wuk_ref, wuv_ref,
                      tab_ref, qc_ref, kc_ref, vc_ref, qd_ref, kd_ref, vd_ref):
    lane = _lane_iota((CD_SUB, LANES))
    first = lane < 80
    for sub in range(TM // CD_SUB):
        rows = slice(sub * CD_SUB, (sub + 1) * CD_SUB)
        h = _rms_rows(x_ref[rows, :], g_ref[...]).astype(BF16)
        z = _dot(h, w_ref[...])
        cq = _rms_rows(z[:, 0:256], gcq_ref[...]).astype(BF16)
        ckv = _rms_rows(z[:, 256:384], gckv_ref[...]).astype(BF16)
        q = _dot(cq, wuq_ref[...])
        kn = _dot(ckv, wuk_ref[...])
        tabs = [tab_ref[i, rows, :] for i in range(4)]
        kr = _rope_chunk(z[:, 384:512], tabs[2], tabs[3], first, 16)
        for hd in range(C_HEADS):
            sl = slice(hd * LANES, (hd + 1) * LANES)
            qc_ref[rows, sl] = _rope_chunk(q[:, sl], tabs[0], tabs[1], first, 16).astype(BF16)
            kc_ref[rows, sl] = (kn[:, sl] + kr).astype(BF16)
        vc_ref[rows, :] = _dot(ckv, wuv_ref[...]).astype(BF16)
        qd_ref[rows, :] = (z[:, 512:1024] * (HEAD_DIM ** -0.5 * LOG2E)).astype(BF16)
        kd_ref[rows, :] = z[:, 1024:1536].astype(BF16)
        vd_ref[rows, :] = z[:, 1536:2048].astype(BF16)


def _inproj_cd(x, g, w, gcq, gckv, wuq, wuk, wuv, tabs):
    t = x.shape[0]
    n_pos = SEQ // TM
    row = lambda i: (i, 0)
    const = lambda i: (0, 0)
    outs = [(t, 1024), (t, 1024), (t, 512), (t, 512), (t, 512), (t, 512)]
    return pl.pallas_call(
        _inproj_cd_kernel,
        grid=(t // TM,),
        in_specs=[
            pl.BlockSpec((TM, D_MODEL), row),
            pl.BlockSpec((1, D_MODEL), const),
            pl.BlockSpec(w.shape, const),
            pl.BlockSpec((1, C_Q_RANK), const),
            pl.BlockSpec((1, C_KV_RANK), const),
            pl.BlockSpec(wuq.shape, const),
            pl.BlockSpec(wuk.shape, const),
            pl.BlockSpec(wuv.shape, const),
            pl.BlockSpec((4, TM, LANES), lambda i: (0, i % n_pos, 0)),
        ],
        out_specs=[pl.BlockSpec((TM, n), row) for _, n in outs],
        out_shape=[jax.ShapeDtypeStruct(s, BF16) for s in outs],
        compiler_params=pltpu.CompilerParams(
            dimension_semantics=("arbitrary",), vmem_limit_bytes=VMEM_LIMIT),
        name="inproj_cd",
    )(x, g, w, gcq, gckv, wuq, wuk, wuv, tabs)


def _attn_c_kernel(q_ref, k_ref, v_ref, o_ref, w_ref):
    @pl.when(pl.program_id(1) == 0)
    def _():
        masks = _quarter_masks(BF16)
        for g in range(2):
            vq = v_ref[0, :, g * QUAD:(g + 1) * QUAD]
            for i in range(4):
                w_ref[g, i] = vq * masks[i]

    for g in range(2):
        for sub in range(TQ_DENSE // TQ_SUB):
            rows = slice(sub * TQ_SUB, (sub + 1) * TQ_SUB)
            o = None
            for i in range(4):
                hd = 4 * g + i
                p, l, _ = _softmax2_parts(_dot_nt(q_ref[0, rows, hd * LANES:(hd + 1) * LANES],
                                                  k_ref[0, :, hd * LANES:(hd + 1) * LANES]))
                o = _add_head(o, p, l, w_ref[g, i])
            o_ref[0, rows, g * QUAD:(g + 1) * QUAD] = o.astype(BF16)


def _attn_c(q, k, v):
    b = q.shape[0]
    return pl.pallas_call(
        _attn_c_kernel,
        grid=(b, SEQ // TQ_DENSE),
        in_specs=[
            pl.BlockSpec((1, TQ_DENSE, 1024), lambda i, j: (i, j, 0)),
            pl.BlockSpec((1, SEQ, 1024), lambda i, j: (i, 0, 0)),
            pl.BlockSpec((1, SEQ, 512), lambda i, j: (i, 0, 0)),
        ],
        out_specs=pl.BlockSpec((1, TQ_DENSE, 512), lambda i, j: (i, j, 0)),
        out_shape=jax.ShapeDtypeStruct((b, SEQ, 512), BF16),
        scratch_shapes=[pltpu.VMEM((2, 4, SEQ, QUAD), BF16)],
        compiler_params=pltpu.CompilerParams(
            dimension_semantics=("arbitrary", "arbitrary"), vmem_limit_bytes=VMEM_LIMIT),
        name="attn_c",
    )(q, k, v)


N_ROWS = SEQ // GRID_W
N_DR = 2 * D_WIN_R - 1
N_DC = 2 * D_WIN_C - 1
D_QROWS = 4
D_WROWS = D_QROWS + D_WIN_R
D_Q = D_QROWS * GRID_W
D_KEYS = D_WROWS * GRID_W
D_BLOCKS = N_ROWS // D_QROWS
D_SUBS = 2
D_KINDS = 3


def _d_window_row(kind, a, w):
    if kind == 0:
        return w < D_WIN_R, w - a + (D_WIN_R - 1)
    if kind == 1:
        return a <= w < a + D_WIN_R, w - a + (D_WIN_R - 1) - D_WIN_R // 2
    lead = D_WROWS - D_WIN_R
    return w >= lead, w - a - 1


def _bias_table_kernel(rpb_ref, tab_ref):
    h = pl.program_id(0)
    shape = (GRID_W, LANES)
    qc = lax.broadcasted_iota(jnp.int32, shape, 0)
    lane = lax.broadcasted_iota(jnp.int32, shape, 1)
    kc = lane & (GRID_W - 1)
    lower = lane < GRID_W
    idx = jnp.clip(kc - qc + (D_WIN_C - 1), 0, N_DC - 1)
    c0 = jnp.clip(qc - D_WIN_C // 2, 0, GRID_W - D_WIN_C)
    col_ok = (kc >= c0) & (kc < c0 + D_WIN_C)
    base = h * (N_DR * N_DC)
    tiles = []
    for d in range(N_DR):
        acc = jnp.zeros(shape, F32)
        for j in range(N_DC):
            acc = jnp.where(idx == j, rpb_ref[base + d * N_DC + j], acc)
        tiles.append(jnp.where(col_ok, acc * LOG2E, NEG))
    neg = jnp.full(shape, NEG, F32)
    for kind in range(D_KINDS):
        for a in range(D_QROWS):
            for wp in range(D_WROWS // 2):
                ok_lo, d_lo = _d_window_row(kind, a, 2 * wp)
                ok_hi, d_hi = _d_window_row(kind, a, 2 * wp + 1)
                t_lo = tiles[d_lo] if ok_lo else neg
                t_hi = tiles[d_hi] if ok_hi else neg
                tab_ref[0, kind, a * GRID_W:(a + 1) * GRID_W, wp * LANES:(wp + 1) * LANES] = (
                    jnp.where(lower, t_lo, t_hi))


def _bias_table(rpb_flat, n_layers):
    return pl.pallas_call(
        _bias_table_kernel,
        grid=(n_layers * D_HEADS,),
        in_specs=[pl.BlockSpec(memory_space=pltpu.SMEM)],
        out_specs=pl.BlockSpec((1, D_KINDS, D_Q, D_KEYS), lambda i: (i, 0, 0, 0)),
        out_shape=jax.ShapeDtypeStruct((n_layers * D_HEADS, D_KINDS, D_Q, D_KEYS), F32),
        compiler_params=pltpu.CompilerParams(dimension_semantics=("arbitrary",)),
        name="bias_table",
    )(rpb_flat)


def _attn_d_kernel(q_ref, k_ref, v_ref, tab0_ref, tab1_ref, o_ref):
    halves = _half_masks(BF16)
    quarters = _quarter_masks(BF16)

    for sub, tab_ref in enumerate((tab0_ref, tab1_ref)):
        blk = pl.program_id(1) * D_SUBS + sub
        rows = slice(sub * D_Q, (sub + 1) * D_Q)
        w0 = jnp.clip(blk * D_QROWS - D_WIN_R // 2, 0, N_ROWS - D_WROWS)
        start = pl.multiple_of(w0 * GRID_W, D_Q)
        for g in range(2):
            vq = v_ref[0, pl.ds(start, D_KEYS), g * QUAD:(g + 1) * QUAD]
            o = None
            for i in range(4):
                hd = 4 * g + i
                c = hd // 2
                qh = q_ref[0, rows, c * LANES:(c + 1) * LANES] * halves[i % 2]
                kwin = k_ref[0, pl.ds(start, D_KEYS), c * LANES:(c + 1) * LANES]
                p, l, _ = _softmax2_parts(_dot_nt(qh, kwin) + tab_ref[hd, 0])
                o = _add_head(o, p, l, vq * quarters[i])
            o_ref[0, rows, g * QUAD:(g + 1) * QUAD] = o.astype(BF16)


def _attn_d(q, k, v, tab, layer):
    b = q.shape[0]

    def kind(sub):
        def index(i, j):
            blk = j * D_SUBS + sub
            return (layer, jnp.where(blk == 0, 0, jnp.where(blk == D_BLOCKS - 1, 2, 1)), 0, 0)
        return index

    return pl.pallas_call(
        _attn_d_kernel,
        grid=(b, D_BLOCKS // D_SUBS),
        in_specs=[
            pl.BlockSpec((1, D_SUBS * D_Q, 512), lambda i, j: (i, j, 0)),
            pl.BlockSpec((1, SEQ, 512), lambda i, j: (i, 0, 0)),
            pl.BlockSpec((1, SEQ, 512), lambda i, j: (i, 0, 0)),
            pl.BlockSpec((D_HEADS, 1, D_Q, D_KEYS), kind(0)),
            pl.BlockSpec((D_HEADS, 1, D_Q, D_KEYS), kind(1)),
        ],
        out_specs=pl.BlockSpec((1, D_SUBS * D_Q, 512), lambda i, j: (i, j, 0)),
        out_shape=jax.ShapeDtypeStruct((b, SEQ, 512), BF16),
        compiler_params=pltpu.CompilerParams(
            dimension_semantics=("arbitrary", "arbitrary"), vmem_limit_bytes=VMEM_LIMIT),
        name="attn_d",
    )(q, k, v, tab, tab)


def _norm_proj_kernel(x_ref, g_ref, w_ref, o_ref):
    h = _rms_rows(x_ref[...], g_ref[...]).astype(BF16)
    o_ref[...] = _dot(h, w_ref[...]).astype(BF16)


def _norm_proj(x, g, w):
    t = x.shape[0]
    depth, _, n = w.shape
    return pl.pallas_call(
        _norm_proj_kernel,
        grid=(depth, t // TM),
        in_specs=[
            pl.BlockSpec((TM, D_MODEL), lambda l, i: (i, 0)),
            pl.BlockSpec((None, 1, D_MODEL), lambda l, i: (l, 0, 0)),
            pl.BlockSpec((None, D_MODEL, n), lambda l, i: (l, 0, 0)),
        ],
        out_specs=pl.BlockSpec((None, TM, n), lambda l, i: (l, i, 0)),
        out_shape=jax.ShapeDtypeStruct((depth, t, n), BF16),
        compiler_params=pltpu.CompilerParams(
            dimension_semantics=("arbitrary", "arbitrary"), vmem_limit_bytes=VMEM_LIMIT),
        name="mem_kv_proj",
    )(x, g, w)


def _cross_block(x, a1, a2, wo1_ref, wo2_ref, g_ref, wq_ref, kv_ref, wxo_ref):
    x1 = x + _dot(a1, wo1_ref[...]) + _dot(a2, wo2_ref[...])
    h = _rms_rows(x1, g_ref[...]).astype(BF16)
    q = (_dot(h, wq_ref[...]) * (X_HEAD_DIM ** -0.5 * LOG2E)).astype(BF16)
    heads = []
    for hd in range(X_HEADS):
        sl = slice(hd * LANES, (hd + 1) * LANES)
        k = kv_ref[0, :, sl]
        v = kv_ref[0, :, X_HEADS * LANES + hd * LANES:X_HEADS * LANES + (hd + 1) * LANES]
        p, l, _ = _softmax2_parts(_dot_nt(q[:, sl], k))
        heads.append((_dot(p, v) * (1.0 / l)).astype(BF16))
    return x1 + _dot(jnp.concatenate(heads, axis=1), wxo_ref[...])


def _ffn_block(x, g_ref, wgu_ref, wd_ref):
    h = _rms_rows(x, g_ref[...]).astype(BF16)
    acc = x
    for c in range(D_FF // FF_CHUNK):
        lo = c * FF_CHUNK
        gate = _dot(h, wgu_ref[:, lo:lo + FF_CHUNK])
        up = _dot(h, wgu_ref[:, D_FF + lo:D_FF + lo + FF_CHUNK])
        act = (gate * jax.nn.sigmoid(gate) * up).astype(BF16)
        acc = acc + _dot(act, wd_ref[lo:lo + FF_CHUNK, :])
    return acc


def _tail_kernel(x_ref, a1_ref, a2_ref, wo1_ref, wo2_ref, gx_ref, wq_ref, kv_ref, wxo_ref,
                 gf_ref, wgu_ref, wd_ref, gfin_ref, o_ref, *, final_norm):
    for sub in range(TAIL_TM // TAIL_SUB):
        rows = slice(sub * TAIL_SUB, (sub + 1) * TAIL_SUB)
        x2 = _cross_block(x_ref[rows, :], a1_ref[rows, :], a2_ref[rows, :],
                          wo1_ref, wo2_ref, gx_ref, wq_ref, kv_ref, wxo_ref)
        x3 = _ffn_block(x2, gf_ref, wgu_ref, wd_ref)
        if final_norm:
            x3 = _rms_rows(x3, gfin_ref[...])
        o_ref[rows, :] = x3


def _layer_tail(x, a1, a2, w_out, mix_layer, gx, wq, kv, wxo, gf, wgu, wd, layer, gfin,
                final_norm):
    t = x.shape[0]
    per_b = SEQ // TAIL_TM
    row = lambda i: (i, 0)
    const = lambda i: (0, 0)
    return pl.pallas_call(
        functools.partial(_tail_kernel, final_norm=final_norm),
        grid=(t // TAIL_TM,),
        in_specs=[
            pl.BlockSpec((TAIL_TM, D_MODEL), row),
            pl.BlockSpec((TAIL_TM, 512), row),
            pl.BlockSpec((TAIL_TM, 512), row),
            _layer_spec(w_out, mix_layer, rows=512, row_block=0),
            _layer_spec(w_out, mix_layer, rows=512, row_block=1),
            pl.BlockSpec((1, D_MODEL), const),
            _layer_spec(wq, layer),
            pl.BlockSpec((None, 1, MEM_LEN, 2 * X_HEADS * X_HEAD_DIM),
                         lambda i: (layer, i // per_b, 0, 0)),
            _layer_spec(wxo, layer),
            pl.BlockSpec((1, D_MODEL), const),
            _layer_spec(wgu, layer),
            _layer_spec(wd, layer),
            pl.BlockSpec((1, D_MODEL), const),
        ],
        out_specs=pl.BlockSpec((TAIL_TM, D_MODEL), row),
        out_shape=jax.ShapeDtypeStruct((t, D_MODEL), F32),
        compiler_params=pltpu.CompilerParams(
            dimension_semantics=("arbitrary",), vmem_limit_bytes=VMEM_LIMIT),
        name="tail_final" if final_norm else "tail",
    )(x, a1, a2, w_out, w_out, gx, wq, kv, wxo, gf, wgu, wd, gfin)


def _rope_angles(pos, dim):
    inv = ROPE_THETA ** (-jnp.arange(0, dim, 2, dtype=F32) / dim)
    return pos.astype(F32)[:, None] * inv[None, :]


def _head64_tables(ang, scale):
    d = np.arange(LANES) % HEAD_DIM
    sign = np.where(d < HEAD_DIM // 2, -1.0, 1.0).astype(np.float32)
    a = ang[:, d % (HEAD_DIM // 2)]
    return jnp.cos(a) * scale, jnp.sin(a) * sign[None, :] * scale


def _latent_tables(ang, scale):
    lane = np.arange(LANES)
    in_rope = (lane >= C_NOPE) & (lane < C_NOPE + C_ROPE)
    a = ang[:, (lane - C_NOPE) % (C_ROPE // 2)]
    sign = np.where(lane < C_NOPE + C_ROPE // 2, -1.0, 1.0).astype(np.float32)
    cos = jnp.where(in_rope[None, :], jnp.cos(a), 1.0) * scale
    sin = jnp.where(in_rope[None, :], jnp.sin(a) * sign[None, :], 0.0) * scale
    return cos, sin


def _pad_cols(w, width):
    return jnp.pad(w, ((0, 0), (0, width - w.shape[1])))


def kernel(x, mem, g_mix, w_in_ab, g_qa, g_ka, sink_b, w_out_ab, w_in_cd, g_cq, g_ckv,
           w_uq, w_ukv, rpb_d, w_out_cd, g_xq, g_mem, w_xq, w_xkv, w_xo, g_ffn,
           w_gate_up, w_down, g_final):
    b, s, d = x.shape
    assert (s, d) == (SEQ, D_MODEL) and mem.shape == (b, MEM_LEN, D_MODEL)
    depth = g_mix.shape[0]
    t = b * s

    pos = jnp.arange(s)
    ang_1d = _rope_angles(pos, HEAD_DIM)
    ang_2d = jnp.concatenate([_rope_angles(pos // GRID_W, HEAD_DIM // 2),
                              _rope_angles(pos % GRID_W, HEAD_DIM // 2)], axis=-1)
    ang_c = _rope_angles(pos, C_ROPE)
    q_scale = HEAD_DIM ** -0.5 * LOG2E
    tabs_ab = jnp.stack([*_head64_tables(ang_2d, q_scale), *_head64_tables(ang_2d, 1.0),
                         *_head64_tables(ang_1d, q_scale), *_head64_tables(ang_1d, 1.0)])
    tabs_cd = jnp.stack([*_latent_tables(ang_c, (C_NOPE + C_ROPE) ** -0.5 * LOG2E),
                         *_latent_tables(ang_c, 1.0)])
    lane = np.arange(LANES)
    block_ones = jnp.asarray(lane[:, None] // HEAD_DIM == lane[None, :] // HEAD_DIM, BF16)

    row2 = lambda v: v.reshape(1, -1).astype(F32)
    xf = x.reshape(t, d)
    mem_f = mem.reshape(b * MEM_LEN, d)
    w_in_ab_h, w_out_ab_h, w_out_cd_h = (w.astype(BF16) for w in (w_in_ab, w_out_ab, w_out_cd))
    w_xq_h, w_xkv_h, w_xo_h = (w.astype(BF16) for w in (w_xq, w_xkv, w_xo))
    w_gate_up_h, w_down_h = w_gate_up.astype(BF16), w_down.astype(BF16)
    kv_all = _norm_proj(mem_f, g_mem.reshape(depth, 1, d).astype(F32), w_xkv_h)
    kv_all = kv_all.reshape(depth, b, MEM_LEN, 2 * X_HEADS * X_HEAD_DIM)
    bias_tabs = _bias_table(rpb_d.reshape(-1).astype(F32), rpb_d.shape[0])

    for i in range(depth):
        j = i // 2
        if i % 2 == 0:
            gq = row2(jnp.tile(g_qa[j], 2))
            gk = row2(jnp.tile(g_ka[j], 2))
            qa, ka, va, qb, kb, vb = _inproj_ab(
                xf, row2(g_mix[i]), w_in_ab_h, j, gq, gk, block_ones, tabs_ab)
            sh = lambda a: a.reshape(b, s, a.shape[-1])
            o1, o2 = _attn_ab(sink_b[j].astype(F32), sh(qa), sh(ka), sh(va),
                              sh(qb), sh(kb), sh(vb))
            o1, o2 = o1.reshape(t, 512), o2.reshape(t, 512)
            w_out = w_out_ab_h
        else:
            w_in = w_in_cd[j]
            w_in = jnp.concatenate(
                [w_in[:, :C_Q_RANK + C_KV_RANK],
                 _pad_cols(jnp.pad(w_in[:, 384:416], ((0, 0), (C_NOPE, 0))), LANES),
                 w_in[:, 416:]], axis=1).astype(BF16)
            wuq = w_uq[j].reshape(C_Q_RANK, C_HEADS, C_NOPE + C_ROPE)
            wuq = jnp.pad(wuq, ((0, 0), (0, 0), (0, LANES - C_NOPE - C_ROPE)))
            wuq = wuq.reshape(C_Q_RANK, C_HEADS * LANES).astype(BF16)
            wukv = w_ukv[j].reshape(C_KV_RANK, C_HEADS, C_NOPE + C_V)
            wuk = jnp.pad(wukv[:, :, :C_NOPE], ((0, 0), (0, 0), (0, LANES - C_NOPE)))
            wuk = wuk.reshape(C_KV_RANK, C_HEADS * LANES).astype(BF16)
            wuv = wukv[:, :, C_NOPE:].reshape(C_KV_RANK, C_HEADS * C_V).astype(BF16)
            qc, kc, vc, qd, kd, vd = _inproj_cd(
                xf, row2(g_mix[i]), w_in, row2(g_cq[j]), row2(g_ckv[j]), wuq, wuk, wuv, tabs_cd)
            sh = lambda a: a.reshape(b, s, a.shape[-1])
            o1 = _attn_c(sh(qc), sh(kc), sh(vc)).reshape(t, 512)
            o2 = _attn_d(sh(qd), sh(kd), sh(vd), bias_tabs, j).reshape(t, 512)
            w_out = w_out_cd_h

        xf = _layer_tail(xf, o1, o2, w_out, j, row2(g_xq[i]), w_xq_h, kv_all, w_xo_h,
                         row2(g_ffn[i]), w_gate_up_h, w_down_h, i, row2(g_final),
                         final_norm=(i == depth - 1))
    return xf.reshape(b, s, d)
```

```python
import functools

import numpy as np
import jax
import jax.numpy as jnp
from jax import lax
from jax.experimental import pallas as pl
from jax.experimental.pallas import tpu as pltpu

D_MODEL = 1024
SEQ = 2048
HEAD_DIM = 64
GRID_W = 64
MEM_LEN = 256
ROPE_THETA = 10000.0
EPS = 1e-6
NEG = -1e30

A_HEADS = 8
B_HEADS = 8
B_WINDOW = 128
C_HEADS = 8
C_Q_RANK = 256
C_KV_RANK = 128
C_NOPE = 64
C_ROPE = 32
C_V = 64
D_HEADS = 8
D_WIN_R = 8
D_WIN_C = 16
X_HEADS = 4
X_HEAD_DIM = 128
D_FF = 2816

LANES = 128
QUAD = 256
LOG2E = 1.4426950408889634
TM = 1024
TM_SUB = 128
CD_SUB = 256
TAIL_TM = 1024
TAIL_SUB = 512
TQ_DENSE = 512
TQ_SUB = 256
FF_CHUNK = 256
VMEM_LIMIT = 56 * 1024 * 1024

F32 = jnp.float32
BF16 = jnp.bfloat16

_NT = (((1,), (1,)), ((), ()))


def _layer_spec(w, layer, rows=None, row_block=0):
    k = w.shape[1] if rows is None else rows
    return pl.BlockSpec((None, k, w.shape[2]), lambda *_: (layer, row_block, 0),
                        pipeline_mode=pl.Buffered(1))


def _dot(a, b):
    return jnp.dot(a, b, preferred_element_type=F32)


def _dot_nt(a, b):
    return lax.dot_general(a, b, _NT, preferred_element_type=F32)


def _rms_rows(xf, g):
    return xf * lax.rsqrt(jnp.mean(xf * xf, axis=-1, keepdims=True) + EPS) * g


def _lane_iota(shape):
    return lax.broadcasted_iota(jnp.int32, shape, len(shape) - 1)


def _rope_chunk(x, cos, sin_signed, first_mask, half):
    rot = jnp.where(first_mask, pltpu.roll(x, LANES - half, 1), pltpu.roll(x, half, 1))
    return x * cos + rot * sin_signed


def _half_masks(dtype):
    lane = _lane_iota((1, LANES))
    return (lane < 64).astype(dtype), (lane >= 64).astype(dtype)


def _quarter_masks(dtype):
    lane = _lane_iota((1, QUAD))
    return [((lane >= HEAD_DIM * i) & (lane < HEAD_DIM * (i + 1))).astype(dtype) for i in range(4)]


def _sum_col(i):
    return HEAD_DIM * ((i + 1) % 4)


def _place_head(v_masked, i):
    lane = _lane_iota(v_masked.shape)
    return jnp.where(lane == _sum_col(i), 1.0, v_masked).astype(BF16)


def _softmax2_parts(s):
    m = jnp.max(s, axis=-1, keepdims=True)
    p = jnp.exp2(s - m)
    return p.astype(BF16), jnp.sum(p, axis=-1, keepdims=True), m


def _add_head(o, p, w, i, keep, l_extra=None):
    raw = _dot(p, w)
    l = raw[:, _sum_col(i):_sum_col(i) + 1]
    if l_extra is not None:
        l = l + l_extra
    term = raw * (1.0 / l) * keep[i]
    return term if o is None else o + term


def _inproj_ab_kernel(x_ref, g_ref, w_ref, gq_ref, gk_ref, bd_ref, tab_ref,
                      qa_ref, ka_ref, va_ref, qb_ref, kb_ref, vb_ref):
    lane = _lane_iota((TM_SUB, LANES))
    first = (lane & 63) < 32
    lo = lane < 64
    bd = bd_ref[...]

    def head_norm(zc, gain):
        ss = _dot((zc * zc).astype(BF16), bd)
        return zc * lax.rsqrt(ss * (1.0 / HEAD_DIM) + EPS) * gain

    def dup(c):
        r = pltpu.roll(c, 64, 1)
        return jnp.where(lo, c, r), jnp.where(lo, r, c)

    for sub in range(TM // TM_SUB):
        rows = slice(sub * TM_SUB, (sub + 1) * TM_SUB)
        h = _rms_rows(x_ref[rows, :], g_ref[...]).astype(BF16)
        z = _dot(h, w_ref[...])

        def chunk(i, z=z):
            return z[:, i * LANES:(i + 1) * LANES]

        def tab(i, rows=rows):
            return tab_ref[i, rows, :]

        for c in range(4):
            n = head_norm(chunk(c), gq_ref[...])
            qa_ref[rows, c * LANES:(c + 1) * LANES] = _rope_chunk(
                n, tab(0), tab(1), first, 32).astype(BF16)
        k = _rope_chunk(head_norm(chunk(4), gk_ref[...]), tab(2), tab(3), first, 32)
        k0, k1 = dup(k)
        ka_ref[rows, 0:LANES] = k0.astype(BF16)
        ka_ref[rows, LANES:2 * LANES] = k1.astype(BF16)
        v0, v1 = dup(chunk(5))
        va_ref[rows, 0:LANES] = v0.astype(BF16)
        va_ref[rows, LANES:2 * LANES] = v1.astype(BF16)
        for c in range(4):
            qb_ref[rows, c * LANES:(c + 1) * LANES] = _rope_chunk(
                chunk(6 + c), tab(4), tab(5), first, 32).astype(BF16)
        k = _rope_chunk(chunk(10), tab(6), tab(7), first, 32)
        k0, k1 = dup(k)
        kb_ref[rows, 0:LANES] = k0.astype(BF16)
        kb_ref[rows, LANES:2 * LANES] = k1.astype(BF16)
        v0, v1 = dup(chunk(11))
        vb_ref[rows, 0:LANES] = v0.astype(BF16)
        vb_ref[rows, LANES:2 * LANES] = v1.astype(BF16)


def _inproj_ab(x, g, w, layer, gq, gk, bd, tabs):
    t = x.shape[0]
    n_pos = SEQ // TM
    row = lambda i: (i, 0)
    const = lambda i: (0, 0)
    outs = [(t, 512), (t, 256), (t, 256), (t, 512), (t, 256), (t, 256)]
    return pl.pallas_call(
        _inproj_ab_kernel,
        grid=(t // TM,),
        in_specs=[
            pl.BlockSpec((TM, D_MODEL), row),
            pl.BlockSpec((1, D_MODEL), const),
            _layer_spec(w, layer),
            pl.BlockSpec((1, LANES), const),
            pl.BlockSpec((1, LANES), const),
            pl.BlockSpec((LANES, LANES), const),
            pl.BlockSpec((8, TM, LANES), lambda i: (0, i % n_pos, 0)),
        ],
        out_specs=[pl.BlockSpec((TM, n), row) for _, n in outs],
        out_shape=[jax.ShapeDtypeStruct(s, BF16) for s in outs],
        compiler_params=pltpu.CompilerParams(
            dimension_semantics=("arbitrary",), vmem_limit_bytes=VMEM_LIMIT),
        name="inproj_ab",
    )(x, g, w, gq, gk, bd, tabs)


def _place_shared_v(v2, masks):
    zero = jnp.zeros_like(v2)
    lo, hi = v2 * masks[0], v2 * masks[1]
    return [_place_head(jnp.concatenate(parts, axis=1), i)
            for i, parts in enumerate(((lo, zero), (hi, zero), (zero, lo), (zero, hi)))]


def _attn_a_units(q_ref, k_ref, o_ref, w_ref, masks):
    def unit(kvh, sub):
        rows = slice(sub * TQ_SUB, (sub + 1) * TQ_SUB)
        k2 = k_ref[0, :, kvh * LANES:(kvh + 1) * LANES]
        keep = _quarter_masks(F32)
        probs = []
        for i in range(4):
            c = 2 * kvh + i // 2
            qc = q_ref[0, rows, c * LANES:(c + 1) * LANES]
            s = _dot_nt(qc * masks[i % 2], k2)
            probs.append(jnp.exp2(s - jnp.max(s, axis=-1, keepdims=True)).astype(BF16))
        o = None
        for i, p in enumerate(probs):
            o = _add_head(o, p, w_ref[kvh, i], i, keep)
        o_ref[0, rows, kvh * QUAD:(kvh + 1) * QUAD] = o.astype(BF16)

    return [functools.partial(unit, kvh, sub)
            for kvh in range(2) for sub in range(TQ_DENSE // TQ_SUB)]


B_BLOCK = 256
B_KEYS = B_BLOCK + 2 * B_WINDOW


def _attn_b_units(sink_ref, q_ref, k_ref, v_ref, o_ref, masks):
    def unit(sub, kvh):
        n = pl.program_id(1) * (TQ_DENSE // B_BLOCK) + sub
        rows = slice(sub * B_BLOCK, (sub + 1) * B_BLOCK)
        start = pl.multiple_of(jnp.clip(n * B_BLOCK - B_WINDOW, 0, SEQ - B_KEYS), B_WINDOW)
        q_pos = n * B_BLOCK + lax.broadcasted_iota(jnp.int32, (B_BLOCK, B_KEYS), 0)
        k_pos = start + lax.broadcasted_iota(jnp.int32, (B_BLOCK, B_KEYS), 1)
        delta = k_pos - q_pos
        valid = (delta <= B_WINDOW) & (delta >= -B_WINDOW)
        k2 = k_ref[0, pl.ds(start, B_KEYS), kvh * LANES:(kvh + 1) * LANES]
        ws = _place_shared_v(v_ref[0, pl.ds(start, B_KEYS), kvh * LANES:(kvh + 1) * LANES], masks)
        keep = _quarter_masks(F32)
        heads = []
        for i in range(4):
            c = 2 * kvh + i // 2
            qc = q_ref[0, rows, c * LANES:(c + 1) * LANES]
            sink = sink_ref[4 * kvh + i] * LOG2E
            s = jnp.where(valid, _dot_nt(qc * masks[i % 2], k2), NEG)
            m = jnp.maximum(jnp.max(s, axis=-1, keepdims=True), sink)
            heads.append((jnp.exp2(s - m).astype(BF16), jnp.exp2(sink - m)))
        o = None
        for i, (p, sink_term) in enumerate(heads):
            o = _add_head(o, p, ws[i], i, keep, l_extra=sink_term)
        o_ref[0, rows, kvh * QUAD:(kvh + 1) * QUAD] = o.astype(BF16)

    return [functools.partial(unit, sub, kvh)
            for sub in range(TQ_DENSE // B_BLOCK) for kvh in range(2)]


def _attn_ab_kernel(sink_ref, qa_ref, ka_ref, va_ref, qb_ref, kb_ref, vb_ref,
                    oa_ref, ob_ref, w_ref):
    masks = _half_masks(BF16)

    @pl.when(pl.program_id(1) == 0)
    def _():
        for kvh in range(2):
            placed = _place_shared_v(va_ref[0, :, kvh * LANES:(kvh + 1) * LANES], masks)
            for i in range(4):
                w_ref[kvh, i] = placed[i]

    a_units = _attn_a_units(qa_ref, ka_ref, oa_ref, w_ref, masks)
    b_units = _attn_b_units(sink_ref, qb_ref, kb_ref, vb_ref, ob_ref, masks)
    assert len(a_units) == len(b_units)
    for a_unit, b_unit in zip(a_units, b_units):
        a_unit()
        b_unit()


def _attn_ab(sink, qa, ka, va, qb, kb, vb):
    b = qa.shape[0]
    tile = lambda i, j: (i, j, 0)
    whole = lambda i, j: (i, 0, 0)
    out = jax.ShapeDtypeStruct((b, SEQ, 512), BF16)
    return pl.pallas_call(
        _attn_ab_kernel,
        grid=(b, SEQ // TQ_DENSE),
        in_specs=[
            pl.BlockSpec(memory_space=pltpu.SMEM),
            pl.BlockSpec((1, TQ_DENSE, 512), tile),
            pl.BlockSpec((1, SEQ, 256), whole),
            pl.BlockSpec((1, SEQ, 256), whole),
            pl.BlockSpec((1, TQ_DENSE, 512), tile),
            pl.BlockSpec((1, SEQ, 256), whole),
            pl.BlockSpec((1, SEQ, 256), whole),
        ],
        out_specs=[pl.BlockSpec((1, TQ_DENSE, 512), tile)] * 2,
        out_shape=[out, out],
        scratch_shapes=[pltpu.VMEM((2, 4, SEQ, QUAD), BF16)],
        compiler_params=pltpu.CompilerParams(
            dimension_semantics=("arbitrary", "arbitrary"), vmem_limit_bytes=VMEM_LIMIT),
        name="attn_ab",
    )(sink, qa, ka, va, qb, kb, vb)


def _inproj_cd_kernel(x_ref, g_ref, w_ref, gcq_ref, gckv_ref, wuq_ref, wuk_ref, wuv_ref,
                      tab_ref, qc_ref, kc_ref, vc_ref, qd_ref, kd_ref, vd_ref):
    lane = _lane_iota((CD_SUB, LANES))
    first = lane < 80
    for sub in range(TM // CD_SUB):
        rows = slice(sub * CD_SUB, (sub + 1) * CD_SUB)
        h = _rms_rows(x_ref[rows, :], g_ref[...]).astype(BF16)
        z = _dot(h, w_ref[...])
        cq = _rms_rows(z[:, 0:256], gcq_ref[...]).astype(BF16)
        ckv = _rms_rows(z[:, 256:384], gckv_ref[...]).astype(BF16)
        q = _dot(cq, wuq_ref[...])
        kn = _dot(ckv, wuk_ref[...])
        tabs = [tab_ref[i, rows, :] for i in range(4)]
        kr = _rope_chunk(z[:, 384:512], tabs[2], tabs[3], first, 16)
        for hd in range(C_HEADS):
            sl = slice(hd * LANES, (hd + 1) * LANES)
            qc_ref[rows, sl] = _rope_chunk(q[:, sl], tabs[0], tabs[1], first, 16).astype(BF16)
            kc_ref[rows, sl] = (kn[:, sl] + kr).astype(BF16)
        vc_ref[rows, :] = _dot(ckv, wuv_ref[...]).astype(BF16)
        qd_ref[rows, :] = (z[:, 512:1024] * (HEAD_DIM ** -0.5 * LOG2E)).astype(BF16)
        kd_ref[rows, :] = z[:, 1024:1536].astype(BF16)
        vd_ref[rows, :] = z[:, 1536:2048].astype(BF16)


def _inproj_cd(x, g, w, gcq, gckv, wuq, wuk, wuv, tabs):
    t = x.shape[0]
    n_pos = SEQ // TM
    row = lambda i: (i, 0)
    const = lambda i: (0, 0)
    outs = [(t, 1024), (t, 1024), (t, 512), (t, 512), (t, 512), (t, 512)]
    return pl.pallas_call(
        _inproj_cd_kernel,
        grid=(t // TM,),
        in_specs=[
            pl.BlockSpec((TM, D_MODEL), row),
            pl.BlockSpec((1, D_MODEL), const),
            pl.BlockSpec(w.shape, const),
            pl.BlockSpec((1, C_Q_RANK), const),
            pl.BlockSpec((1, C_KV_RANK), const),
            pl.BlockSpec(wuq.shape, const),
            pl.BlockSpec(wuk.shape, const),
            pl.BlockSpec(wuv.shape, const),
            pl.BlockSpec((4, TM, LANES), lambda i: (0, i % n_pos, 0)),
        ],
        out_specs=[pl.BlockSpec((TM, n), row) for _, n in outs],
        out_shape=[jax.ShapeDtypeStruct(s, BF16) for s in outs],
        compiler_params=pltpu.CompilerParams(
            dimension_semantics=("arbitrary",), vmem_limit_bytes=VMEM_LIMIT),
        name="inproj_cd",
    )(x, g, w, gcq, gckv, wuq, wuk, wuv, tabs)


def _attn_c_kernel(q_ref, k_ref, v_ref, o_ref, w_ref):
    @pl.when(pl.program_id(1) == 0)
    def _():
        masks = _quarter_masks(BF16)
        for g in range(2):
            vq = v_ref[0, :, g * QUAD:(g + 1) * QUAD]
            for i in range(4):
                w_ref[g, i] = _place_head(vq * masks[i], i)

    keep = _quarter_masks(F32)
    for g in range(2):
        for sub in range(TQ_DENSE // TQ_SUB):
            rows = slice(sub * TQ_SUB, (sub + 1) * TQ_SUB)
            o = None
            for i in range(4):
                hd = 4 * g + i
                s = _dot_nt(q_ref[0, rows, hd * LANES:(hd + 1) * LANES],
                            k_ref[0, :, hd * LANES:(hd + 1) * LANES])
                p = jnp.exp2(s - jnp.max(s, axis=-1, keepdims=True)).astype(BF16)
                o = _add_head(o, p, w_ref[g, i], i, keep)
            o_ref[0, rows, g * QUAD:(g + 1) * QUAD] = o.astype(BF16)


def _attn_c(q, k, v):
    b = q.shape[0]
    return pl.pallas_call(
        _attn_c_kernel,
        grid=(b, SEQ // TQ_DENSE),
        in_specs=[
            pl.BlockSpec((1, TQ_DENSE, 1024), lambda i, j: (i, j, 0)),
            pl.BlockSpec((1, SEQ, 1024), lambda i, j: (i, 0, 0)),
            pl.BlockSpec((1, SEQ, 512), lambda i, j: (i, 0, 0)),
        ],
        out_specs=pl.BlockSpec((1, TQ_DENSE, 512), lambda i, j: (i, j, 0)),
        out_shape=jax.ShapeDtypeStruct((b, SEQ, 512), BF16),
        scratch_shapes=[pltpu.VMEM((2, 4, SEQ, QUAD), BF16)],
        compiler_params=pltpu.CompilerParams(
            dimension_semantics=("arbitrary", "arbitrary"), vmem_limit_bytes=VMEM_LIMIT),
        name="attn_c",
    )(q, k, v)


N_ROWS = SEQ // GRID_W
N_DR = 2 * D_WIN_R - 1
N_DC = 2 * D_WIN_C - 1
D_QROWS = 4
D_WROWS = D_QROWS + D_WIN_R
D_Q = D_QROWS * GRID_W
D_KEYS = D_WROWS * GRID_W
D_BLOCKS = N_ROWS // D_QROWS
D_SUBS = 2
D_KINDS = 3


def _d_window_row(kind, a, w):
    if kind == 0:
        return w < D_WIN_R, w - a + (D_WIN_R - 1)
    if kind == 1:
        return a <= w < a + D_WIN_R, w - a + (D_WIN_R - 1) - D_WIN_R // 2
    lead = D_WROWS - D_WIN_R
    return w >= lead, w - a - 1


def _bias_table_kernel(rpb_ref, tab_ref):
    h = pl.program_id(0)
    shape = (GRID_W, LANES)
    qc = lax.broadcasted_iota(jnp.int32, shape, 0)
    lane = lax.broadcasted_iota(jnp.int32, shape, 1)
    kc = lane & (GRID_W - 1)
    lower = lane < GRID_W
    idx = jnp.clip(kc - qc + (D_WIN_C - 1), 0, N_DC - 1)
    c0 = jnp.clip(qc - D_WIN_C // 2, 0, GRID_W - D_WIN_C)
    col_ok = (kc >= c0) & (kc < c0 + D_WIN_C)
    base = h * (N_DR * N_DC)
    tiles = []
    for d in range(N_DR):
        acc = jnp.zeros(shape, F32)
        for j in range(N_DC):
            acc = jnp.where(idx == j, rpb_ref[base + d * N_DC + j], acc)
        tiles.append(jnp.where(col_ok, acc * LOG2E, NEG))
    neg = jnp.full(shape, NEG, F32)
    for kind in range(D_KINDS):
        for a in range(D_QROWS):
            for wp in range(D_WROWS // 2):
                ok_lo, d_lo = _d_window_row(kind, a, 2 * wp)
                ok_hi, d_hi = _d_window_row(kind, a, 2 * wp + 1)
                t_lo = tiles[d_lo] if ok_lo else neg
                t_hi = tiles[d_hi] if ok_hi else neg
                tab_ref[0, kind, a * GRID_W:(a + 1) * GRID_W, wp * LANES:(wp + 1) * LANES] = (
                    jnp.where(lower, t_lo, t_hi))


def _bias_table(rpb_flat, n_layers):
    return pl.pallas_call(
        _bias_table_kernel,
        grid=(n_layers * D_HEADS,),
        in_specs=[pl.BlockSpec(memory_space=pltpu.SMEM)],
        out_specs=pl.BlockSpec((1, D_KINDS, D_Q, D_KEYS), lambda i: (i, 0, 0, 0)),
        out_shape=jax.ShapeDtypeStruct((n_layers * D_HEADS, D_KINDS, D_Q, D_KEYS), F32),
        compiler_params=pltpu.CompilerParams(dimension_semantics=("arbitrary",)),
        name="bias_table",
    )(rpb_flat)


def _attn_d_kernel(q_ref, k_ref, v_ref, tab0_ref, tab1_ref, o_ref):
    halves = _half_masks(BF16)
    quarters = _quarter_masks(BF16)
    keep = _quarter_masks(F32)

    for sub, tab_ref in enumerate((tab0_ref, tab1_ref)):
        blk = pl.program_id(1) * D_SUBS + sub
        rows = slice(sub * D_Q, (sub + 1) * D_Q)
        w0 = jnp.clip(blk * D_QROWS - D_WIN_R // 2, 0, N_ROWS - D_WROWS)
        start = pl.multiple_of(w0 * GRID_W, D_Q)
        for g in range(2):
            vq = v_ref[0, pl.ds(start, D_KEYS), g * QUAD:(g + 1) * QUAD]
            o = None
            for i in range(4):
                hd = 4 * g + i
                c = hd // 2
                qh = q_ref[0, rows, c * LANES:(c + 1) * LANES] * halves[i % 2]
                kwin = k_ref[0, pl.ds(start, D_KEYS), c * LANES:(c + 1) * LANES]
                s = _dot_nt(qh, kwin) + tab_ref[hd, 0]
                p = jnp.exp2(s - jnp.max(s, axis=-1, keepdims=True)).astype(BF16)
                o = _add_head(o, p, _place_head(vq * quarters[i], i), i, keep)
            o_ref[0, rows, g * QUAD:(g + 1) * QUAD] = o.astype(BF16)


def _attn_d(q, k, v, tab, layer):
    b = q.shape[0]

    def kind(sub):
        def index(i, j):
            blk = j * D_SUBS + sub
            return (layer, jnp.where(blk == 0, 0, jnp.where(blk == D_BLOCKS - 1, 2, 1)), 0, 0)
        return index

    return pl.pallas_call(
        _attn_d_kernel,
        grid=(b, D_BLOCKS // D_SUBS),
        in_specs=[
            pl.BlockSpec((1, D_SUBS * D_Q, 512), lambda i, j: (i, j, 0)),
            pl.BlockSpec((1, SEQ, 512), lambda i, j: (i, 0, 0)),
            pl.BlockSpec((1, SEQ, 512), lambda i, j: (i, 0, 0)),
            pl.BlockSpec((D_HEADS, 1, D_Q, D_KEYS), kind(0)),
            pl.BlockSpec((D_HEADS, 1, D_Q, D_KEYS), kind(1)),
        ],
        out_specs=pl.BlockSpec((1, D_SUBS * D_Q, 512), lambda i, j: (i, j, 0)),
        out_shape=jax.ShapeDtypeStruct((b, SEQ, 512), BF16),
        compiler_params=pltpu.CompilerParams(
            dimension_semantics=("arbitrary", "arbitrary"), vmem_limit_bytes=VMEM_LIMIT),
        name="attn_d",
    )(q, k, v, tab, tab)


def _norm_proj_kernel(x_ref, g_ref, w_ref, o_ref):
    h = _rms_rows(x_ref[...], g_ref[...]).astype(BF16)
    o_ref[...] = _dot(h, w_ref[...]).astype(BF16)


def _norm_proj(x, g, w):
    t = x.shape[0]
    depth, _, n = w.shape
    return pl.pallas_call(
        _norm_proj_kernel,
        grid=(depth, t // TM),
        in_specs=[
            pl.BlockSpec((TM, D_MODEL), lambda l, i: (i, 0)),
            pl.BlockSpec((None, 1, D_MODEL), lambda l, i: (l, 0, 0)),
            pl.BlockSpec((None, D_MODEL, n), lambda l, i: (l, 0, 0)),
        ],
        out_specs=pl.BlockSpec((None, TM, n), lambda l, i: (l, i, 0)),
        out_shape=jax.ShapeDtypeStruct((depth, t, n), BF16),
        compiler_params=pltpu.CompilerParams(
            dimension_semantics=("arbitrary", "arbitrary"), vmem_limit_bytes=VMEM_LIMIT),
        name="mem_kv_proj",
    )(x, g, w)


def _cross_block(x, a1, a2, wo1_ref, wo2_ref, g_ref, wq_ref, kv_ref, wxo_ref):
    x1 = x + _dot(a1, wo1_ref[...]) + _dot(a2, wo2_ref[...])
    h = _rms_rows(x1, g_ref[...]).astype(BF16)
    q = (_dot(h, wq_ref[...]) * (X_HEAD_DIM ** -0.5 * LOG2E)).astype(BF16)
    heads = []
    for hd in range(X_HEADS):
        sl = slice(hd * LANES, (hd + 1) * LANES)
        k = kv_ref[0, :, sl]
        v = kv_ref[0, :, X_HEADS * LANES + hd * LANES:X_HEADS * LANES + (hd + 1) * LANES]
        p, l, _ = _softmax2_parts(_dot_nt(q[:, sl], k))
        heads.append((_dot(p, v) * (1.0 / l)).astype(BF16))
    return x1 + _dot(jnp.concatenate(heads, axis=1), wxo_ref[...])


def _ffn_block(x, g_ref, wgu_ref, wd_ref):
    h = _rms_rows(x, g_ref[...]).astype(BF16)
    acc = x
    for c in range(D_FF // FF_CHUNK):
        lo = c * FF_CHUNK
        gate = _dot(h, wgu_ref[:, lo:lo + FF_CHUNK])
        up = _dot(h, wgu_ref[:, D_FF + lo:D_FF + lo + FF_CHUNK])
        act = (gate * jax.nn.sigmoid(gate) * up).astype(BF16)
        acc = acc + _dot(act, wd_ref[lo:lo + FF_CHUNK, :])
    return acc


def _tail_kernel(x_ref, a1_ref, a2_ref, wo1_ref, wo2_ref, gx_ref, wq_ref, kv_ref, wxo_ref,
                 gf_ref, wgu_ref, wd_ref, gfin_ref, o_ref, *, final_norm):
    for sub in range(TAIL_TM // TAIL_SUB):
        rows = slice(sub * TAIL_SUB, (sub + 1) * TAIL_SUB)
        x2 = _cross_block(x_ref[rows, :], a1_ref[rows, :], a2_ref[rows, :],
                          wo1_ref, wo2_ref, gx_ref, wq_ref, kv_ref, wxo_ref)
        x3 = _ffn_block(x2, gf_ref, wgu_ref, wd_ref)
        if final_norm:
            x3 = _rms_rows(x3, gfin_ref[...])
        o_ref[rows, :] = x3


def _layer_tail(x, a1, a2, w_out, mix_layer, gx, wq, kv, wxo, gf, wgu, wd, layer, gfin,
                final_norm):
    t = x.shape[0]
    per_b = SEQ // TAIL_TM
    row = lambda i: (i, 0)
    const = lambda i: (0, 0)
    return pl.pallas_call(
        functools.partial(_tail_kernel, final_norm=final_norm),
        grid=(t // TAIL_TM,),
        in_specs=[
            pl.BlockSpec((TAIL_TM, D_MODEL), row),
            pl.BlockSpec((TAIL_TM, 512), row),
            pl.BlockSpec((TAIL_TM, 512), row),
            _layer_spec(w_out, mix_layer, rows=512, row_block=0),
            _layer_spec(w_out, mix_layer, rows=512, row_block=1),
            pl.BlockSpec((1, D_MODEL), const),
            _layer_spec(wq, layer),
            pl.BlockSpec((None, 1, MEM_LEN, 2 * X_HEADS * X_HEAD_DIM),
                         lambda i: (layer, i // per_b, 0, 0)),
            _layer_spec(wxo, layer),
            pl.BlockSpec((1, D_MODEL), const),
            _layer_spec(wgu, layer),
            _layer_spec(wd, layer),
            pl.BlockSpec((1, D_MODEL), const),
        ],
        out_specs=pl.BlockSpec((TAIL_TM, D_MODEL), row),
        out_shape=jax.ShapeDtypeStruct((t, D_MODEL), F32),
        compiler_params=pltpu.CompilerParams(
            dimension_semantics=("arbitrary",), vmem_limit_bytes=VMEM_LIMIT),
        name="tail_final" if final_norm else "tail",
    )(x, a1, a2, w_out, w_out, gx, wq, kv, wxo, gf, wgu, wd, gfin)


def _rope_angles(pos, dim):
    inv = ROPE_THETA ** (-jnp.arange(0, dim, 2, dtype=F32) / dim)
    return pos.astype(F32)[:, None] * inv[None, :]


def _head64_tables(ang, scale):
    d = np.arange(LANES) % HEAD_DIM
    sign = np.where(d < HEAD_DIM // 2, -1.0, 1.0).astype(np.float32)
    a = ang[:, d % (HEAD_DIM // 2)]
    return jnp.cos(a) * scale, jnp.sin(a) * sign[None, :] * scale


def _latent_tables(ang, scale):
    lane = np.arange(LANES)
    in_rope = (lane >= C_NOPE) & (lane < C_NOPE + C_ROPE)
    a = ang[:, (lane - C_NOPE) % (C_ROPE // 2)]
    sign = np.where(lane < C_NOPE + C_ROPE // 2, -1.0, 1.0).astype(np.float32)
    cos = jnp.where(in_rope[None, :], jnp.cos(a), 1.0) * scale
    sin = jnp.where(in_rope[None, :], jnp.sin(a) * sign[None, :], 0.0) * scale
    return cos, sin


def _pad_cols(w, width):
    return jnp.pad(w, ((0, 0), (0, width - w.shape[1])))


def kernel(x, mem, g_mix, w_in_ab, g_qa, g_ka, sink_b, w_out_ab, w_in_cd, g_cq, g_ckv,
           w_uq, w_ukv, rpb_d, w_out_cd, g_xq, g_mem, w_xq, w_xkv, w_xo, g_ffn,
           w_gate_up, w_down, g_final):
    b, s, d = x.shape
    assert (s, d) == (SEQ, D_MODEL) and mem.shape == (b, MEM_LEN, D_MODEL)
    depth = g_mix.shape[0]
    t = b * s

    pos = jnp.arange(s)
    ang_1d = _rope_angles(pos, HEAD_DIM)
    ang_2d = jnp.concatenate([_rope_angles(pos // GRID_W, HEAD_DIM // 2),
                              _rope_angles(pos % GRID_W, HEAD_DIM // 2)], axis=-1)
    ang_c = _rope_angles(pos, C_ROPE)
    q_scale = HEAD_DIM ** -0.5 * LOG2E
    tabs_ab = jnp.stack([*_head64_tables(ang_2d, q_scale), *_head64_tables(ang_2d, 1.0),
                         *_head64_tables(ang_1d, q_scale), *_head64_tables(ang_1d, 1.0)])
    tabs_cd = jnp.stack([*_latent_tables(ang_c, (C_NOPE + C_ROPE) ** -0.5 * LOG2E),
                         *_latent_tables(ang_c, 1.0)])
    lane = np.arange(LANES)
    block_ones = jnp.asarray(lane[:, None] // HEAD_DIM == lane[None, :] // HEAD_DIM, BF16)

    row2 = lambda v: v.reshape(1, -1).astype(F32)
    xf = x.reshape(t, d)
    mem_f = mem.reshape(b * MEM_LEN, d)
    w_in_ab_h, w_out_ab_h, w_out_cd_h = (w.astype(BF16) for w in (w_in_ab, w_out_ab, w_out_cd))
    w_xq_h, w_xkv_h, w_xo_h = (w.astype(BF16) for w in (w_xq, w_xkv, w_xo))
    w_gate_up_h, w_down_h = w_gate_up.astype(BF16), w_down.astype(BF16)
    kv_all = _norm_proj(mem_f, g_mem.reshape(depth, 1, d).astype(F32), w_xkv_h)
    kv_all = kv_all.reshape(depth, b, MEM_LEN, 2 * X_HEADS * X_HEAD_DIM)
    bias_tabs = _bias_table(rpb_d.reshape(-1).astype(F32), rpb_d.shape[0])

    for i in range(depth):
        j = i // 2
        if i % 2 == 0:
            gq = row2(jnp.tile(g_qa[j], 2))
            gk = row2(jnp.tile(g_ka[j], 2))
            qa, ka, va, qb, kb, vb = _inproj_ab(
                xf, row2(g_mix[i]), w_in_ab_h, j, gq, gk, block_ones, tabs_ab)
            sh = lambda a: a.reshape(b, s, a.shape[-1])
            o1, o2 = _attn_ab(sink_b[j].astype(F32), sh(qa), sh(ka), sh(va),
                              sh(qb), sh(kb), sh(vb))
            o1, o2 = o1.reshape(t, 512), o2.reshape(t, 512)
            w_out = w_out_ab_h
        else:
            w_in = w_in_cd[j]
            w_in = jnp.concatenate(
                [w_in[:, :C_Q_RANK + C_KV_RANK],
                 _pad_cols(jnp.pad(w_in[:, 384:416], ((0, 0), (C_NOPE, 0))), LANES),
                 w_in[:, 416:]], axis=1).astype(BF16)
            wuq = w_uq[j].reshape(C_Q_RANK, C_HEADS, C_NOPE + C_ROPE)
            wuq = jnp.pad(wuq, ((0, 0), (0, 0), (0, LANES - C_NOPE - C_ROPE)))
            wuq = wuq.reshape(C_Q_RANK, C_HEADS * LANES).astype(BF16)
            wukv = w_ukv[j].reshape(C_KV_RANK, C_HEADS, C_NOPE + C_V)
            wuk = jnp.pad(wukv[:, :, :C_NOPE], ((0, 0), (0, 0), (0, LANES - C_NOPE)))
            wuk = wuk.reshape(C_KV_RANK, C_HEADS * LANES).astype(BF16)
            wuv = wukv[:, :, C_NOPE:].reshape(C_KV_RANK, C_HEADS * C_V).astype(BF16)
            qc, kc, vc, qd, kd, vd = _inproj_cd(
                xf, row2(g_mix[i]), w_in, row2(g_cq[j]), row2(g_ckv[j]), wuq, wuk, wuv, tabs_cd)
            sh = lambda a: a.reshape(b, s, a.shape[-1])
            o1 = _attn_c(sh(qc), sh(kc), sh(vc)).reshape(t, 512)
            o2 = _attn_d(sh(qd), sh(kd), sh(vd), bias_tabs, j).reshape(t, 512)
            w_out = w_out_cd_h

        xf = _layer_tail(xf, o1, o2, w_out, j, row2(g_xq[i]), w_xq_h, kv_all, w_xo_h,
                         row2(g_ffn[i]), w_gate_up_h, w_down_h, i, row2(g_final),
                         final_norm=(i == depth - 1))
    return xf.reshape(b, s, d)
```

```python
import functools

import numpy as np
import jax
import jax.numpy as jnp
from jax import lax
from jax.experimental import pallas as pl
from jax.experimental.pallas import tpu as pltpu

D_MODEL = 1024
SEQ = 2048
HEAD_DIM = 64
GRID_W = 64
MEM_LEN = 256
ROPE_THETA = 10000.0
EPS = 1e-6
NEG = -1e30

A_HEADS = 8
B_HEADS = 8
B_WINDOW = 128
C_HEADS = 8
C_Q_RANK = 256
C_KV_RANK = 128
C_NOPE = 64
C_ROPE = 32
C_V = 64
D_HEADS = 8
D_WIN_R = 8
D_WIN_C = 16
X_HEADS = 4
X_HEAD_DIM = 128
D_FF = 2816

LANES = 128
QUAD = 256
LOG2E = 1.4426950408889634
TM = 1024
TM_SUB = 128
CD_SUB = 256
TAIL_TM = 1024
TAIL_SUB = 512
TQ_DENSE = 512
C_TQ = 1024
TQ_SUB = 256
FF_CHUNK = 256
VMEM_LIMIT = 56 * 1024 * 1024

F32 = jnp.float32
BF16 = jnp.bfloat16

_NT = (((1,), (1,)), ((), ()))


def _layer_spec(w, layer, rows=None, row_block=0):
    k = w.shape[1] if rows is None else rows
    return pl.BlockSpec((None, k, w.shape[2]), lambda *_: (layer, row_block, 0),
                        pipeline_mode=pl.Buffered(1))


def _dot(a, b):
    return jnp.dot(a, b, preferred_element_type=F32)


def _dot_nt(a, b):
    return lax.dot_general(a, b, _NT, preferred_element_type=F32)


def _rms_rows(xf, g):
    return xf * lax.rsqrt(jnp.mean(xf * xf, axis=-1, keepdims=True) + EPS) * g


def _lane_iota(shape):
    return lax.broadcasted_iota(jnp.int32, shape, len(shape) - 1)


def _rope_chunk(x, cos, sin_signed, first_mask, half):
    rot = jnp.where(first_mask, pltpu.roll(x, LANES - half, 1), pltpu.roll(x, half, 1))
    return x * cos + rot * sin_signed


def _half_masks(dtype):
    lane = _lane_iota((1, LANES))
    return (lane < 64).astype(dtype), (lane >= 64).astype(dtype)


def _quarter_masks(dtype):
    lane = _lane_iota((1, QUAD))
    return [((lane >= HEAD_DIM * i) & (lane < HEAD_DIM * (i + 1))).astype(dtype) for i in range(4)]


def _sum_col(i):
    return HEAD_DIM * ((i + 1) % 4)


def _place_head(v_masked, i):
    lane = _lane_iota(v_masked.shape)
    return jnp.where(lane == _sum_col(i), 1.0, v_masked).astype(BF16)


def _softmax2_parts(s):
    m = jnp.max(s, axis=-1, keepdims=True)
    p = jnp.exp2(s - m)
    return p.astype(BF16), jnp.sum(p, axis=-1, keepdims=True), m


def _add_head(o, p, w, i, keep, l_extra=None):
    raw = _dot(p, w)
    l = raw[:, _sum_col(i):_sum_col(i) + 1]
    if l_extra is not None:
        l = l + l_extra
    term = raw * (1.0 / l) * keep[i]
    return term if o is None else o + term


def _inproj_ab_kernel(x_ref, g_ref, w_ref, gq_ref, gk_ref, bd_ref, tab_ref,
                      qa_ref, ka_ref, va_ref, qb_ref, kb_ref, vb_ref):
    lane = _lane_iota((TM_SUB, LANES))
    first = (lane & 63) < 32
    lo = lane < 64
    bd = bd_ref[...]

    def head_norm(zc, gain):
        ss = _dot((zc * zc).astype(BF16), bd)
        return zc * lax.rsqrt(ss * (1.0 / HEAD_DIM) + EPS) * gain

    def dup(c):
        r = pltpu.roll(c, 64, 1)
        return jnp.where(lo, c, r), jnp.where(lo, r, c)

    for sub in range(TM // TM_SUB):
        rows = slice(sub * TM_SUB, (sub + 1) * TM_SUB)
        h = _rms_rows(x_ref[rows, :], g_ref[...]).astype(BF16)
        z = _dot(h, w_ref[...])

        def chunk(i, z=z):
            return z[:, i * LANES:(i + 1) * LANES]

        def tab(i, rows=rows):
            return tab_ref[i, rows, :]

        for c in range(4):
            n = head_norm(chunk(c), gq_ref[...])
            qa_ref[rows, c * LANES:(c + 1) * LANES] = _rope_chunk(
                n, tab(0), tab(1), first, 32).astype(BF16)
        k = _rope_chunk(head_norm(chunk(4), gk_ref[...]), tab(2), tab(3), first, 32)
        k0, k1 = dup(k)
        ka_ref[rows, 0:LANES] = k0.astype(BF16)
        ka_ref[rows, LANES:2 * LANES] = k1.astype(BF16)
        v0, v1 = dup(chunk(5))
        va_ref[rows, 0:LANES] = v0.astype(BF16)
        va_ref[rows, LANES:2 * LANES] = v1.astype(BF16)
        for c in range(4):
            qb_ref[rows, c * LANES:(c + 1) * LANES] = _rope_chunk(
                chunk(6 + c), tab(4), tab(5), first, 32).astype(BF16)
        k = _rope_chunk(chunk(10), tab(6), tab(7), first, 32)
        k0, k1 = dup(k)
        kb_ref[rows, 0:LANES] = k0.astype(BF16)
        kb_ref[rows, LANES:2 * LANES] = k1.astype(BF16)
        v0, v1 = dup(chunk(11))
        vb_ref[rows, 0:LANES] = v0.astype(BF16)
        vb_ref[rows, LANES:2 * LANES] = v1.astype(BF16)


def _inproj_ab(x, g, w, layer, gq, gk, bd, tabs):
    t = x.shape[0]
    n_pos = SEQ // TM
    row = lambda i: (i, 0)
    const = lambda i: (0, 0)
    outs = [(t, 512), (t, 256), (t, 256), (t, 512), (t, 256), (t, 256)]
    return pl.pallas_call(
        _inproj_ab_kernel,
        grid=(t // TM,),
        in_specs=[
            pl.BlockSpec((TM, D_MODEL), row),
            pl.BlockSpec((1, D_MODEL), const),
            _layer_spec(w, layer),
            pl.BlockSpec((1, LANES), const),
            pl.BlockSpec((1, LANES), const),
            pl.BlockSpec((LANES, LANES), const),
            pl.BlockSpec((8, TM, LANES), lambda i: (0, i % n_pos, 0)),
        ],
        out_specs=[pl.BlockSpec((TM, n), row) for _, n in outs],
        out_shape=[jax.ShapeDtypeStruct(s, BF16) for s in outs],
        compiler_params=pltpu.CompilerParams(
            dimension_semantics=("arbitrary",), vmem_limit_bytes=VMEM_LIMIT),
        name="inproj_ab",
    )(x, g, w, gq, gk, bd, tabs)


def _place_shared_v(v2, masks):
    zero = jnp.zeros_like(v2)
    lo, hi = v2 * masks[0], v2 * masks[1]
    return [_place_head(jnp.concatenate(parts, axis=1), i)
            for i, parts in enumerate(((lo, zero), (hi, zero), (zero, lo), (zero, hi)))]


def _attn_a_units(q_ref, k_ref, o_ref, w_ref, masks):
    def unit(kvh, sub):
        rows = slice(sub * TQ_SUB, (sub + 1) * TQ_SUB)
        k2 = k_ref[0, :, kvh * LANES:(kvh + 1) * LANES]
        keep = _quarter_masks(F32)
        probs = []
        for i in range(4):
            c = 2 * kvh + i // 2
            qc = q_ref[0, rows, c * LANES:(c + 1) * LANES]
            s = _dot_nt(qc * masks[i % 2], k2)
            probs.append(jnp.exp2(s - jnp.max(s, axis=-1, keepdims=True)).astype(BF16))
        o = None
        for i, p in enumerate(probs):
            o = _add_head(o, p, w_ref[kvh, i], i, keep)
        o_ref[0, rows, kvh * QUAD:(kvh + 1) * QUAD] = o.astype(BF16)

    return [functools.partial(unit, kvh, sub)
            for kvh in range(2) for sub in range(TQ_DENSE // TQ_SUB)]


B_BLOCK = 256
B_KEYS = B_BLOCK + 2 * B_WINDOW


def _attn_b_units(sink_ref, q_ref, k_ref, v_ref, o_ref, masks):
    def unit(sub, kvh):
        n = pl.program_id(1) * (TQ_DENSE // B_BLOCK) + sub
        rows = slice(sub * B_BLOCK, (sub + 1) * B_BLOCK)
        start = pl.multiple_of(jnp.clip(n * B_BLOCK - B_WINDOW, 0, SEQ - B_KEYS), B_WINDOW)
        q_pos = n * B_BLOCK + lax.broadcasted_iota(jnp.int32, (B_BLOCK, B_KEYS), 0)
        k_pos = start + lax.broadcasted_iota(jnp.int32, (B_BLOCK, B_KEYS), 1)
        delta = k_pos - q_pos
        valid = (delta <= B_WINDOW) & (delta >= -B_WINDOW)
        k2 = k_ref[0, pl.ds(start, B_KEYS), kvh * LANES:(kvh + 1) * LANES]
        ws = _place_shared_v(v_ref[0, pl.ds(start, B_KEYS), kvh * LANES:(kvh + 1) * LANES], masks)
        keep = _quarter_masks(F32)
        heads = []
        for i in range(4):
            c = 2 * kvh + i // 2
            qc = q_ref[0, rows, c * LANES:(c + 1) * LANES]
            sink = sink_ref[4 * kvh + i] * LOG2E
            s = jnp.where(valid, _dot_nt(qc * masks[i % 2], k2), NEG)
            m = jnp.maximum(jnp.max(s, axis=-1, keepdims=True), sink)
            heads.append((jnp.exp2(s - m).astype(BF16), jnp.exp2(sink - m)))
        o = None
        for i, (p, sink_term) in enumerate(heads):
            o = _add_head(o, p, ws[i], i, keep, l_extra=sink_term)
        o_ref[0, rows, kvh * QUAD:(kvh + 1) * QUAD] = o.astype(BF16)

    return [functools.partial(unit, sub, kvh)
            for sub in range(TQ_DENSE // B_BLOCK) for kvh in range(2)]


def _attn_ab_kernel(sink_ref, qa_ref, ka_ref, va_ref, qb_ref, kb_ref, vb_ref,
                    oa_ref, ob_ref, w_ref):
    masks = _half_masks(BF16)

    @pl.when(pl.program_id(1) == 0)
    def _():
        for kvh in range(2):
            placed = _place_shared_v(va_ref[0, :, kvh * LANES:(kvh + 1) * LANES], masks)
            for i in range(4):
                w_ref[kvh, i] = placed[i]

    a_units = _attn_a_units(qa_ref, ka_ref, oa_ref, w_ref, masks)
    b_units = _attn_b_units(sink_ref, qb_ref, kb_ref, vb_ref, ob_ref, masks)
    assert len(a_units) == len(b_units)
    for a_unit, b_unit in zip(a_units, b_units):
        a_unit()
        b_unit()


def _attn_ab(sink, qa, ka, va, qb, kb, vb):
    b = qa.shape[0]
    tile = lambda i, j: (i, j, 0)
    whole = lambda i, j: (i, 0, 0)
    out = jax.ShapeDtypeStruct((b, SEQ, 512), BF16)
    return pl.pallas_call(
        _attn_ab_kernel,
        grid=(b, SEQ // TQ_DENSE),
        in_specs=[
            pl.BlockSpec(memory_space=pltpu.SMEM),
            pl.BlockSpec((1, TQ_DENSE, 512), tile),
            pl.BlockSpec((1, SEQ, 256), whole),
            pl.BlockSpec((1, SEQ, 256), whole),
            pl.BlockSpec((1, TQ_DENSE, 512), tile),
            pl.BlockSpec((1, SEQ, 256), whole),
            pl.BlockSpec((1, SEQ, 256), whole),
        ],
        out_specs=[pl.BlockSpec((1, TQ_DENSE, 512), tile)] * 2,
        out_shape=[out, out],
        scratch_shapes=[pltpu.VMEM((2, 4, SEQ, QUAD), BF16)],
        compiler_params=pltpu.CompilerParams(
            dimension_semantics=("arbitrary", "arbitrary"), vmem_limit_bytes=VMEM_LIMIT),
        name="attn_ab",
    )(sink, qa, ka, va, qb, kb, vb)


def _inproj_cd_kernel(x_ref, g_ref, w_ref, gcq_ref, gckv_ref, wuq_ref, wuk_ref, wuv_ref,
                      tab_ref, qc_ref, kc_ref, vc_ref, qd_ref, kd_ref, vd_ref):
    lane = _lane_iota((CD_SUB, LANES))
    first = lane < 80
    for sub in range(TM // CD_SUB):
        rows = slice(sub * CD_SUB, (sub + 1) * CD_SUB)
        h = _rms_rows(x_ref[rows, :], g_ref[...]).astype(BF16)
        z = _dot(h, w_ref[...])
        cq = _rms_rows(z[:, 0:256], gcq_ref[...]).astype(BF16)
        ckv = _rms_rows(z[:, 256:384], gckv_ref[...]).astype(BF16)
        q = _dot(cq, wuq_ref[...])
        kn = _dot(ckv, wuk_ref[...])
        tabs = [tab_ref[i, rows, :] for i in range(4)]
        kr = _rope_chunk(z[:, 384:512], tabs[2], tabs[3], first, 16)
        for hd in range(C_HEADS):
            sl = slice(hd * LANES, (hd + 1) * LANES)
            qc_ref[rows, sl] = _rope_chunk(q[:, sl], tabs[0], tabs[1], first, 16).astype(BF16)
            kc_ref[rows, sl] = (kn[:, sl] + kr).astype(BF16)
        vc_ref[rows, :] = _dot(ckv, wuv_ref[...]).astype(BF16)
        qd_ref[rows, :] = (z[:, 512:1024] * (HEAD_DIM ** -0.5 * LOG2E)).astype(BF16)
        kd_ref[rows, :] = z[:, 1024:1536].astype(BF16)
        vd_ref[rows, :] = z[:, 1536:2048].astype(BF16)


def _inproj_cd(x, g, w, gcq, gckv, wuq, wuk, wuv, tabs):
    t = x.shape[0]
    n_pos = SEQ // TM
    row = lambda i: (i, 0)
    const = lambda i: (0, 0)
    outs = [(t, 1024), (t, 1024), (t, 512), (t, 512), (t, 512), (t, 512)]
    return pl.pallas_call(
        _inproj_cd_kernel,
        grid=(t // TM,),
        in_specs=[
            pl.BlockSpec((TM, D_MODEL), row),
            pl.BlockSpec((1, D_MODEL), const),
            pl.BlockSpec(w.shape, const),
            pl.BlockSpec((1, C_Q_RANK), const),
            pl.BlockSpec((1, C_KV_RANK), const),
            pl.BlockSpec(wuq.shape, const),
            pl.BlockSpec(wuk.shape, const),
            pl.BlockSpec(wuv.shape, const),
            pl.BlockSpec((4, TM, LANES), lambda i: (0, i % n_pos, 0)),
        ],
        out_specs=[pl.BlockSpec((TM, n), row) for _, n in outs],
        out_shape=[jax.ShapeDtypeStruct(s, BF16) for s in outs],
        compiler_params=pltpu.CompilerParams(
            dimension_semantics=("arbitrary",), vmem_limit_bytes=VMEM_LIMIT),
        name="inproj_cd",
    )(x, g, w, gcq, gckv, wuq, wuk, wuv, tabs)


def _attn_c_kernel(q_ref, k_ref, v_ref, o_ref, w_ref):
    @pl.when(pl.program_id(1) == 0)
    def _():
        masks = _quarter_masks(BF16)
        for g in range(2):
            vq = v_ref[0, :, g * QUAD:(g + 1) * QUAD]
            for i in range(4):
                w_ref[g, i] = _place_head(vq * masks[i], i)

    keep = _quarter_masks(F32)
    for g in range(2):
        for sub in range(C_TQ // TQ_SUB):
            rows = slice(sub * TQ_SUB, (sub + 1) * TQ_SUB)
            o = None
            for i in range(4):
                hd = 4 * g + i
                s = _dot_nt(q_ref[0, rows, hd * LANES:(hd + 1) * LANES],
                            k_ref[0, :, hd * LANES:(hd + 1) * LANES])
                p = jnp.exp2(s - jnp.max(s, axis=-1, keepdims=True)).astype(BF16)
                o = _add_head(o, p, w_ref[g, i], i, keep)
            o_ref[0, rows, g * QUAD:(g + 1) * QUAD] = o.astype(BF16)


def _attn_c(q, k, v):
    b = q.shape[0]
    return pl.pallas_call(
        _attn_c_kernel,
        grid=(b, SEQ // C_TQ),
        in_specs=[
            pl.BlockSpec((1, C_TQ, 1024), lambda i, j: (i, j, 0)),
            pl.BlockSpec((1, SEQ, 1024), lambda i, j: (i, 0, 0)),
            pl.BlockSpec((1, SEQ, 512), lambda i, j: (i, 0, 0)),
        ],
        out_specs=pl.BlockSpec((1, C_TQ, 512), lambda i, j: (i, j, 0)),
        out_shape=jax.ShapeDtypeStruct((b, SEQ, 512), BF16),
        scratch_shapes=[pltpu.VMEM((2, 4, SEQ, QUAD), BF16)],
        compiler_params=pltpu.CompilerParams(
            dimension_semantics=("arbitrary", "arbitrary"), vmem_limit_bytes=VMEM_LIMIT),
        name="attn_c",
    )(q, k, v)


N_ROWS = SEQ // GRID_W
N_DR = 2 * D_WIN_R - 1
N_DC = 2 * D_WIN_C - 1
D_QROWS = 4
D_WROWS = D_QROWS + D_WIN_R
D_Q = D_QROWS * GRID_W
D_KEYS = D_WROWS * GRID_W
D_BLOCKS = N_ROWS // D_QROWS
D_SUBS = 2
D_KINDS = 3


def _d_window_row(kind, a, w):
    if kind == 0:
        return w < D_WIN_R, w - a + (D_WIN_R - 1)
    if kind == 1:
        return a <= w < a + D_WIN_R, w - a + (D_WIN_R - 1) - D_WIN_R // 2
    lead = D_WROWS - D_WIN_R
    return w >= lead, w - a - 1


def _bias_table_kernel(rpb_ref, tab_ref):
    h = pl.program_id(0)
    shape = (GRID_W, LANES)
    qc = lax.broadcasted_iota(jnp.int32, shape, 0)
    lane = lax.broadcasted_iota(jnp.int32, shape, 1)
    kc = lane & (GRID_W - 1)
    lower = lane < GRID_W
    idx = jnp.clip(kc - qc + (D_WIN_C - 1), 0, N_DC - 1)
    c0 = jnp.clip(qc - D_WIN_C // 2, 0, GRID_W - D_WIN_C)
    col_ok = (kc >= c0) & (kc < c0 + D_WIN_C)
    base = h * (N_DR * N_DC)
    tiles = []
    for d in range(N_DR):
        acc = jnp.zeros(shape, F32)
        for j in range(N_DC):
            acc = jnp.where(idx == j, rpb_ref[base + d * N_DC + j], acc)
        tiles.append(jnp.where(col_ok, acc * LOG2E, NEG))
    neg = jnp.full(shape, NEG, F32)
    for kind in range(D_KINDS):
        for a in range(D_QROWS):
            for wp in range(D_WROWS // 2):
                ok_lo, d_lo = _d_window_row(kind, a, 2 * wp)
                ok_hi, d_hi = _d_window_row(kind, a, 2 * wp + 1)
                t_lo = tiles[d_lo] if ok_lo else neg
                t_hi = tiles[d_hi] if ok_hi else neg
                tab_ref[0, kind, a * GRID_W:(a + 1) * GRID_W, wp * LANES:(wp + 1) * LANES] = (
                    jnp.where(lower, t_lo, t_hi))


def _bias_table(rpb_flat, n_layers):
    return pl.pallas_call(
        _bias_table_kernel,
        grid=(n_layers * D_HEADS,),
        in_specs=[pl.BlockSpec(memory_space=pltpu.SMEM)],
        out_specs=pl.BlockSpec((1, D_KINDS, D_Q, D_KEYS), lambda i: (i, 0, 0, 0)),
        out_shape=jax.ShapeDtypeStruct((n_layers * D_HEADS, D_KINDS, D_Q, D_KEYS), F32),
        compiler_params=pltpu.CompilerParams(dimension_semantics=("arbitrary",)),
        name="bias_table",
    )(rpb_flat)


def _attn_d_kernel(q_ref, k_ref, v_ref, tab0_ref, tab1_ref, o_ref):
    halves = _half_masks(BF16)
    quarters = _quarter_masks(BF16)
    keep = _quarter_masks(F32)

    for sub, tab_ref in enumerate((tab0_ref, tab1_ref)):
        blk = pl.program_id(1) * D_SUBS + sub
        rows = slice(sub * D_Q, (sub + 1) * D_Q)
        w0 = jnp.clip(blk * D_QROWS - D_WIN_R // 2, 0, N_ROWS - D_WROWS)
        start = pl.multiple_of(w0 * GRID_W, D_Q)
        for g in range(2):
            vq = v_ref[0, pl.ds(start, D_KEYS), g * QUAD:(g + 1) * QUAD]
            o = None
            for i in range(4):
                hd = 4 * g + i
                c = hd // 2
                qh = q_ref[0, rows, c * LANES:(c + 1) * LANES] * halves[i % 2]
                kwin = k_ref[0, pl.ds(start, D_KEYS), c * LANES:(c + 1) * LANES]
                s = _dot_nt(qh, kwin) + tab_ref[hd, 0]
                p = jnp.exp2(s - jnp.max(s, axis=-1, keepdims=True)).astype(BF16)
                o = _add_head(o, p, _place_head(vq * quarters[i], i), i, keep)
            o_ref[0, rows, g * QUAD:(g + 1) * QUAD] = o.astype(BF16)


def _attn_d(q, k, v, tab, layer):
    b = q.shape[0]

    def kind(sub):
        def index(i, j):
            blk = j * D_SUBS + sub
            return (layer, jnp.where(blk == 0, 0, jnp.where(blk == D_BLOCKS - 1, 2, 1)), 0, 0)
        return index

    return pl.pallas_call(
        _attn_d_kernel,
        grid=(b, D_BLOCKS // D_SUBS),
        in_specs=[
            pl.BlockSpec((1, D_SUBS * D_Q, 512), lambda i, j: (i, j, 0)),
            pl.BlockSpec((1, SEQ, 512), lambda i, j: (i, 0, 0)),
            pl.BlockSpec((1, SEQ, 512), lambda i, j: (i, 0, 0)),
            pl.BlockSpec((D_HEADS, 1, D_Q, D_KEYS), kind(0)),
            pl.BlockSpec((D_HEADS, 1, D_Q, D_KEYS), kind(1)),
        ],
        out_specs=pl.BlockSpec((1, D_SUBS * D_Q, 512), lambda i, j: (i, j, 0)),
        out_shape=jax.ShapeDtypeStruct((b, SEQ, 512), BF16),
        compiler_params=pltpu.CompilerParams(
            dimension_semantics=("arbitrary", "arbitrary"), vmem_limit_bytes=VMEM_LIMIT),
        name="attn_d",
    )(q, k, v, tab, tab)


def _norm_proj_kernel(x_ref, g_ref, w_ref, o_ref):
    h = _rms_rows(x_ref[...], g_ref[...]).astype(BF16)
    o_ref[...] = _dot(h, w_ref[...]).astype(BF16)


def _norm_proj(x, g, w):
    t = x.shape[0]
    depth, _, n = w.shape
    return pl.pallas_call(
        _norm_proj_kernel,
        grid=(depth, t // TM),
        in_specs=[
            pl.BlockSpec((TM, D_MODEL), lambda l, i: (i, 0)),
            pl.BlockSpec((None, 1, D_MODEL), lambda l, i: (l, 0, 0)),
            pl.BlockSpec((None, D_MODEL, n), lambda l, i: (l, 0, 0)),
        ],
        out_specs=pl.BlockSpec((None, TM, n), lambda l, i: (l, i, 0)),
        out_shape=jax.ShapeDtypeStruct((depth, t, n), BF16),
        compiler_params=pltpu.CompilerParams(
            dimension_semantics=("arbitrary", "arbitrary"), vmem_limit_bytes=VMEM_LIMIT),
        name="mem_kv_proj",
    )(x, g, w)


def _cross_block(x, a1, a2, wo1_ref, wo2_ref, g_ref, wq_ref, kv_ref, wxo_ref):
    x1 = x + _dot(a1, wo1_ref[...]) + _dot(a2, wo2_ref[...])
    h = _rms_rows(x1, g_ref[...]).astype(BF16)
    q = (_dot(h, wq_ref[...]) * (X_HEAD_DIM ** -0.5 * LOG2E)).astype(BF16)
    heads = []
    for hd in range(X_HEADS):
        sl = slice(hd * LANES, (hd + 1) * LANES)
        k = kv_ref[0, :, sl]
        v = kv_ref[0, :, X_HEADS * LANES + hd * LANES:X_HEADS * LANES + (hd + 1) * LANES]
        p, l, _ = _softmax2_parts(_dot_nt(q[:, sl], k))
        heads.append((_dot(p, v) * (1.0 / l)).astype(BF16))
    return x1 + _dot(jnp.concatenate(heads, axis=1), wxo_ref[...])


def _ffn_block(x, g_ref, wgu_ref, wd_ref):
    h = _rms_rows(x, g_ref[...]).astype(BF16)
    acc = x
    for c in range(D_FF // FF_CHUNK):
        lo = c * FF_CHUNK
        gate = _dot(h, wgu_ref[:, lo:lo + FF_CHUNK])
        up = _dot(h, wgu_ref[:, D_FF + lo:D_FF + lo + FF_CHUNK])
        act = (gate * jax.nn.sigmoid(gate) * up).astype(BF16)
        acc = acc + _dot(act, wd_ref[lo:lo + FF_CHUNK, :])
    return acc


def _tail_kernel(x_ref, a1_ref, a2_ref, wo1_ref, wo2_ref, gx_ref, wq_ref, kv_ref, wxo_ref,
                 gf_ref, wgu_ref, wd_ref, gfin_ref, o_ref, *, final_norm):
    for sub in range(TAIL_TM // TAIL_SUB):
        rows = slice(sub * TAIL_SUB, (sub + 1) * TAIL_SUB)
        x2 = _cross_block(x_ref[rows, :], a1_ref[rows, :], a2_ref[rows, :],
                          wo1_ref, wo2_ref, gx_ref, wq_ref, kv_ref, wxo_ref)
        x3 = _ffn_block(x2, gf_ref, wgu_ref, wd_ref)
        if final_norm:
            x3 = _rms_rows(x3, gfin_ref[...])
        o_ref[rows, :] = x3


def _layer_tail(x, a1, a2, w_out, mix_layer, gx, wq, kv, wxo, gf, wgu, wd, layer, gfin,
                final_norm):
    t = x.shape[0]
    per_b = SEQ // TAIL_TM
    row = lambda i: (i, 0)
    const = lambda i: (0, 0)
    return pl.pallas_call(
        functools.partial(_tail_kernel, final_norm=final_norm),
        grid=(t // TAIL_TM,),
        in_specs=[
            pl.BlockSpec((TAIL_TM, D_MODEL), row),
            pl.BlockSpec((TAIL_TM, 512), row),
            pl.BlockSpec((TAIL_TM, 512), row),
            _layer_spec(w_out, mix_layer, rows=512, row_block=0),
            _layer_spec(w_out, mix_layer, rows=512, row_block=1),
            pl.BlockSpec((1, D_MODEL), const),
            _layer_spec(wq, layer),
            pl.BlockSpec((None, 1, MEM_LEN, 2 * X_HEADS * X_HEAD_DIM),
                         lambda i: (layer, i // per_b, 0, 0)),
            _layer_spec(wxo, layer),
            pl.BlockSpec((1, D_MODEL), const),
            _layer_spec(wgu, layer),
            _layer_spec(wd, layer),
            pl.BlockSpec((1, D_MODEL), const),
        ],
        out_specs=pl.BlockSpec((TAIL_TM, D_MODEL), row),
        out_shape=jax.ShapeDtypeStruct((t, D_MODEL), F32),
        compiler_params=pltpu.CompilerParams(
            dimension_semantics=("arbitrary",), vmem_limit_bytes=VMEM_LIMIT),
        name="tail_final" if final_norm else "tail",
    )(x, a1, a2, w_out, w_out, gx, wq, kv, wxo, gf, wgu, wd, gfin)


def _rope_angles(pos, dim):
    inv = ROPE_THETA ** (-jnp.arange(0, dim, 2, dtype=F32) / dim)
    return pos.astype(F32)[:, None] * inv[None, :]


def _head64_tables(ang, scale):
    d = np.arange(LANES) % HEAD_DIM
    sign = np.where(d < HEAD_DIM // 2, -1.0, 1.0).astype(np.float32)
    a = ang[:, d % (HEAD_DIM // 2)]
    return jnp.cos(a) * scale, jnp.sin(a) * sign[None, :] * scale


def _latent_tables(ang, scale):
    lane = np.arange(LANES)
    in_rope = (lane >= C_NOPE) & (lane < C_NOPE + C_ROPE)
    a = ang[:, (lane - C_NOPE) % (C_ROPE // 2)]
    sign = np.where(lane < C_NOPE + C_ROPE // 2, -1.0, 1.0).astype(np.float32)
    cos = jnp.where(in_rope[None, :], jnp.cos(a), 1.0) * scale
    sin = jnp.where(in_rope[None, :], jnp.sin(a) * sign[None, :], 0.0) * scale
    return cos, sin


def _pad_cols(w, width):
    return jnp.pad(w, ((0, 0), (0, width - w.shape[1])))


def kernel(x, mem, g_mix, w_in_ab, g_qa, g_ka, sink_b, w_out_ab, w_in_cd, g_cq, g_ckv,
           w_uq, w_ukv, rpb_d, w_out_cd, g_xq, g_mem, w_xq, w_xkv, w_xo, g_ffn,
           w_gate_up, w_down, g_final):
    b, s, d = x.shape
    assert (s, d) == (SEQ, D_MODEL) and mem.shape == (b, MEM_LEN, D_MODEL)
    depth = g_mix.shape[0]
    t = b * s

    pos = jnp.arange(s)
    ang_1d = _rope_angles(pos, HEAD_DIM)
    ang_2d = jnp.concatenate([_rope_angles(pos // GRID_W, HEAD_DIM // 2),
                              _rope_angles(pos % GRID_W, HEAD_DIM // 2)], axis=-1)
    ang_c = _rope_angles(pos, C_ROPE)
    q_scale = HEAD_DIM ** -0.5 * LOG2E
    tabs_ab = jnp.stack([*_head64_tables(ang_2d, q_scale), *_head64_tables(ang_2d, 1.0),
                         *_head64_tables(ang_1d, q_scale), *_head64_tables(ang_1d, 1.0)])
    tabs_cd = jnp.stack([*_latent_tables(ang_c, (C_NOPE + C_ROPE) ** -0.5 * LOG2E),
                         *_latent_tables(ang_c, 1.0)])
    lane = np.arange(LANES)
    block_ones = jnp.asarray(lane[:, None] // HEAD_DIM == lane[None, :] // HEAD_DIM, BF16)

    row2 = lambda v: v.reshape(1, -1).astype(F32)
    xf = x.reshape(t, d)
    mem_f = mem.reshape(b * MEM_LEN, d)
    w_in_ab_h, w_out_ab_h, w_out_cd_h = (w.astype(BF16) for w in (w_in_ab, w_out_ab, w_out_cd))
    w_xq_h, w_xkv_h, w_xo_h = (w.astype(BF16) for w in (w_xq, w_xkv, w_xo))
    w_gate_up_h, w_down_h = w_gate_up.astype(BF16), w_down.astype(BF16)
    kv_all = _norm_proj(mem_f, g_mem.reshape(depth, 1, d).astype(F32), w_xkv_h)
    kv_all = kv_all.reshape(depth, b, MEM_LEN, 2 * X_HEADS * X_HEAD_DIM)
    bias_tabs = _bias_table(rpb_d.reshape(-1).astype(F32), rpb_d.shape[0])

    for i in range(depth):
        j = i // 2
        if i % 2 == 0:
            gq = row2(jnp.tile(g_qa[j], 2))
            gk = row2(jnp.tile(g_ka[j], 2))
            qa, ka, va, qb, kb, vb = _inproj_ab(
                xf, row2(g_mix[i]), w_in_ab_h, j, gq, gk, block_ones, tabs_ab)
            sh = lambda a: a.reshape(b, s, a.shape[-1])
            o1, o2 = _attn_ab(sink_b[j].astype(F32), sh(qa), sh(ka), sh(va),
                              sh(qb), sh(kb), sh(vb))
            o1, o2 = o1.reshape(t, 512), o2.reshape(t, 512)
            w_out = w_out_ab_h
        else:
            w_in = w_in_cd[j]
            w_in = jnp.concatenate(
                [w_in[:, :C_Q_RANK + C_KV_RANK],
                 _pad_cols(jnp.pad(w_in[:, 384:416], ((0, 0), (C_NOPE, 0))), LANES),
                 w_in[:, 416:]], axis=1).astype(BF16)
            wuq = w_uq[j].reshape(C_Q_RANK, C_HEADS, C_NOPE + C_ROPE)
            wuq = jnp.pad(wuq, ((0, 0), (0, 0), (0, LANES - C_NOPE - C_ROPE)))
            wuq = wuq.reshape(C_Q_RANK, C_HEADS * LANES).astype(BF16)
            wukv = w_ukv[j].reshape(C_KV_RANK, C_HEADS, C_NOPE + C_V)
            wuk = jnp.pad(wukv[:, :, :C_NOPE], ((0, 0), (0, 0), (0, LANES - C_NOPE)))
            wuk = wuk.reshape(C_KV_RANK, C_HEADS * LANES).astype(BF16)
            wuv = wukv[:, :, C_NOPE:].reshape(C_KV_RANK, C_HEADS * C_V).astype(BF16)
            qc, kc, vc, qd, kd, vd = _inproj_cd(
                xf, row2(g_mix[i]), w_in, row2(g_cq[j]), row2(g_ckv[j]), wuq, wuk, wuv, tabs_cd)
            sh = lambda a: a.reshape(b, s, a.shape[-1])
            o1 = _attn_c(sh(qc), sh(kc), sh(vc)).reshape(t, 512)
            o2 = _attn_d(sh(qd), sh(kd), sh(vd), bias_tabs, j).reshape(t, 512)
            w_out = w_out_cd_h

        xf = _layer_tail(xf, o1, o2, w_out, j, row2(g_xq[i]), w_xq_h, kv_all, w_xo_h,
                         row2(g_ffn[i]), w_gate_up_h, w_down_h, i, row2(g_final),
                         final_norm=(i == depth - 1))
    return xf.reshape(b, s, d)
```

```python
import functools

import numpy as np
import jax
import jax.numpy as jnp
from jax import lax
from jax.experimental import pallas as pl
from jax.experimental.pallas import tpu as pltpu

D_MODEL = 1024
SEQ = 2048
HEAD_DIM = 64
GRID_W = 64
MEM_LEN = 256
ROPE_THETA = 10000.0
EPS = 1e-6
NEG = -1e30

A_HEADS = 8
B_HEADS = 8
B_WINDOW = 128
C_HEADS = 8
C_Q_RANK = 256
C_KV_RANK = 128
C_NOPE = 64
C_ROPE = 32
C_V = 64
D_HEADS = 8
D_WIN_R = 8
D_WIN_C = 16
X_HEADS = 4
X_HEAD_DIM = 128
D_FF = 2816

LANES = 128
QUAD = 256
LOG2E = 1.4426950408889634
TM = 1024
TM_SUB = 128
CD_SUB = 256
TAIL_TM = 1024
TAIL_SUB = 512
TQ_DENSE = 1024
C_TQ = 1024
TQ_SUB = 256
FF_CHUNK = 256
VMEM_LIMIT = 56 * 1024 * 1024

F32 = jnp.float32
BF16 = jnp.bfloat16

_NT = (((1,), (1,)), ((), ()))


def _layer_spec(w, layer, rows=None, row_block=0):
    k = w.shape[1] if rows is None else rows
    return pl.BlockSpec((None, k, w.shape[2]), lambda *_: (layer, row_block, 0),
                        pipeline_mode=pl.Buffered(1))


def _dot(a, b):
    return jnp.dot(a, b, preferred_element_type=F32)


def _dot_nt(a, b):
    return lax.dot_general(a, b, _NT, preferred_element_type=F32)


def _rms_rows(xf, g):
    return xf * lax.rsqrt(jnp.mean(xf * xf, axis=-1, keepdims=True) + EPS) * g


def _lane_iota(shape):
    return lax.broadcasted_iota(jnp.int32, shape, len(shape) - 1)


def _rope_chunk(x, cos, sin_signed, first_mask, half):
    rot = jnp.where(first_mask, pltpu.roll(x, LANES - half, 1), pltpu.roll(x, half, 1))
    return x * cos + rot * sin_signed


def _half_masks(dtype):
    lane = _lane_iota((1, LANES))
    return (lane < 64).astype(dtype), (lane >= 64).astype(dtype)


def _quarter_masks(dtype):
    lane = _lane_iota((1, QUAD))
    return [((lane >= HEAD_DIM * i) & (lane < HEAD_DIM * (i + 1))).astype(dtype) for i in range(4)]


def _sum_col(i):
    return HEAD_DIM * ((i + 1) % 4)


def _place_head(v_masked, i):
    lane = _lane_iota(v_masked.shape)
    return jnp.where(lane == _sum_col(i), 1.0, v_masked).astype(BF16)


def _softmax2_parts(s):
    m = jnp.max(s, axis=-1, keepdims=True)
    p = jnp.exp2(s - m)
    return p.astype(BF16), jnp.sum(p, axis=-1, keepdims=True), m


def _add_head(o, p, w, i, keep, l_extra=None):
    raw = _dot(p, w)
    l = raw[:, _sum_col(i):_sum_col(i) + 1]
    if l_extra is not None:
        l = l + l_extra
    term = raw * (1.0 / l) * keep[i]
    return term if o is None else o + term


def _inproj_ab_kernel(x_ref, g_ref, w_ref, gq_ref, gk_ref, bd_ref, tab_ref,
                      qa_ref, ka_ref, va_ref, qb_ref, kb_ref, vb_ref):
    lane = _lane_iota((TM_SUB, LANES))
    first = (lane & 63) < 32
    lo = lane < 64
    bd = bd_ref[...]

    def head_norm(zc, gain):
        ss = _dot((zc * zc).astype(BF16), bd)
        return zc * lax.rsqrt(ss * (1.0 / HEAD_DIM) + EPS) * gain

    def dup(c):
        r = pltpu.roll(c, 64, 1)
        return jnp.where(lo, c, r), jnp.where(lo, r, c)

    for sub in range(TM // TM_SUB):
        rows = slice(sub * TM_SUB, (sub + 1) * TM_SUB)
        h = _rms_rows(x_ref[rows, :], g_ref[...]).astype(BF16)
        z = _dot(h, w_ref[...])

        def chunk(i, z=z):
            return z[:, i * LANES:(i + 1) * LANES]

        def tab(i, rows=rows):
            return tab_ref[i, rows, :]

        for c in range(4):
            n = head_norm(chunk(c), gq_ref[...])
            qa_ref[rows, c * LANES:(c + 1) * LANES] = _rope_chunk(
                n, tab(0), tab(1), first, 32).astype(BF16)
        k = _rope_chunk(head_norm(chunk(4), gk_ref[...]), tab(2), tab(3), first, 32)
        k0, k1 = dup(k)
        ka_ref[rows, 0:LANES] = k0.astype(BF16)
        ka_ref[rows, LANES:2 * LANES] = k1.astype(BF16)
        v0, v1 = dup(chunk(5))
        va_ref[rows, 0:LANES] = v0.astype(BF16)
        va_ref[rows, LANES:2 * LANES] = v1.astype(BF16)
        for c in range(4):
            qb_ref[rows, c * LANES:(c + 1) * LANES] = _rope_chunk(
                chunk(6 + c), tab(4), tab(5), first, 32).astype(BF16)
        k = _rope_chunk(chunk(10), tab(6), tab(7), first, 32)
        k0, k1 = dup(k)
        kb_ref[rows, 0:LANES] = k0.astype(BF16)
        kb_ref[rows, LANES:2 * LANES] = k1.astype(BF16)
        v0, v1 = dup(chunk(11))
        vb_ref[rows, 0:LANES] = v0.astype(BF16)
        vb_ref[rows, LANES:2 * LANES] = v1.astype(BF16)


def _inproj_ab(x, g, w, layer, gq, gk, bd, tabs):
    t = x.shape[0]
    n_pos = SEQ // TM
    row = lambda i: (i, 0)
    const = lambda i: (0, 0)
    outs = [(t, 512), (t, 256), (t, 256), (t, 512), (t, 256), (t, 256)]
    return pl.pallas_call(
        _inproj_ab_kernel,
        grid=(t // TM,),
        in_specs=[
            pl.BlockSpec((TM, D_MODEL), row),
            pl.BlockSpec((1, D_MODEL), const),
            _layer_spec(w, layer),
            pl.BlockSpec((1, LANES), const),
            pl.BlockSpec((1, LANES), const),
            pl.BlockSpec((LANES, LANES), const),
            pl.BlockSpec((8, TM, LANES), lambda i: (0, i % n_pos, 0)),
        ],
        out_specs=[pl.BlockSpec((TM, n), row) for _, n in outs],
        out_shape=[jax.ShapeDtypeStruct(s, BF16) for s in outs],
        compiler_params=pltpu.CompilerParams(
            dimension_semantics=("arbitrary",), vmem_limit_bytes=VMEM_LIMIT),
        name="inproj_ab",
    )(x, g, w, gq, gk, bd, tabs)


def _place_shared_v(v2, masks):
    zero = jnp.zeros_like(v2)
    lo, hi = v2 * masks[0], v2 * masks[1]
    return [_place_head(jnp.concatenate(parts, axis=1), i)
            for i, parts in enumerate(((lo, zero), (hi, zero), (zero, lo), (zero, hi)))]


def _attn_a_units(q_ref, k_ref, o_ref, w_ref, masks):
    def unit(kvh, sub):
        rows = slice(sub * TQ_SUB, (sub + 1) * TQ_SUB)
        k2 = k_ref[0, :, kvh * LANES:(kvh + 1) * LANES]
        keep = _quarter_masks(F32)
        probs = []
        for i in range(4):
            c = 2 * kvh + i // 2
            qc = q_ref[0, rows, c * LANES:(c + 1) * LANES]
            s = _dot_nt(qc * masks[i % 2], k2)
            probs.append(jnp.exp2(s - jnp.max(s, axis=-1, keepdims=True)).astype(BF16))
        o = None
        for i, p in enumerate(probs):
            o = _add_head(o, p, w_ref[kvh, i], i, keep)
        o_ref[0, rows, kvh * QUAD:(kvh + 1) * QUAD] = o.astype(BF16)

    return [functools.partial(unit, kvh, sub)
            for kvh in range(2) for sub in range(TQ_DENSE // TQ_SUB)]


B_BLOCK = 256
B_KEYS = B_BLOCK + 2 * B_WINDOW


def _attn_b_units(sink_ref, q_ref, k_ref, v_ref, o_ref, masks):
    def unit(sub, kvh):
        n = pl.program_id(1) * (TQ_DENSE // B_BLOCK) + sub
        rows = slice(sub * B_BLOCK, (sub + 1) * B_BLOCK)
        start = pl.multiple_of(jnp.clip(n * B_BLOCK - B_WINDOW, 0, SEQ - B_KEYS), B_WINDOW)
        q_pos = n * B_BLOCK + lax.broadcasted_iota(jnp.int32, (B_BLOCK, B_KEYS), 0)
        k_pos = start + lax.broadcasted_iota(jnp.int32, (B_BLOCK, B_KEYS), 1)
        delta = k_pos - q_pos
        valid = (delta <= B_WINDOW) & (delta >= -B_WINDOW)
        k2 = k_ref[0, pl.ds(start, B_KEYS), kvh * LANES:(kvh + 1) * LANES]
        ws = _place_shared_v(v_ref[0, pl.ds(start, B_KEYS), kvh * LANES:(kvh + 1) * LANES], masks)
        keep = _quarter_masks(F32)
        heads = []
        for i in range(4):
            c = 2 * kvh + i // 2
            qc = q_ref[0, rows, c * LANES:(c + 1) * LANES]
            sink = sink_ref[4 * kvh + i] * LOG2E
            s = jnp.where(valid, _dot_nt(qc * masks[i % 2], k2), NEG)
            m = jnp.maximum(jnp.max(s, axis=-1, keepdims=True), sink)
            heads.append((jnp.exp2(s - m).astype(BF16), jnp.exp2(sink - m)))
        o = None
        for i, (p, sink_term) in enumerate(heads):
            o = _add_head(o, p, ws[i], i, keep, l_extra=sink_term)
        o_ref[0, rows, kvh * QUAD:(kvh + 1) * QUAD] = o.astype(BF16)

    return [functools.partial(unit, sub, kvh)
            for sub in range(TQ_DENSE // B_BLOCK) for kvh in range(2)]


def _attn_ab_kernel(sink_ref, qa_ref, ka_ref, va_ref, qb_ref, kb_ref, vb_ref,
                    oa_ref, ob_ref, w_ref):
    masks = _half_masks(BF16)

    @pl.when(pl.program_id(1) == 0)
    def _():
        for kvh in range(2):
            placed = _place_shared_v(va_ref[0, :, kvh * LANES:(kvh + 1) * LANES], masks)
            for i in range(4):
                w_ref[kvh, i] = placed[i]

    a_units = _attn_a_units(qa_ref, ka_ref, oa_ref, w_ref, masks)
    b_units = _attn_b_units(sink_ref, qb_ref, kb_ref, vb_ref, ob_ref, masks)
    assert len(a_units) == len(b_units)
    for a_unit, b_unit in zip(a_units, b_units):
        a_unit()
        b_unit()


def _attn_ab(sink, qa, ka, va, qb, kb, vb):
    b = qa.shape[0]
    tile = lambda i, j: (i, j, 0)
    whole = lambda i, j: (i, 0, 0)
    out = jax.ShapeDtypeStruct((b, SEQ, 512), BF16)
    return pl.pallas_call(
        _attn_ab_kernel,
        grid=(b, SEQ // TQ_DENSE),
        in_specs=[
            pl.BlockSpec(memory_space=pltpu.SMEM),
            pl.BlockSpec((1, TQ_DENSE, 512), tile),
            pl.BlockSpec((1, SEQ, 256), whole),
            pl.BlockSpec((1, SEQ, 256), whole),
            pl.BlockSpec((1, TQ_DENSE, 512), tile),
            pl.BlockSpec((1, SEQ, 256), whole),
            pl.BlockSpec((1, SEQ, 256), whole),
        ],
        out_specs=[pl.BlockSpec((1, TQ_DENSE, 512), tile)] * 2,
        out_shape=[out, out],
        scratch_shapes=[pltpu.VMEM((2, 4, SEQ, QUAD), BF16)],
        compiler_params=pltpu.CompilerParams(
            dimension_semantics=("arbitrary", "arbitrary"), vmem_limit_bytes=VMEM_LIMIT),
        name="attn_ab",
    )(sink, qa, ka, va, qb, kb, vb)


def _inproj_cd_kernel(x_ref, g_ref, w_ref, gcq_ref, gckv_ref, wuq_ref, wuk_ref, wuv_ref,
                      tab_ref, qc_ref, kc_ref, vc_ref, qd_ref, kd_ref, vd_ref):
    lane = _lane_iota((CD_SUB, LANES))
    first = lane < 80
    for sub in range(TM // CD_SUB):
        rows = slice(sub * CD_SUB, (sub + 1) * CD_SUB)
        h = _rms_rows(x_ref[rows, :], g_ref[...]).astype(BF16)
        z = _dot(h, w_ref[...])
        cq = _rms_rows(z[:, 0:256], gcq_ref[...]).astype(BF16)
        ckv = _rms_rows(z[:, 256:384], gckv_ref[...]).astype(BF16)
        q = _dot(cq, wuq_ref[...])
        kn = _dot(ckv, wuk_ref[...])
        tabs = [tab_ref[i, rows, :] for i in range(4)]
        kr = _rope_chunk(z[:, 384:512], tabs[2], tabs[3], first, 16)
        for hd in range(C_HEADS):
            sl = slice(hd * LANES, (hd + 1) * LANES)
            qc_ref[rows, sl] = _rope_chunk(q[:, sl], tabs[0], tabs[1], first, 16).astype(BF16)
            kc_ref[rows, sl] = (kn[:, sl] + kr).astype(BF16)
        vc_ref[rows, :] = _dot(ckv, wuv_ref[...]).astype(BF16)
        qd_ref[rows, :] = (z[:, 512:1024] * (HEAD_DIM ** -0.5 * LOG2E)).astype(BF16)
        kd_ref[rows, :] = z[:, 1024:1536].astype(BF16)
        vd_ref[rows, :] = z[:, 1536:2048].astype(BF16)


def _inproj_cd(x, g, w, gcq, gckv, wuq, wuk, wuv, tabs):
    t = x.shape[0]
    n_pos = SEQ // TM
    row = lambda i: (i, 0)
    const = lambda i: (0, 0)
    outs = [(t, 1024), (t, 1024), (t, 512), (t, 512), (t, 512), (t, 512)]
    return pl.pallas_call(
        _inproj_cd_kernel,
        grid=(t // TM,),
        in_specs=[
            pl.BlockSpec((TM, D_MODEL), row),
            pl.BlockSpec((1, D_MODEL), const),
            pl.BlockSpec(w.shape, const),
            pl.BlockSpec((1, C_Q_RANK), const),
            pl.BlockSpec((1, C_KV_RANK), const),
            pl.BlockSpec(wuq.shape, const),
            pl.BlockSpec(wuk.shape, const),
            pl.BlockSpec(wuv.shape, const),
            pl.BlockSpec((4, TM, LANES), lambda i: (0, i % n_pos, 0)),
        ],
        out_specs=[pl.BlockSpec((TM, n), row) for _, n in outs],
        out_shape=[jax.ShapeDtypeStruct(s, BF16) for s in outs],
        compiler_params=pltpu.CompilerParams(
            dimension_semantics=("arbitrary",), vmem_limit_bytes=VMEM_LIMIT),
        name="inproj_cd",
    )(x, g, w, gcq, gckv, wuq, wuk, wuv, tabs)


def _attn_c_kernel(q_ref, k_ref, v_ref, o_ref, w_ref):
    @pl.when(pl.program_id(1) == 0)
    def _():
        masks = _quarter_masks(BF16)
        for g in range(2):
            vq = v_ref[0, :, g * QUAD:(g + 1) * QUAD]
            for i in range(4):
                w_ref[g, i] = _place_head(vq * masks[i], i)

    keep = _quarter_masks(F32)
    for g in range(2):
        for sub in range(C_TQ // TQ_SUB):
            rows = slice(sub * TQ_SUB, (sub + 1) * TQ_SUB)
            o = None
            for i in range(4):
                hd = 4 * g + i
                s = _dot_nt(q_ref[0, rows, hd * LANES:(hd + 1) * LANES],
                            k_ref[0, :, hd * LANES:(hd + 1) * LANES])
                p = jnp.exp2(s - jnp.max(s, axis=-1, keepdims=True)).astype(BF16)
                o = _add_head(o, p, w_ref[g, i], i, keep)
            o_ref[0, rows, g * QUAD:(g + 1) * QUAD] = o.astype(BF16)


def _attn_c(q, k, v):
    b = q.shape[0]
    return pl.pallas_call(
        _attn_c_kernel,
        grid=(b, SEQ // C_TQ),
        in_specs=[
            pl.BlockSpec((1, C_TQ, 1024), lambda i, j: (i, j, 0)),
            pl.BlockSpec((1, SEQ, 1024), lambda i, j: (i, 0, 0)),
            pl.BlockSpec((1, SEQ, 512), lambda i, j: (i, 0, 0)),
        ],
        out_specs=pl.BlockSpec((1, C_TQ, 512), lambda i, j: (i, j, 0)),
        out_shape=jax.ShapeDtypeStruct((b, SEQ, 512), BF16),
        scratch_shapes=[pltpu.VMEM((2, 4, SEQ, QUAD), BF16)],
        compiler_params=pltpu.CompilerParams(
            dimension_semantics=("arbitrary", "arbitrary"), vmem_limit_bytes=VMEM_LIMIT),
        name="attn_c",
    )(q, k, v)


N_ROWS = SEQ // GRID_W
N_DR = 2 * D_WIN_R - 1
N_DC = 2 * D_WIN_C - 1
D_QROWS = 4
D_WROWS = D_QROWS + D_WIN_R
D_Q = D_QROWS * GRID_W
D_KEYS = D_WROWS * GRID_W
D_BLOCKS = N_ROWS // D_QROWS
D_SUBS = 2
D_KINDS = 3


def _d_window_row(kind, a, w):
    if kind == 0:
        return w < D_WIN_R, w - a + (D_WIN_R - 1)
    if kind == 1:
        return a <= w < a + D_WIN_R, w - a + (D_WIN_R - 1) - D_WIN_R // 2
    lead = D_WROWS - D_WIN_R
    return w >= lead, w - a - 1


def _bias_table_kernel(rpb_ref, tab_ref):
    h = pl.program_id(0)
    shape = (GRID_W, LANES)
    qc = lax.broadcasted_iota(jnp.int32, shape, 0)
    lane = lax.broadcasted_iota(jnp.int32, shape, 1)
    kc = lane & (GRID_W - 1)
    lower = lane < GRID_W
    idx = jnp.clip(kc - qc + (D_WIN_C - 1), 0, N_DC - 1)
    c0 = jnp.clip(qc - D_WIN_C // 2, 0, GRID_W - D_WIN_C)
    col_ok = (kc >= c0) & (kc < c0 + D_WIN_C)
    base = h * (N_DR * N_DC)
    tiles = []
    for d in range(N_DR):
        acc = jnp.zeros(shape, F32)
        for j in range(N_DC):
            acc = jnp.where(idx == j, rpb_ref[base + d * N_DC + j], acc)
        tiles.append(jnp.where(col_ok, acc * LOG2E, NEG))
    neg = jnp.full(shape, NEG, F32)
    for kind in range(D_KINDS):
        for a in range(D_QROWS):
            for wp in range(D_WROWS // 2):
                ok_lo, d_lo = _d_window_row(kind, a, 2 * wp)
                ok_hi, d_hi = _d_window_row(kind, a, 2 * wp + 1)
                t_lo = tiles[d_lo] if ok_lo else neg
                t_hi = tiles[d_hi] if ok_hi else neg
                tab_ref[0, kind, a * GRID_W:(a + 1) * GRID_W, wp * LANES:(wp + 1) * LANES] = (
                    jnp.where(lower, t_lo, t_hi))


def _bias_table(rpb_flat, n_layers):
    return pl.pallas_call(
        _bias_table_kernel,
        grid=(n_layers * D_HEADS,),
        in_specs=[pl.BlockSpec(memory_space=pltpu.SMEM)],
        out_specs=pl.BlockSpec((1, D_KINDS, D_Q, D_KEYS), lambda i: (i, 0, 0, 0)),
        out_shape=jax.ShapeDtypeStruct((n_layers * D_HEADS, D_KINDS, D_Q, D_KEYS), F32),
        compiler_params=pltpu.CompilerParams(dimension_semantics=("arbitrary",)),
        name="bias_table",
    )(rpb_flat)


def _attn_d_kernel(q_ref, k_ref, v_ref, tab0_ref, tab1_ref, o_ref):
    halves = _half_masks(BF16)
    quarters = _quarter_masks(BF16)
    keep = _quarter_masks(F32)

    for sub, tab_ref in enumerate((tab0_ref, tab1_ref)):
        blk = pl.program_id(1) * D_SUBS + sub
        rows = slice(sub * D_Q, (sub + 1) * D_Q)
        w0 = jnp.clip(blk * D_QROWS - D_WIN_R // 2, 0, N_ROWS - D_WROWS)
        start = pl.multiple_of(w0 * GRID_W, D_Q)
        for g in range(2):
            vq = v_ref[0, pl.ds(start, D_KEYS), g * QUAD:(g + 1) * QUAD]
            o = None
            for i in range(4):
                hd = 4 * g + i
                c = hd // 2
                qh = q_ref[0, rows, c * LANES:(c + 1) * LANES] * halves[i % 2]
                kwin = k_ref[0, pl.ds(start, D_KEYS), c * LANES:(c + 1) * LANES]
                s = _dot_nt(qh, kwin) + tab_ref[hd, 0]
                p = jnp.exp2(s - jnp.max(s, axis=-1, keepdims=True)).astype(BF16)
                o = _add_head(o, p, _place_head(vq * quarters[i], i), i, keep)
            o_ref[0, rows, g * QUAD:(g + 1) * QUAD] = o.astype(BF16)


def _attn_d(q, k, v, tab, layer):
    b = q.shape[0]

    def kind(sub):
        def index(i, j):
            blk = j * D_SUBS + sub
            return (layer, jnp.where(blk == 0, 0, jnp.where(blk == D_BLOCKS - 1, 2, 1)), 0, 0)
        return index

    return pl.pallas_call(
        _attn_d_kernel,
        grid=(b, D_BLOCKS // D_SUBS),
        in_specs=[
            pl.BlockSpec((1, D_SUBS * D_Q, 512), lambda i, j: (i, j, 0)),
            pl.BlockSpec((1, SEQ, 512), lambda i, j: (i, 0, 0)),
            pl.BlockSpec((1, SEQ, 512), lambda i, j: (i, 0, 0)),
            pl.BlockSpec((D_HEADS, 1, D_Q, D_KEYS), kind(0)),
            pl.BlockSpec((D_HEADS, 1, D_Q, D_KEYS), kind(1)),
        ],
        out_specs=pl.BlockSpec((1, D_SUBS * D_Q, 512), lambda i, j: (i, j, 0)),
        out_shape=jax.ShapeDtypeStruct((b, SEQ, 512), BF16),
        compiler_params=pltpu.CompilerParams(
            dimension_semantics=("arbitrary", "arbitrary"), vmem_limit_bytes=VMEM_LIMIT),
        name="attn_d",
    )(q, k, v, tab, tab)


def _norm_proj_kernel(x_ref, g_ref, w_ref, o_ref):
    h = _rms_rows(x_ref[...], g_ref[...]).astype(BF16)
    o_ref[...] = _dot(h, w_ref[...]).astype(BF16)


def _norm_proj(x, g, w):
    t = x.shape[0]
    depth, _, n = w.shape
    return pl.pallas_call(
        _norm_proj_kernel,
        grid=(depth, t // TM),
        in_specs=[
            pl.BlockSpec((TM, D_MODEL), lambda l, i: (i, 0)),
            pl.BlockSpec((None, 1, D_MODEL), lambda l, i: (l, 0, 0)),
            pl.BlockSpec((None, D_MODEL, n), lambda l, i: (l, 0, 0)),
        ],
        out_specs=pl.BlockSpec((None, TM, n), lambda l, i: (l, i, 0)),
        out_shape=jax.ShapeDtypeStruct((depth, t, n), BF16),
        compiler_params=pltpu.CompilerParams(
            dimension_semantics=("arbitrary", "arbitrary"), vmem_limit_bytes=VMEM_LIMIT),
        name="mem_kv_proj",
    )(x, g, w)


def _cross_block(x, a1, a2, wo1_ref, wo2_ref, g_ref, wq_ref, kv_ref, wxo_ref):
    x1 = x + _dot(a1, wo1_ref[...]) + _dot(a2, wo2_ref[...])
    h = _rms_rows(x1, g_ref[...]).astype(BF16)
    q = (_dot(h, wq_ref[...]) * (X_HEAD_DIM ** -0.5 * LOG2E)).astype(BF16)
    heads = []
    for hd in range(X_HEADS):
        sl = slice(hd * LANES, (hd + 1) * LANES)
        k = kv_ref[0, :, sl]
        v = kv_ref[0, :, X_HEADS * LANES + hd * LANES:X_HEADS * LANES + (hd + 1) * LANES]
        p, l, _ = _softmax2_parts(_dot_nt(q[:, sl], k))
        heads.append((_dot(p, v) * (1.0 / l)).astype(BF16))
    return x1 + _dot(jnp.concatenate(heads, axis=1), wxo_ref[...])


def _ffn_block(x, g_ref, wgu_ref, wd_ref):
    h = _rms_rows(x, g_ref[...]).astype(BF16)
    acc = x
    for c in range(D_FF // FF_CHUNK):
        lo = c * FF_CHUNK
        gate = _dot(h, wgu_ref[:, lo:lo + FF_CHUNK])
        up = _dot(h, wgu_ref[:, D_FF + lo:D_FF + lo + FF_CHUNK])
        act = (gate * jax.nn.sigmoid(gate) * up).astype(BF16)
        acc = acc + _dot(act, wd_ref[lo:lo + FF_CHUNK, :])
    return acc


def _tail_kernel(x_ref, a1_ref, a2_ref, wo1_ref, wo2_ref, gx_ref, wq_ref, kv_ref, wxo_ref,
                 gf_ref, wgu_ref, wd_ref, gfin_ref, o_ref, *, final_norm):
    for sub in range(TAIL_TM // TAIL_SUB):
        rows = slice(sub * TAIL_SUB, (sub + 1) * TAIL_SUB)
        x2 = _cross_block(x_ref[rows, :], a1_ref[rows, :], a2_ref[rows, :],
                          wo1_ref, wo2_ref, gx_ref, wq_ref, kv_ref, wxo_ref)
        x3 = _ffn_block(x2, gf_ref, wgu_ref, wd_ref)
        if final_norm:
            x3 = _rms_rows(x3, gfin_ref[...])
        o_ref[rows, :] = x3


def _layer_tail(x, a1, a2, w_out, mix_layer, gx, wq, kv, wxo, gf, wgu, wd, layer, gfin,
                final_norm):
    t = x.shape[0]
    per_b = SEQ // TAIL_TM
    row = lambda i: (i, 0)
    const = lambda i: (0, 0)
    return pl.pallas_call(
        functools.partial(_tail_kernel, final_norm=final_norm),
        grid=(t // TAIL_TM,),
        in_specs=[
            pl.BlockSpec((TAIL_TM, D_MODEL), row),
            pl.BlockSpec((TAIL_TM, 512), row),
            pl.BlockSpec((TAIL_TM, 512), row),
            _layer_spec(w_out, mix_layer, rows=512, row_block=0),
            _layer_spec(w_out, mix_layer, rows=512, row_block=1),
            pl.BlockSpec((1, D_MODEL), const),
            _layer_spec(wq, layer),
            pl.BlockSpec((None, 1, MEM_LEN, 2 * X_HEADS * X_HEAD_DIM),
                         lambda i: (layer, i // per_b, 0, 0)),
            _layer_spec(wxo, layer),
            pl.BlockSpec((1, D_MODEL), const),
            _layer_spec(wgu, layer),
            _layer_spec(wd, layer),
            pl.BlockSpec((1, D_MODEL), const),
        ],
        out_specs=pl.BlockSpec((TAIL_TM, D_MODEL), row),
        out_shape=jax.ShapeDtypeStruct((t, D_MODEL), F32),
        compiler_params=pltpu.CompilerParams(
            dimension_semantics=("arbitrary",), vmem_limit_bytes=VMEM_LIMIT),
        name="tail_final" if final_norm else "tail",
    )(x, a1, a2, w_out, w_out, gx, wq, kv, wxo, gf, wgu, wd, gfin)


def _rope_angles(pos, dim):
    inv = ROPE_THETA ** (-jnp.arange(0, dim, 2, dtype=F32) / dim)
    return pos.astype(F32)[:, None] * inv[None, :]


def _head64_tables(ang, scale):
    d = np.arange(LANES) % HEAD_DIM
    sign = np.where(d < HEAD_DIM // 2, -1.0, 1.0).astype(np.float32)
    a = ang[:, d % (HEAD_DIM // 2)]
    return jnp.cos(a) * scale, jnp.sin(a) * sign[None, :] * scale


def _latent_tables(ang, scale):
    lane = np.arange(LANES)
    in_rope = (lane >= C_NOPE) & (lane < C_NOPE + C_ROPE)
    a = ang[:, (lane - C_NOPE) % (C_ROPE // 2)]
    sign = np.where(lane < C_NOPE + C_ROPE // 2, -1.0, 1.0).astype(np.float32)
    cos = jnp.where(in_rope[None, :], jnp.cos(a), 1.0) * scale
    sin = jnp.where(in_rope[None, :], jnp.sin(a) * sign[None, :], 0.0) * scale
    return cos, sin


def _pad_cols(w, width):
    return jnp.pad(w, ((0, 0), (0, width - w.shape[1])))


def kernel(x, mem, g_mix, w_in_ab, g_qa, g_ka, sink_b, w_out_ab, w_in_cd, g_cq, g_ckv,
           w_uq, w_ukv, rpb_d, w_out_cd, g_xq, g_mem, w_xq, w_xkv, w_xo, g_ffn,
           w_gate_up, w_down, g_final):
    b, s, d = x.shape
    assert (s, d) == (SEQ, D_MODEL) and mem.shape == (b, MEM_LEN, D_MODEL)
    depth = g_mix.shape[0]
    t = b * s

    pos = jnp.arange(s)
    ang_1d = _rope_angles(pos, HEAD_DIM)
    ang_2d = jnp.concatenate([_rope_angles(pos // GRID_W, HEAD_DIM // 2),
                              _rope_angles(pos % GRID_W, HEAD_DIM // 2)], axis=-1)
    ang_c = _rope_angles(pos, C_ROPE)
    q_scale = HEAD_DIM ** -0.5 * LOG2E
    tabs_ab = jnp.stack([*_head64_tables(ang_2d, q_scale), *_head64_tables(ang_2d, 1.0),
                         *_head64_tables(ang_1d, q_scale), *_head64_tables(ang_1d, 1.0)])
    tabs_cd = jnp.stack([*_latent_tables(ang_c, (C_NOPE + C_ROPE) ** -0.5 * LOG2E),
                         *_latent_tables(ang_c, 1.0)])
    lane = np.arange(LANES)
    block_ones = jnp.asarray(lane[:, None] // HEAD_DIM == lane[None, :] // HEAD_DIM, BF16)

    row2 = lambda v: v.reshape(1, -1).astype(F32)
    xf = x.reshape(t, d)
    mem_f = mem.reshape(b * MEM_LEN, d)
    w_in_ab_h, w_out_ab_h, w_out_cd_h = (w.astype(BF16) for w in (w_in_ab, w_out_ab, w_out_cd))
    w_xq_h, w_xkv_h, w_xo_h = (w.astype(BF16) for w in (w_xq, w_xkv, w_xo))
    w_gate_up_h, w_down_h = w_gate_up.astype(BF16), w_down.astype(BF16)
    kv_all = _norm_proj(mem_f, g_mem.reshape(depth, 1, d).astype(F32), w_xkv_h)
    kv_all = kv_all.reshape(depth, b, MEM_LEN, 2 * X_HEADS * X_HEAD_DIM)
    bias_tabs = _bias_table(rpb_d.reshape(-1).astype(F32), rpb_d.shape[0])

    for i in range(depth):
        j = i // 2
        if i % 2 == 0:
            gq = row2(jnp.tile(g_qa[j], 2))
            gk = row2(jnp.tile(g_ka[j], 2))
            qa, ka, va, qb, kb, vb = _inproj_ab(
                xf, row2(g_mix[i]), w_in_ab_h, j, gq, gk, block_ones, tabs_ab)
            sh = lambda a: a.reshape(b, s, a.shape[-1])
            o1, o2 = _attn_ab(sink_b[j].astype(F32), sh(qa), sh(ka), sh(va),
                              sh(qb), sh(kb), sh(vb))
            o1, o2 = o1.reshape(t, 512), o2.reshape(t, 512)
            w_out = w_out_ab_h
        else:
            w_in = w_in_cd[j]
            w_in = jnp.concatenate(
                [w_in[:, :C_Q_RANK + C_KV_RANK],
                 _pad_cols(jnp.pad(w_in[:, 384:416], ((0, 0), (C_NOPE, 0))), LANES),
                 w_in[:, 416:]], axis=1).astype(BF16)
            wuq = w_uq[j].reshape(C_Q_RANK, C_HEADS, C_NOPE + C_ROPE)
            wuq = jnp.pad(wuq, ((0, 0), (0, 0), (0, LANES - C_NOPE - C_ROPE)))
            wuq = wuq.reshape(C_Q_RANK, C_HEADS * LANES).astype(BF16)
            wukv = w_ukv[j].reshape(C_KV_RANK, C_HEADS, C_NOPE + C_V)
            wuk = jnp.pad(wukv[:, :, :C_NOPE], ((0, 0), (0, 0), (0, LANES - C_NOPE)))
            wuk = wuk.reshape(C_KV_RANK, C_HEADS * LANES).astype(BF16)
            wuv = wukv[:, :, C_NOPE:].reshape(C_KV_RANK, C_HEADS * C_V).astype(BF16)
            qc, kc, vc, qd, kd, vd = _inproj_cd(
                xf, row2(g_mix[i]), w_in, row2(g_cq[j]), row2(g_ckv[j]), wuq, wuk, wuv, tabs_cd)
            sh = lambda a: a.reshape(b, s, a.shape[-1])
            o1 = _attn_c(sh(qc), sh(kc), sh(vc)).reshape(t, 512)
            o2 = _attn_d(sh(qd), sh(kd), sh(vd), bias_tabs, j).reshape(t, 512)
            w_out = w_out_cd_h

        xf = _layer_tail(xf, o1, o2, w_out, j, row2(g_xq[i]), w_xq_h, kv_all, w_xo_h,
                         row2(g_ffn[i]), w_gate_up_h, w_down_h, i, row2(g_final),
                         final_norm=(i == depth - 1))
    return xf.reshape(b, s, d)
```

```python
import functools

import numpy as np
import jax
import jax.numpy as jnp
from jax import lax
from jax.experimental import pallas as pl
from jax.experimental.pallas import tpu as pltpu

D_MODEL = 1024
SEQ = 2048
HEAD_DIM = 64
GRID_W = 64
MEM_LEN = 256
ROPE_THETA = 10000.0
EPS = 1e-6
NEG = -1e30

A_HEADS = 8
B_HEADS = 8
B_WINDOW = 128
C_HEADS = 8
C_Q_RANK = 256
C_KV_RANK = 128
C_NOPE = 64
C_ROPE = 32
C_V = 64
D_HEADS = 8
D_WIN_R = 8
D_WIN_C = 16
X_HEADS = 4
X_HEAD_DIM = 128
D_FF = 2816

LANES = 128
QUAD = 256
LOG2E = 1.4426950408889634
TM = 1024
TM_SUB = 128
CD_SUB = 256
TAIL_TM = 1024
TAIL_SUB = 512
TQ_DENSE = 512
C_TQ = 1024
TQ_SUB = 256
FF_CHUNK = 256
VMEM_LIMIT = 56 * 1024 * 1024

F32 = jnp.float32
BF16 = jnp.bfloat16

_NT = (((1,), (1,)), ((), ()))


def _layer_spec(w, layer, rows=None, row_block=0):
    k = w.shape[1] if rows is None else rows
    return pl.BlockSpec((None, k, w.shape[2]), lambda *_: (layer, row_block, 0),
                        pipeline_mode=pl.Buffered(1))


def _dot(a, b):
    return jnp.dot(a, b, preferred_element_type=F32)


def _dot_nt(a, b):
    return lax.dot_general(a, b, _NT, preferred_element_type=F32)


def _rms_rows(xf, g):
    return xf * lax.rsqrt(jnp.mean(xf * xf, axis=-1, keepdims=True) + EPS) * g


def _lane_iota(shape):
    return lax.broadcasted_iota(jnp.int32, shape, len(shape) - 1)


def _rope_chunk(x, cos, sin_signed, first_mask, half):
    rot = jnp.where(first_mask, pltpu.roll(x, LANES - half, 1), pltpu.roll(x, half, 1))
    return x * cos + rot * sin_signed


def _half_masks(dtype):
    lane = _lane_iota((1, LANES))
    return (lane < 64).astype(dtype), (lane >= 64).astype(dtype)


def _quarter_masks(dtype):
    lane = _lane_iota((1, QUAD))
    return [((lane >= HEAD_DIM * i) & (lane < HEAD_DIM * (i + 1))).astype(dtype) for i in range(4)]


def _sum_col(i):
    return HEAD_DIM * ((i + 1) % 4)


def _place_head(v_masked, i):
    lane = _lane_iota(v_masked.shape)
    return jnp.where(lane == _sum_col(i), 1.0, v_masked).astype(BF16)


def _softmax2_parts(s):
    m = jnp.max(s, axis=-1, keepdims=True)
    p = jnp.exp2(s - m)
    return p.astype(BF16), jnp.sum(p, axis=-1, keepdims=True), m


def _add_head(o, p, w, i, keep, l_extra=None):
    raw = _dot(p, w)
    l = raw[:, _sum_col(i):_sum_col(i) + 1]
    if l_extra is not None:
        l = l + l_extra
    term = raw * (1.0 / l) * keep[i]
    return term if o is None else o + term


def _inproj_ab_kernel(x_ref, g_ref, w_ref, gq_ref, gk_ref, bd_ref, tab_ref,
                      qa_ref, ka_ref, va_ref, qb_ref, kb_ref, vb_ref):
    lane = _lane_iota((TM_SUB, LANES))
    first = (lane & 63) < 32
    lo = lane < 64
    bd = bd_ref[...]

    def head_norm(zc, gain):
        ss = _dot((zc * zc).astype(BF16), bd)
        return zc * lax.rsqrt(ss * (1.0 / HEAD_DIM) + EPS) * gain

    def dup(c):
        r = pltpu.roll(c, 64, 1)
        return jnp.where(lo, c, r), jnp.where(lo, r, c)

    for sub in range(TM // TM_SUB):
        rows = slice(sub * TM_SUB, (sub + 1) * TM_SUB)
        h = _rms_rows(x_ref[rows, :], g_ref[...]).astype(BF16)
        z = _dot(h, w_ref[...])

        def chunk(i, z=z):
            return z[:, i * LANES:(i + 1) * LANES]

        def tab(i, rows=rows):
            return tab_ref[i, rows, :]

        for c in range(4):
            n = head_norm(chunk(c), gq_ref[...])
            qa_ref[rows, c * LANES:(c + 1) * LANES] = _rope_chunk(
                n, tab(0), tab(1), first, 32).astype(BF16)
        k = _rope_chunk(head_norm(chunk(4), gk_ref[...]), tab(2), tab(3), first, 32)
        k0, k1 = dup(k)
        ka_ref[rows, 0:LANES] = k0.astype(BF16)
        ka_ref[rows, LANES:2 * LANES] = k1.astype(BF16)
        v0, v1 = dup(chunk(5))
        va_ref[rows, 0:LANES] = v0.astype(BF16)
        va_ref[rows, LANES:2 * LANES] = v1.astype(BF16)
        for c in range(4):
            qb_ref[rows, c * LANES:(c + 1) * LANES] = _rope_chunk(
                chunk(6 + c), tab(4), tab(5), first, 32).astype(BF16)
        k = _rope_chunk(chunk(10), tab(6), tab(7), first, 32)
        k0, k1 = dup(k)
        kb_ref[rows, 0:LANES] = k0.astype(BF16)
        kb_ref[rows, LANES:2 * LANES] = k1.astype(BF16)
        v0, v1 = dup(chunk(11))
        vb_ref[rows, 0:LANES] = v0.astype(BF16)
        vb_ref[rows, LANES:2 * LANES] = v1.astype(BF16)


def _inproj_ab(x, g, w, layer, gq, gk, bd, tabs):
    t = x.shape[0]
    n_pos = SEQ // TM
    row = lambda i: (i, 0)
    const = lambda i: (0, 0)
    outs = [(t, 512), (t, 256), (t, 256), (t, 512), (t, 256), (t, 256)]
    return pl.pallas_call(
        _inproj_ab_kernel,
        grid=(t // TM,),
        in_specs=[
            pl.BlockSpec((TM, D_MODEL), row),
            pl.BlockSpec((1, D_MODEL), const),
            _layer_spec(w, layer),
            pl.BlockSpec((1, LANES), const),
            pl.BlockSpec((1, LANES), const),
            pl.BlockSpec((LANES, LANES), const),
            pl.BlockSpec((8, TM, LANES), lambda i: (0, i % n_pos, 0)),
        ],
        out_specs=[pl.BlockSpec((TM, n), row) for _, n in outs],
        out_shape=[jax.ShapeDtypeStruct(s, BF16) for s in outs],
        compiler_params=pltpu.CompilerParams(
            dimension_semantics=("arbitrary",), vmem_limit_bytes=VMEM_LIMIT),
        name="inproj_ab",
    )(x, g, w, gq, gk, bd, tabs)


def _place_shared_v(v2, masks):
    zero = jnp.zeros_like(v2)
    lo, hi = v2 * masks[0], v2 * masks[1]
    return [_place_head(jnp.concatenate(parts, axis=1), i)
            for i, parts in enumerate(((lo, zero), (hi, zero), (zero, lo), (zero, hi)))]


def _attn_a_units(q_ref, k_ref, o_ref, w_ref, masks):
    def unit(kvh, sub):
        rows = slice(sub * TQ_SUB, (sub + 1) * TQ_SUB)
        k2 = k_ref[0, :, kvh * LANES:(kvh + 1) * LANES]
        keep = _quarter_masks(F32)
        probs = []
        for i in range(4):
            c = 2 * kvh + i // 2
            qc = q_ref[0, rows, c * LANES:(c + 1) * LANES]
            s = _dot_nt(qc * masks[i % 2], k2)
            probs.append(jnp.exp2(s - jnp.max(s, axis=-1, keepdims=True)).astype(BF16))
        o = None
        for i, p in enumerate(probs):
            o = _add_head(o, p, w_ref[kvh, i], i, keep)
        o_ref[0, rows, kvh * QUAD:(kvh + 1) * QUAD] = o.astype(BF16)

    return [functools.partial(unit, kvh, sub)
            for kvh in range(2) for sub in range(TQ_DENSE // TQ_SUB)]


B_BLOCK = 256
B_KEYS = B_BLOCK + 2 * B_WINDOW


def _attn_b_units(sink_ref, q_ref, k_ref, v_ref, o_ref, masks):
    def unit(sub, kvh):
        n = pl.program_id(1) * (TQ_DENSE // B_BLOCK) + sub
        rows = slice(sub * B_BLOCK, (sub + 1) * B_BLOCK)
        start = pl.multiple_of(jnp.clip(n * B_BLOCK - B_WINDOW, 0, SEQ - B_KEYS), B_WINDOW)
        q_pos = n * B_BLOCK + lax.broadcasted_iota(jnp.int32, (B_BLOCK, B_KEYS), 0)
        k_pos = start + lax.broadcasted_iota(jnp.int32, (B_BLOCK, B_KEYS), 1)
        delta = k_pos - q_pos
        valid = (delta <= B_WINDOW) & (delta >= -B_WINDOW)
        k2 = k_ref[0, pl.ds(start, B_KEYS), kvh * LANES:(kvh + 1) * LANES]
        ws = _place_shared_v(v_ref[0, pl.ds(start, B_KEYS), kvh * LANES:(kvh + 1) * LANES], masks)
        keep = _quarter_masks(F32)
        heads = []
        for i in range(4):
            c = 2 * kvh + i // 2
            qc = q_ref[0, rows, c * LANES:(c + 1) * LANES]
            sink = sink_ref[4 * kvh + i] * LOG2E
            s = jnp.where(valid, _dot_nt(qc * masks[i % 2], k2), NEG)
            m = jnp.maximum(jnp.max(s, axis=-1, keepdims=True), sink)
            heads.append((jnp.exp2(s - m).astype(BF16), jnp.exp2(sink - m)))
        o = None
        for i, (p, sink_term) in enumerate(heads):
            o = _add_head(o, p, ws[i], i, keep, l_extra=sink_term)
        o_ref[0, rows, kvh * QUAD:(kvh + 1) * QUAD] = o.astype(BF16)

    return [functools.partial(unit, sub, kvh)
            for sub in range(TQ_DENSE // B_BLOCK) for kvh in range(2)]


def _attn_ab_kernel(sink_ref, qa_ref, ka_ref, va_ref, qb_ref, kb_ref, vb_ref,
                    oa_ref, ob_ref, w_ref):
    masks = _half_masks(BF16)

    @pl.when(pl.program_id(1) == 0)
    def _():
        for kvh in range(2):
            placed = _place_shared_v(va_ref[0, :, kvh * LANES:(kvh + 1) * LANES], masks)
            for i in range(4):
                w_ref[kvh, i] = placed[i]

    a_units = _attn_a_units(qa_ref, ka_ref, oa_ref, w_ref, masks)
    b_units = _attn_b_units(sink_ref, qb_ref, kb_ref, vb_ref, ob_ref, masks)
    assert len(a_units) == len(b_units)
    for a_unit, b_unit in zip(a_units, b_units):
        a_unit()
        b_unit()


def _attn_ab(sink, qa, ka, va, qb, kb, vb):
    b = qa.shape[0]
    tile = lambda i, j: (i, j, 0)
    whole = lambda i, j: (i, 0, 0)
    out = jax.ShapeDtypeStruct((b, SEQ, 512), BF16)
    return pl.pallas_call(
        _attn_ab_kernel,
        grid=(b, SEQ // TQ_DENSE),
        in_specs=[
            pl.BlockSpec(memory_space=pltpu.SMEM),
            pl.BlockSpec((1, TQ_DENSE, 512), tile),
            pl.BlockSpec((1, SEQ, 256), whole),
            pl.BlockSpec((1, SEQ, 256), whole),
            pl.BlockSpec((1, TQ_DENSE, 512), tile),
            pl.BlockSpec((1, SEQ, 256), whole),
            pl.BlockSpec((1, SEQ, 256), whole),
        ],
        out_specs=[pl.BlockSpec((1, TQ_DENSE, 512), tile)] * 2,
        out_shape=[out, out],
        scratch_shapes=[pltpu.VMEM((2, 4, SEQ, QUAD), BF16)],
        compiler_params=pltpu.CompilerParams(
            dimension_semantics=("arbitrary", "arbitrary"), vmem_limit_bytes=VMEM_LIMIT),
        name="attn_ab",
    )(sink, qa, ka, va, qb, kb, vb)


def _inproj_cd_kernel(x_ref, g_ref, w_ref, gcq_ref, gckv_ref, wuq_ref, wuk_ref, wuv_ref,
                      tab_ref, qc_ref, kc_ref, vc_ref, qd_ref, kd_ref, vd_ref):
    lane = _lane_iota((CD_SUB, LANES))
    first = lane < 80
    for sub in range(TM // CD_SUB):
        rows = slice(sub * CD_SUB, (sub + 1) * CD_SUB)
        h = _rms_rows(x_ref[rows, :], g_ref[...]).astype(BF16)
        z = _dot(h, w_ref[...])
        cq = _rms_rows(z[:, 0:256], gcq_ref[...]).astype(BF16)
        ckv = _rms_rows(z[:, 256:384], gckv_ref[...]).astype(BF16)
        q = _dot(cq, wuq_ref[...])
        kn = _dot(ckv, wuk_ref[...])
        tabs = [tab_ref[i, rows, :] for i in range(4)]
        kr = _rope_chunk(z[:, 384:512], tabs[2], tabs[3], first, 16)
        for hd in range(C_HEADS):
            sl = slice(hd * LANES, (hd + 1) * LANES)
            qc_ref[rows, sl] = _rope_chunk(q[:, sl], tabs[0], tabs[1], first, 16).astype(BF16)
            kc_ref[rows, sl] = (kn[:, sl] + kr).astype(BF16)
        vc_ref[rows, :] = _dot(ckv, wuv_ref[...]).astype(BF16)
        qd_ref[rows, :] = (z[:, 512:1024] * (HEAD_DIM ** -0.5 * LOG2E)).astype(BF16)
        kd_ref[rows, :] = z[:, 1024:1536].astype(BF16)
        vd_ref[rows, :] = z[:, 1536:2048].astype(BF16)


def _inproj_cd(x, g, w, gcq, gckv, wuq, wuk, wuv, tabs):
    t = x.shape[0]
    n_pos = SEQ // TM
    row = lambda i: (i, 0)
    const = lambda i: (0, 0)
    outs = [(t, 1024), (t, 1024), (t, 512), (t, 512), (t, 512), (t, 512)]
    return pl.pallas_call(
        _inproj_cd_kernel,
        grid=(t // TM,),
        in_specs=[
            pl.BlockSpec((TM, D_MODEL), row),
            pl.BlockSpec((1, D_MODEL), const),
            pl.BlockSpec(w.shape, const),
            pl.BlockSpec((1, C_Q_RANK), const),
            pl.BlockSpec((1, C_KV_RANK), const),
            pl.BlockSpec(wuq.shape, const),
            pl.BlockSpec(wuk.shape, const),
            pl.BlockSpec(wuv.shape, const),
            pl.BlockSpec((4, TM, LANES), lambda i: (0, i % n_pos, 0)),
        ],
        out_specs=[pl.BlockSpec((TM, n), row) for _, n in outs],
        out_shape=[jax.ShapeDtypeStruct(s, BF16) for s in outs],
        compiler_params=pltpu.CompilerParams(
            dimension_semantics=("arbitrary",), vmem_limit_bytes=VMEM_LIMIT),
        name="inproj_cd",
    )(x, g, w, gcq, gckv, wuq, wuk, wuv, tabs)


def _attn_c_kernel(q_ref, k_ref, v_ref, o_ref, w_ref):
    @pl.when(pl.program_id(1) == 0)
    def _():
        masks = _quarter_masks(BF16)
        for g in range(2):
            vq = v_ref[0, :, g * QUAD:(g + 1) * QUAD]
            for i in range(4):
                w_ref[g, i] = _place_head(vq * masks[i], i)

    keep = _quarter_masks(F32)
    for g in range(2):
        for sub in range(C_TQ // TQ_SUB):
            rows = slice(sub * TQ_SUB, (sub + 1) * TQ_SUB)
            o = None
            for i in range(4):
                hd = 4 * g + i
                s = _dot_nt(q_ref[0, rows, hd * LANES:(hd + 1) * LANES],
                            k_ref[0, :, hd * LANES:(hd + 1) * LANES])
                p = jnp.exp2(s - jnp.max(s, axis=-1, keepdims=True)).astype(BF16)
                o = _add_head(o, p, w_ref[g, i], i, keep)
            o_ref[0, rows, g * QUAD:(g + 1) * QUAD] = o.astype(BF16)


def _attn_c(q, k, v):
    b = q.shape[0]
    return pl.pallas_call(
        _attn_c_kernel,
        grid=(b, SEQ // C_TQ),
        in_specs=[
            pl.BlockSpec((1, C_TQ, 1024), lambda i, j: (i, j, 0)),
            pl.BlockSpec((1, SEQ, 1024), lambda i, j: (i, 0, 0)),
            pl.BlockSpec((1, SEQ, 512), lambda i, j: (i, 0, 0)),
        ],
        out_specs=pl.BlockSpec((1, C_TQ, 512), lambda i, j: (i, j, 0)),
        out_shape=jax.ShapeDtypeStruct((b, SEQ, 512), BF16),
        scratch_shapes=[pltpu.VMEM((2, 4, SEQ, QUAD), BF16)],
        compiler_params=pltpu.CompilerParams(
            dimension_semantics=("arbitrary", "arbitrary"), vmem_limit_bytes=VMEM_LIMIT),
        name="attn_c",
    )(q, k, v)


N_ROWS = SEQ // GRID_W
N_DR = 2 * D_WIN_R - 1
N_DC = 2 * D_WIN_C - 1
D_QROWS = 4
D_WROWS = D_QROWS + D_WIN_R
D_Q = D_QROWS * GRID_W
D_KEYS = D_WROWS * GRID_W
D_BLOCKS = N_ROWS // D_QROWS
D_SUBS = 4
D_KINDS = 3


def _d_window_row(kind, a, w):
    if kind == 0:
        return w < D_WIN_R, w - a + (D_WIN_R - 1)
    if kind == 1:
        return a <= w < a + D_WIN_R, w - a + (D_WIN_R - 1) - D_WIN_R // 2
    lead = D_WROWS - D_WIN_R
    return w >= lead, w - a - 1


def _bias_table_kernel(rpb_ref, tab_ref):
    h = pl.program_id(0)
    shape = (GRID_W, LANES)
    qc = lax.broadcasted_iota(jnp.int32, shape, 0)
    lane = lax.broadcasted_iota(jnp.int32, shape, 1)
    kc = lane & (GRID_W - 1)
    lower = lane < GRID_W
    idx = jnp.clip(kc - qc + (D_WIN_C - 1), 0, N_DC - 1)
    c0 = jnp.clip(qc - D_WIN_C // 2, 0, GRID_W - D_WIN_C)
    col_ok = (kc >= c0) & (kc < c0 + D_WIN_C)
    base = h * (N_DR * N_DC)
    tiles = []
    for d in range(N_DR):
        acc = jnp.zeros(shape, F32)
        for j in range(N_DC):
            acc = jnp.where(idx == j, rpb_ref[base + d * N_DC + j], acc)
        tiles.append(jnp.where(col_ok, acc * LOG2E, NEG))
    neg = jnp.full(shape, NEG, F32)
    for kind in range(D_KINDS):
        for a in range(D_QROWS):
            for wp in range(D_WROWS // 2):
                ok_lo, d_lo = _d_window_row(kind, a, 2 * wp)
                ok_hi, d_hi = _d_window_row(kind, a, 2 * wp + 1)
                t_lo = tiles[d_lo] if ok_lo else neg
                t_hi = tiles[d_hi] if ok_hi else neg
                tab_ref[0, kind, a * GRID_W:(a + 1) * GRID_W, wp * LANES:(wp + 1) * LANES] = (
                    jnp.where(lower, t_lo, t_hi))


def _bias_table(rpb_flat, n_layers):
    return pl.pallas_call(
        _bias_table_kernel,
        grid=(n_layers * D_HEADS,),
        in_specs=[pl.BlockSpec(memory_space=pltpu.SMEM)],
        out_specs=pl.BlockSpec((1, D_KINDS, D_Q, D_KEYS), lambda i: (i, 0, 0, 0)),
        out_shape=jax.ShapeDtypeStruct((n_layers * D_HEADS, D_KINDS, D_Q, D_KEYS), F32),
        compiler_params=pltpu.CompilerParams(dimension_semantics=("arbitrary",)),
        name="bias_table",
    )(rpb_flat)


def _attn_d_kernel(q_ref, k_ref, v_ref, tab_ref, o_ref):
    halves = _half_masks(BF16)
    quarters = _quarter_masks(BF16)
    keep = _quarter_masks(F32)

    for sub in range(D_SUBS):
        blk = pl.program_id(1) * D_SUBS + sub
        rows = slice(sub * D_Q, (sub + 1) * D_Q)
        w0 = jnp.clip(blk * D_QROWS - D_WIN_R // 2, 0, N_ROWS - D_WROWS)
        start = pl.multiple_of(w0 * GRID_W, D_Q)
        kind = jnp.where(blk == 0, 0, jnp.where(blk == D_BLOCKS - 1, 2, 1))
        for g in range(2):
            vq = v_ref[0, pl.ds(start, D_KEYS), g * QUAD:(g + 1) * QUAD]
            o = None
            for i in range(4):
                hd = 4 * g + i
                c = hd // 2
                qh = q_ref[0, rows, c * LANES:(c + 1) * LANES] * halves[i % 2]
                kwin = k_ref[0, pl.ds(start, D_KEYS), c * LANES:(c + 1) * LANES]
                s = _dot_nt(qh, kwin) + tab_ref[hd, kind]
                p = jnp.exp2(s - jnp.max(s, axis=-1, keepdims=True)).astype(BF16)
                o = _add_head(o, p, _place_head(vq * quarters[i], i), i, keep)
            o_ref[0, rows, g * QUAD:(g + 1) * QUAD] = o.astype(BF16)


def _attn_d(q, k, v, tab, layer):
    b = q.shape[0]
    return pl.pallas_call(
        _attn_d_kernel,
        grid=(b, D_BLOCKS // D_SUBS),
        in_specs=[
            pl.BlockSpec((1, D_SUBS * D_Q, 512), lambda i, j: (i, j, 0)),
            pl.BlockSpec((1, SEQ, 512), lambda i, j: (i, 0, 0)),
            pl.BlockSpec((1, SEQ, 512), lambda i, j: (i, 0, 0)),
            pl.BlockSpec((D_HEADS, D_KINDS, D_Q, D_KEYS), lambda i, j: (layer, 0, 0, 0),
                         pipeline_mode=pl.Buffered(1)),
        ],
        out_specs=pl.BlockSpec((1, D_SUBS * D_Q, 512), lambda i, j: (i, j, 0)),
        out_shape=jax.ShapeDtypeStruct((b, SEQ, 512), BF16),
        compiler_params=pltpu.CompilerParams(
            dimension_semantics=("arbitrary", "arbitrary"), vmem_limit_bytes=VMEM_LIMIT),
        name="attn_d",
    )(q, k, v, tab)


def _norm_proj_kernel(x_ref, g_ref, w_ref, o_ref):
    h = _rms_rows(x_ref[...], g_ref[...]).astype(BF16)
    o_ref[...] = _dot(h, w_ref[...]).astype(BF16)


def _norm_proj(x, g, w):
    t = x.shape[0]
    depth, _, n = w.shape
    return pl.pallas_call(
        _norm_proj_kernel,
        grid=(depth, t // TM),
        in_specs=[
            pl.BlockSpec((TM, D_MODEL), lambda l, i: (i, 0)),
            pl.BlockSpec((None, 1, D_MODEL), lambda l, i: (l, 0, 0)),
            pl.BlockSpec((None, D_MODEL, n), lambda l, i: (l, 0, 0)),
        ],
        out_specs=pl.BlockSpec((None, TM, n), lambda l, i: (l, i, 0)),
        out_shape=jax.ShapeDtypeStruct((depth, t, n), BF16),
        compiler_params=pltpu.CompilerParams(
            dimension_semantics=("arbitrary", "arbitrary"), vmem_limit_bytes=VMEM_LIMIT),
        name="mem_kv_proj",
    )(x, g, w)


def _cross_block(x, a1, a2, wo1_ref, wo2_ref, g_ref, wq_ref, kv_ref, wxo_ref):
    x1 = x + _dot(a1, wo1_ref[...]) + _dot(a2, wo2_ref[...])
    h = _rms_rows(x1, g_ref[...]).astype(BF16)
    q = (_dot(h, wq_ref[...]) * (X_HEAD_DIM ** -0.5 * LOG2E)).astype(BF16)
    heads = []
    for hd in range(X_HEADS):
        sl = slice(hd * LANES, (hd + 1) * LANES)
        k = kv_ref[0, :, sl]
        v = kv_ref[0, :, X_HEADS * LANES + hd * LANES:X_HEADS * LANES + (hd + 1) * LANES]
        p, l, _ = _softmax2_parts(_dot_nt(q[:, sl], k))
        heads.append((_dot(p, v) * (1.0 / l)).astype(BF16))
    return x1 + _dot(jnp.concatenate(heads, axis=1), wxo_ref[...])


def _ffn_block(x, g_ref, wgu_ref, wd_ref):
    h = _rms_rows(x, g_ref[...]).astype(BF16)
    acc = x
    for c in range(D_FF // FF_CHUNK):
        lo = c * FF_CHUNK
        gate = _dot(h, wgu_ref[:, lo:lo + FF_CHUNK])
        up = _dot(h, wgu_ref[:, D_FF + lo:D_FF + lo + FF_CHUNK])
        act = (gate * jax.nn.sigmoid(gate) * up).astype(BF16)
        acc = acc + _dot(act, wd_ref[lo:lo + FF_CHUNK, :])
    return acc


def _tail_kernel(x_ref, a1_ref, a2_ref, wo1_ref, wo2_ref, gx_ref, wq_ref, kv_ref, wxo_ref,
                 gf_ref, wgu_ref, wd_ref, gfin_ref, o_ref, *, final_norm):
    for sub in range(TAIL_TM // TAIL_SUB):
        rows = slice(sub * TAIL_SUB, (sub + 1) * TAIL_SUB)
        x2 = _cross_block(x_ref[rows, :], a1_ref[rows, :], a2_ref[rows, :],
                          wo1_ref, wo2_ref, gx_ref, wq_ref, kv_ref, wxo_ref)
        x3 = _ffn_block(x2, gf_ref, wgu_ref, wd_ref)
        if final_norm:
            x3 = _rms_rows(x3, gfin_ref[...])
        o_ref[rows, :] = x3


def _layer_tail(x, a1, a2, w_out, mix_layer, gx, wq, kv, wxo, gf, wgu, wd, layer, gfin,
                final_norm):
    t = x.shape[0]
    per_b = SEQ // TAIL_TM
    row = lambda i: (i, 0)
    const = lambda i: (0, 0)
    return pl.pallas_call(
        functools.partial(_tail_kernel, final_norm=final_norm),
        grid=(t // TAIL_TM,),
        in_specs=[
            pl.BlockSpec((TAIL_TM, D_MODEL), row),
            pl.BlockSpec((TAIL_TM, 512), row),
            pl.BlockSpec((TAIL_TM, 512), row),
            _layer_spec(w_out, mix_layer, rows=512, row_block=0),
            _layer_spec(w_out, mix_layer, rows=512, row_block=1),
            pl.BlockSpec((1, D_MODEL), const),
            _layer_spec(wq, layer),
            pl.BlockSpec((None, 1, MEM_LEN, 2 * X_HEADS * X_HEAD_DIM),
                         lambda i: (layer, i // per_b, 0, 0)),
            _layer_spec(wxo, layer),
            pl.BlockSpec((1, D_MODEL), const),
            _layer_spec(wgu, layer),
            _layer_spec(wd, layer),
            pl.BlockSpec((1, D_MODEL), const),
        ],
        out_specs=pl.BlockSpec((TAIL_TM, D_MODEL), row),
        out_shape=jax.ShapeDtypeStruct((t, D_MODEL), F32),
        compiler_params=pltpu.CompilerParams(
            dimension_semantics=("arbitrary",), vmem_limit_bytes=VMEM_LIMIT),
        name="tail_final" if final_norm else "tail",
    )(x, a1, a2, w_out, w_out, gx, wq, kv, wxo, gf, wgu, wd, gfin)


def _rope_angles(pos, dim):
    inv = ROPE_THETA ** (-jnp.arange(0, dim, 2, dtype=F32) / dim)
    return pos.astype(F32)[:, None] * inv[None, :]


def _head64_tables(ang, scale):
    d = np.arange(LANES) % HEAD_DIM
    sign = np.where(d < HEAD_DIM // 2, -1.0, 1.0).astype(np.float32)
    a = ang[:, d % (HEAD_DIM // 2)]
    return jnp.cos(a) * scale, jnp.sin(a) * sign[None, :] * scale


def _latent_tables(ang, scale):
    lane = np.arange(LANES)
    in_rope = (lane >= C_NOPE) & (lane < C_NOPE + C_ROPE)
    a = ang[:, (lane - C_NOPE) % (C_ROPE // 2)]
    sign = np.where(lane < C_NOPE + C_ROPE // 2, -1.0, 1.0).astype(np.float32)
    cos = jnp.where(in_rope[None, :], jnp.cos(a), 1.0) * scale
    sin = jnp.where(in_rope[None, :], jnp.sin(a) * sign[None, :], 0.0) * scale
    return cos, sin


def _pad_cols(w, width):
    return jnp.pad(w, ((0, 0), (0, width - w.shape[1])))


def kernel(x, mem, g_mix, w_in_ab, g_qa, g_ka, sink_b, w_out_ab, w_in_cd, g_cq, g_ckv,
           w_uq, w_ukv, rpb_d, w_out_cd, g_xq, g_mem, w_xq, w_xkv, w_xo, g_ffn,
           w_gate_up, w_down, g_final):
    b, s, d = x.shape
    assert (s, d) == (SEQ, D_MODEL) and mem.shape == (b, MEM_LEN, D_MODEL)
    depth = g_mix.shape[0]
    t = b * s

    pos = jnp.arange(s)
    ang_1d = _rope_angles(pos, HEAD_DIM)
    ang_2d = jnp.concatenate([_rope_angles(pos // GRID_W, HEAD_DIM // 2),
                              _rope_angles(pos % GRID_W, HEAD_DIM // 2)], axis=-1)
    ang_c = _rope_angles(pos, C_ROPE)
    q_scale = HEAD_DIM ** -0.5 * LOG2E
    tabs_ab = jnp.stack([*_head64_tables(ang_2d, q_scale), *_head64_tables(ang_2d, 1.0),
                         *_head64_tables(ang_1d, q_scale), *_head64_tables(ang_1d, 1.0)])
    tabs_cd = jnp.stack([*_latent_tables(ang_c, (C_NOPE + C_ROPE) ** -0.5 * LOG2E),
                         *_latent_tables(ang_c, 1.0)])
    lane = np.arange(LANES)
    block_ones = jnp.asarray(lane[:, None] // HEAD_DIM == lane[None, :] // HEAD_DIM, BF16)

    row2 = lambda v: v.reshape(1, -1).astype(F32)
    xf = x.reshape(t, d)
    mem_f = mem.reshape(b * MEM_LEN, d)
    w_in_ab_h, w_out_ab_h, w_out_cd_h = (w.astype(BF16) for w in (w_in_ab, w_out_ab, w_out_cd))
    w_xq_h, w_xkv_h, w_xo_h = (w.astype(BF16) for w in (w_xq, w_xkv, w_xo))
    w_gate_up_h, w_down_h = w_gate_up.astype(BF16), w_down.astype(BF16)
    kv_all = _norm_proj(mem_f, g_mem.reshape(depth, 1, d).astype(F32), w_xkv_h)
    kv_all = kv_all.reshape(depth, b, MEM_LEN, 2 * X_HEADS * X_HEAD_DIM)
    bias_tabs = _bias_table(rpb_d.reshape(-1).astype(F32), rpb_d.shape[0])

    for i in range(depth):
        j = i // 2
        if i % 2 == 0:
            gq = row2(jnp.tile(g_qa[j], 2))
            gk = row2(jnp.tile(g_ka[j], 2))
            qa, ka, va, qb, kb, vb = _inproj_ab(
                xf, row2(g_mix[i]), w_in_ab_h, j, gq, gk, block_ones, tabs_ab)
            sh = lambda a: a.reshape(b, s, a.shape[-1])
            o1, o2 = _attn_ab(sink_b[j].astype(F32), sh(qa), sh(ka), sh(va),
                              sh(qb), sh(kb), sh(vb))
            o1, o2 = o1.reshape(t, 512), o2.reshape(t, 512)
            w_out = w_out_ab_h
        else:
            w_in = w_in_cd[j]
            w_in = jnp.concatenate(
                [w_in[:, :C_Q_RANK + C_KV_RANK],
                 _pad_cols(jnp.pad(w_in[:, 384:416], ((0, 0), (C_NOPE, 0))), LANES),
                 w_in[:, 416:]], axis=1).astype(BF16)
            wuq = w_uq[j].reshape(C_Q_RANK, C_HEADS, C_NOPE + C_ROPE)
            wuq = jnp.pad(wuq, ((0, 0), (0, 0), (0, LANES - C_NOPE - C_ROPE)))
            wuq = wuq.reshape(C_Q_RANK, C_HEADS * LANES).astype(BF16)
            wukv = w_ukv[j].reshape(C_KV_RANK, C_HEADS, C_NOPE + C_V)
            wuk = jnp.pad(wukv[:, :, :C_NOPE], ((0, 0), (0, 0), (0, LANES - C_NOPE)))
            wuk = wuk.reshape(C_KV_RANK, C_HEADS * LANES).astype(BF16)
            wuv = wukv[:, :, C_NOPE:].reshape(C_KV_RANK, C_HEADS * C_V).astype(BF16)
            qc, kc, vc, qd, kd, vd = _inproj_cd(
                xf, row2(g_mix[i]), w_in, row2(g_cq[j]), row2(g_ckv[j]), wuq, wuk, wuv, tabs_cd)
            sh = lambda a: a.reshape(b, s, a.shape[-1])
            o1 = _attn_c(sh(qc), sh(kc), sh(vc)).reshape(t, 512)
            o2 = _attn_d(sh(qd), sh(kd), sh(vd), bias_tabs, j).reshape(t, 512)
            w_out = w_out_cd_h

        xf = _layer_tail(xf, o1, o2, w_out, j, row2(g_xq[i]), w_xq_h, kv_all, w_xo_h,
                         row2(g_ffn[i]), w_gate_up_h, w_down_h, i, row2(g_final),
                         final_norm=(i == depth - 1))
    return xf.reshape(b, s, d)
```
